```python
import math
import jax, jax.numpy as jnp
from jax import lax
import numpy as np

D_MODEL = 2048
BATCH = 1
SEQ = 8192
DEPTH = 2
DEC_BATCH = 32
DEC_SEQ = 1
PAST_LEN = 8192
PAGE_SIZE = 128

HEAD_DIM = 128
H_A = 8
W_A = H_A * HEAD_DIM
Q_BLOCK = 128
SB_BIAS_INIT = -8.0
POOL_WINDOWS = (2, 4, 8, 16)
N_POOL_GROUPS = len(POOL_WINDOWS)
W_B = D_MODEL // 2
G_B = W_B // N_POOL_GROUPS
POOL_BUF = max(POOL_WINDOWS) - 1
W_AB = W_A + W_B
D_IN_AB = 3 * W_A + W_B + W_AB
CHUNK_C = 128
H_C = 8
W_C = D_MODEL
G_C = W_C // H_C
D_IN_C = 3 * W_C
N_AB_LAYERS = (DEPTH + 1) // 2
N_C_LAYERS = DEPTH // 2
EPS = 1e-6

kernel_name = 'stick_pool_gmlp_hybrid_step'


def rms_norm(x, gain):
    xf = x.astype(jnp.float32)
    xf = xf * lax.rsqrt(jnp.mean(xf * xf, axis=-1, keepdims=True) + EPS)
    return xf.astype(x.dtype) * gain


def stick_breaking(q, k, v, bias, q_pos, k_pos):
    z = jnp.einsum('bqhd,bkhd->bhqk', q.astype(jnp.float32), k.astype(jnp.float32)) / math.sqrt(HEAD_DIM)
    z = z + bias.astype(jnp.float32)[None, :, None, None]
    valid = k_pos[None, :] < q_pos[:, None]
    sp = jnp.where(valid, jax.nn.softplus(z), 0.0)
    between = lax.cumsum(sp, axis=3, reverse=True) - sp
    w = jnp.where(valid, jnp.exp(jax.nn.log_sigmoid(z) - between), 0.0)
    out = jnp.einsum('bhqk,bkhd->bqhd', w, v.astype(jnp.float32))
    return out.astype(v.dtype)


def stick_breaking_prompt(q, k, v, bias):
    B, S = q.shape[:2]
    nb = S // Q_BLOCK
    qb = q.reshape(B, nb, Q_BLOCK, H_A, HEAD_DIM).swapaxes(0, 1)
    k_pos = jnp.arange(S)
    q_pos = k_pos.reshape(nb, Q_BLOCK)
    out = lax.map(lambda a: stick_breaking(a[0], k, v, bias, a[1], k_pos), (qb, q_pos))
    return out.swapaxes(0, 1).reshape(B, S, W_A)


def multi_scale_pool(p, prev, pos0, w_pool, scale):
    T = p.shape[1]
    xp = jnp.concatenate([prev, p], axis=1)
    c = jnp.cumsum(xp.astype(jnp.float32), axis=1)
    c = jnp.concatenate([jnp.zeros_like(c[:, :1]), c], axis=1)
    end = c[:, POOL_BUF + 1:]
    pos = pos0 + jnp.arange(T)
    pf = p.astype(jnp.float32)
    outs = []
    for g, w in enumerate(POOL_WINDOWS):
        sl = slice(g * G_B, (g + 1) * G_B)
        window_sum = end[..., sl] - c[:, POOL_BUF + 1 - w: POOL_BUF + 1 - w + T, sl]
        count = jnp.minimum(w, pos + 1).astype(jnp.float32)[None, :, None]
        outs.append(window_sum / count - pf[..., sl])
    pooled = jnp.stack(outs, axis=2)
    mixed = jnp.einsum('btgc,gcd->btgd', pooled, w_pool.astype(jnp.float32))
    out = mixed.reshape(p.shape).astype(p.dtype) * scale
    return out, xp[:, -POOL_BUF:]


def chunk_spatial_gate(u, v, w_s, b_s):
    B, T, _ = u.shape
    n = -(-T // CHUNK_C)
    pad = n * CHUNK_C - T
    vc = jnp.pad(v, ((0, 0), (0, pad), (0, 0))).reshape(B, n, CHUNK_C, H_C, G_C)
    mask = jnp.tril(jnp.ones((CHUNK_C, CHUNK_C), dtype=bool))
    ws = jnp.where(mask, w_s, 0)
    mixed = jnp.einsum('hij,bnjhc->bnihc', ws, vc) + b_s.T[:, :, None]
    mixed = mixed.reshape(B, n * CHUNK_C, W_C)[:, :T]
    return u * mixed


def ab_project(h, w_in, q_gain, k_gain):
    B, T = h.shape[:2]
    z = h @ w_in
    q, k, v, p, g = jnp.split(z, [W_A, 2 * W_A, 3 * W_A, 3 * W_A + W_B], axis=-1)
    q = rms_norm(q.reshape(B, T, H_A, HEAD_DIM), q_gain)
    k = rms_norm(k.reshape(B, T, H_A, HEAD_DIM), k_gain)
    v = v.reshape(B, T, H_A, HEAD_DIM)
    return q, k, v, p, g


def gated_out(o, g, w_out):
    return (o * jax.nn.silu(g)) @ w_out


def c_mix(h, w_in, v_gain, w_s, b_s, w_out):
    u, v, g = jnp.split(h @ w_in, 3, axis=-1)
    u = jax.nn.gelu(u)
    v = rms_norm(jax.nn.gelu(v), v_gain)
    o = chunk_spatial_gate(u, v, w_s, b_s)
    return gated_out(o, g, w_out), v


def setup_inputs(seed: int = 0) -> dict:
    key = jax.random.key(seed)
    ks = jax.random.split(key, 20)
    n_pages = PAST_LEN // PAGE_SIZE
    n_pool_pages = (DEC_BATCH * n_pages * 5 + 3) // 4
    nrm = jax.random.normal
    perm = jax.random.permutation(ks[0], n_pool_pages)
    page_table = perm[:DEC_BATCH * n_pages].reshape(DEC_BATCH, n_pages).astype(jnp.int32)
    return {
        'x_prompt': nrm(ks[1], (BATCH, SEQ, D_MODEL), jnp.float32),
        'x_sample': nrm(ks[2], (DEC_BATCH, DEC_SEQ, D_MODEL), jnp.float32),
        'cache_k': nrm(ks[3], (N_AB_LAYERS, n_pool_pages, PAGE_SIZE, H_A, HEAD_DIM), jnp.float32),
        'cache_v': nrm(ks[4], (N_AB_LAYERS, n_pool_pages, PAGE_SIZE, H_A, HEAD_DIM), jnp.float32),
        'state_pool': nrm(ks[5], (N_AB_LAYERS, DEC_BATCH, POOL_BUF, W_B), jnp.float32),
        'page_table': page_table,
        'norm_gain': 1.0 + 0.05 * nrm(ks[6], (DEPTH, D_MODEL), jnp.float32),
        'w_in_ab': nrm(ks[7], (N_AB_LAYERS, D_MODEL, D_IN_AB), jnp.float32) * D_MODEL ** -0.5,
        'q_gain': 1.0 + 0.05 * nrm(ks[8], (N_AB_LAYERS, HEAD_DIM), jnp.float32),
        'k_gain': 1.0 + 0.05 * nrm(ks[9], (N_AB_LAYERS, HEAD_DIM), jnp.float32),
        'sb_bias': SB_BIAS_INIT + 0.1 * nrm(ks[18], (N_AB_LAYERS, H_A), jnp.float32),
        'w_pool': nrm(ks[10], (N_AB_LAYERS, N_POOL_GROUPS, G_B, G_B), jnp.float32) * G_B ** -0.5,
        'pool_scale': 1.0 + 0.05 * nrm(ks[11], (N_AB_LAYERS, W_B), jnp.float32),
        'w_out_ab': nrm(ks[12], (N_AB_LAYERS, W_AB, D_MODEL), jnp.float32) * W_AB ** -0.5,
        'w_in_c': nrm(ks[13], (N_C_LAYERS, D_MODEL, D_IN_C), jnp.float32) * D_MODEL ** -0.5,
        'v_gain': 1.0 + 0.05 * nrm(ks[14], (N_C_LAYERS, W_C), jnp.float32),
        'w_spatial': nrm(ks[15], (N_C_LAYERS, H_C, CHUNK_C, CHUNK_C), jnp.float32) * CHUNK_C ** -0.5,
        'b_spatial': 1.0 + 0.1 * nrm(ks[16], (N_C_LAYERS, H_C, CHUNK_C), jnp.float32),
        'w_out_c': nrm(ks[17], (N_C_LAYERS, W_C, D_MODEL), jnp.float32) * W_C ** -0.5,
    }


def reference(x_prompt, x_sample, cache_k, cache_v, state_pool, page_table, norm_gain, w_in_ab, q_gain, k_gain, sb_bias, w_pool, pool_scale, w_out_ab, w_in_c, v_gain, w_spatial, b_spatial, w_out_c):
    xp, xs = x_prompt, x_sample
    Bp, S = xp.shape[:2]
    Bs, T = xs.shape[:2]
    past = page_table.shape[1] * cache_k.shape[2]
    last_chunk = ((S - 1) // CHUNK_C) * CHUNK_C
    kp_l, vp_l, ks_l, vs_l, pp_l, ps_l, cvp_l, cvs_l = [], [], [], [], [], [], [], []
    for layer in range(DEPTH):
        i = layer // 2
        hp = rms_norm(xp, norm_gain[layer])
        hs = rms_norm(xs, norm_gain[layer])
        if layer % 2 == 0:
            q, k, v, p, g = ab_project(hp, w_in_ab[i], q_gain[i], k_gain[i])
            o_a = stick_breaking_prompt(q, k, v, sb_bias[i])
            o_b, pool_new = multi_scale_pool(p, jnp.zeros((Bp, POOL_BUF, W_B), p.dtype), 0, w_pool[i], pool_scale[i])
            xp = xp + gated_out(jnp.concatenate([o_a, o_b], axis=-1), g, w_out_ab[i])
            kp_l.append(k)
            vp_l.append(v)
            pp_l.append(pool_new)
            q, k, v, p, g = ab_project(hs, w_in_ab[i], q_gain[i], k_gain[i])
            k_past = cache_k[i][page_table].reshape(Bs, past, H_A, HEAD_DIM)
            v_past = cache_v[i][page_table].reshape(Bs, past, H_A, HEAD_DIM)
            k_all = jnp.concatenate([k_past, k], axis=1)
            v_all = jnp.concatenate([v_past, v], axis=1)
            o_a = stick_breaking(q, k_all, v_all, sb_bias[i], past + jnp.arange(T), jnp.arange(past + T)).reshape(Bs, T, W_A)
            o_b, pool_new = multi_scale_pool(p, state_pool[i], past, w_pool[i], pool_scale[i])
            xs = xs + gated_out(jnp.concatenate([o_a, o_b], axis=-1), g, w_out_ab[i])
            ks_l.append(k)
            vs_l.append(v)
            ps_l.append(pool_new)
        else:
            yp, vrows_p = c_mix(hp, w_in_c[i], v_gain[i], w_spatial[i], b_spatial[i], w_out_c[i])
            xp = xp + yp
            cvp_l.append(vrows_p[:, last_chunk:])
            ys, vrows_s = c_mix(hs, w_in_c[i], v_gain[i], w_spatial[i], b_spatial[i], w_out_c[i])
            xs = xs + ys
            cvs_l.append(vrows_s)
    return (xp, xs, jnp.stack(kp_l), jnp.stack(vp_l), jnp.stack(ks_l), jnp.stack(vs_l), jnp.stack(pp_l), jnp.stack(ps_l), jnp.stack(cvp_l), jnp.stack(cvs_l))
```

```python
import functools
import math

import jax
import jax.numpy as jnp
from jax import lax
from jax.experimental import pallas as pl
from jax.experimental.pallas import tpu as pltpu

F32 = jnp.float32
BF16 = jnp.bfloat16

LANES = 128
HEAD_DIM = 128
N_HEADS = 8
W_A = N_HEADS * HEAD_DIM
POOL_WINDOWS = (2, 4, 8, 16)
G_B = 256
W_B = G_B * len(POOL_WINDOWS)
POOL_BUF = max(POOL_WINDOWS) - 1
HALO = 16
CHUNK_C = 128
H_C = 8
EPS = 1e-6
SM_SCALE = 1.0 / math.sqrt(HEAD_DIM)
GELU_C = math.sqrt(2.0 / math.pi)
VMEM_LIMIT = 56 * 1024 * 1024


def _params(semantics):
    return pltpu.CompilerParams(dimension_semantics=semantics, vmem_limit_bytes=VMEM_LIMIT)


def _softplus(z):
    return jnp.maximum(z, 0.0) + jnp.log1p(jnp.exp(-jnp.abs(z)))


def _silu(g):
    return g * jax.nn.sigmoid(g)


def _gelu_tanh(x):
    return 0.5 * x * (1.0 + jnp.tanh(GELU_C * (x + 0.044715 * (x * x * x))))


def _rms_rows(xf, gain):
    ms = jnp.mean(xf * xf, axis=-1, keepdims=True)
    return xf * lax.rsqrt(ms + EPS) * gain


def _group_col(j, first, count):
    return jnp.clip(j - first, 0, count - 1)


def _head_norm(z, gain):
    cols = []
    for c in range(z.shape[1] // HEAD_DIM):
        zc = z[:, c * HEAD_DIM:(c + 1) * HEAD_DIM]
        cols.append(_rms_rows(zc, gain))
    return jnp.concatenate(cols, axis=1) if len(cols) > 1 else cols[0]


def _ab_in_kernel(x_ref, gain_ref, w_ref, qg_ref, kg_ref,
                  q_ref, k_ref, kb_ref, v_ref, vb_ref, p_ref, g_ref, h_ref, *, nb):
    j = pl.program_id(1)

    @pl.when(j == 0)
    def _():
        h_ref[...] = _rms_rows(x_ref[...], gain_ref[...]).astype(BF16)

    z = jnp.dot(h_ref[...], w_ref[...], preferred_element_type=F32)

    @pl.when(j < nb)
    def _():
        q_ref[...] = _head_norm(z, qg_ref[...]).astype(BF16)

    @pl.when((j >= nb) & (j < 2 * nb))
    def _():
        kn = _head_norm(z, kg_ref[...])
        k_ref[...] = kn
        kb_ref[...] = kn.astype(BF16)

    @pl.when((j >= 2 * nb) & (j < 3 * nb))
    def _():
        v_ref[...] = z
        vb_ref[...] = z.astype(BF16)

    @pl.when((j >= 3 * nb) & (j < 4 * nb))
    def _():
        p_ref[...] = z

    @pl.when(j >= 4 * nb)
    def _():
        g_ref[...] = z


def _ab_in_proj(x, gain, w, q_gain, k_gain, *, tm, tn):
    m, d = x.shape
    n = w.shape[1]
    nb = W_A // tn
    grid = (m // tm, n // tn)

    def grp(first, count):
        return pl.BlockSpec((tm, tn), lambda i, j: (i, _group_col(j, first, count)))

    out_shape = (
        jax.ShapeDtypeStruct((m, W_A), BF16),
        jax.ShapeDtypeStruct((m, W_A), F32),
        jax.ShapeDtypeStruct((m, W_A), BF16),
        jax.ShapeDtypeStruct((m, W_A), F32),
        jax.ShapeDtypeStruct((m, W_A), BF16),
        jax.ShapeDtypeStruct((m, W_B), F32),
        jax.ShapeDtypeStruct((m, W_A + W_B), F32),
    )
    out_specs = (grp(0, nb), grp(nb, nb), grp(nb, nb), grp(2 * nb, nb), grp(2 * nb, nb),
                 grp(3 * nb, nb), grp(4 * nb, 2 * nb))
    return pl.pallas_call(
        functools.partial(_ab_in_kernel, nb=nb),
        out_shape=out_shape,
        grid=grid,
        in_specs=[
            pl.BlockSpec((tm, d), lambda i, j: (i, 0)),
            pl.BlockSpec((1, d), lambda i, j: (0, 0)),
            pl.BlockSpec((d, tn), lambda i, j: (0, j)),
            pl.BlockSpec((1, HEAD_DIM), lambda i, j: (0, 0)),
            pl.BlockSpec((1, HEAD_DIM), lambda i, j: (0, 0)),
        ],
        out_specs=out_specs,
        scratch_shapes=[pltpu.VMEM((tm, d), BF16)],
        compiler_params=_params(("parallel", "arbitrary")),
        name="ab_in_proj",
    )(x, gain.reshape(1, d), w, q_gain.reshape(1, HEAD_DIM), k_gain.reshape(1, HEAD_DIM))


def _strict_upper(n):
    r = lax.broadcasted_iota(jnp.int32, (n, n), 0)
    c = lax.broadcasted_iota(jnp.int32, (n, n), 1)
    return jnp.where(r > c, 1.0, 0.0).astype(BF16)


def _sb_prompt_kernel(bias_ref, q_ref, k_ref, v_ref, o_ref, acc_ref, r_ref, *, tq, tk):
    h = pl.program_id(0)
    i = pl.program_id(1)
    bias = bias_ref[h]
    q = q_ref[...]
    upper = _strict_upper(tk)
    acc_ref[...] = jnp.zeros_like(acc_ref)
    r_ref[...] = jnp.zeros_like(r_ref)
    blocks_per_q = tq // tk

    def tile(kb, masked):
        ks = pl.multiple_of(kb * tk, tk)
        kblk = k_ref[pl.ds(ks, tk), :]
        z = lax.dot_general(q, kblk, (((1,), (1,)), ((), ())), preferred_element_type=F32)
        z = z * SM_SCALE + bias
        sp = _softplus(z)
        if masked:
            t_pos = i * tq + lax.broadcasted_iota(jnp.int32, (tq, tk), 0)
            s_pos = ks + lax.broadcasted_iota(jnp.int32, (tq, tk), 1)
            valid = s_pos < t_pos
            sp = jnp.where(valid, sp, 0.0)
        later = jnp.dot(sp.astype(BF16), upper, preferred_element_type=F32)
        w = jnp.exp(z - sp - later - r_ref[...])
        if masked:
            w = jnp.where(valid, w, 0.0)
        acc_ref[...] += jnp.dot(w.astype(BF16), v_ref[pl.ds(ks, tk), :],
                                preferred_element_type=F32)
        r_ref[...] += jnp.sum(sp, axis=-1, keepdims=True)

    for d in reversed(range(blocks_per_q)):
        tile(i * blocks_per_q + d, True)

    n_full = i * blocks_per_q

    def body(it, carry):
        tile(n_full - 1 - it, False)
        return carry

    lax.fori_loop(0, n_full, body, 0)
    o_ref[...] = acc_ref[...]


def _sb_prompt(q, k, v, bias, *, tq, tk):
    s = q.shape[0]
    grid = (N_HEADS, s // tq)
    return pl.pallas_call(
        functools.partial(_sb_prompt_kernel, tq=tq, tk=tk),
        out_shape=jax.ShapeDtypeStruct((s, W_A), F32),
        grid=grid,
        in_specs=[
            pl.BlockSpec(memory_space=pltpu.SMEM),
            pl.BlockSpec((tq, HEAD_DIM), lambda h, i: (i, h)),
            pl.BlockSpec((s, HEAD_DIM), lambda h, i: (0, h)),
            pl.BlockSpec((s, HEAD_DIM), lambda h, i: (0, h)),
        ],
        out_specs=pl.BlockSpec((tq, HEAD_DIM), lambda h, i: (i, h)),
        scratch_shapes=[pltpu.VMEM((tq, HEAD_DIM), F32), pltpu.VMEM((tq, 1), F32)],
        compiler_params=_params(("parallel", "arbitrary")),
        name="sb_prompt",
    )(bias, q, k, v)


def _suffix_and_total(n):
    r = lax.broadcasted_iota(jnp.int32, (n, 2 * n), 0)
    c = lax.broadcasted_iota(jnp.int32, (n, 2 * n), 1)
    return jnp.where((r > c) | (c >= n), 1.0, 0.0).astype(BF16)


def _sb_decode_kernel(pt_ref, q_ref, bias_ref, k_ref, v_ref, o_ref, acc_ref, r_ref, *, page):
    del pt_ref
    pi = pl.program_id(1)
    flat = page * N_HEADS
    tiles = flat // LANES

    @pl.when(pi == 0)
    def _():
        acc_ref[...] = jnp.zeros_like(acc_ref)
        r_ref[...] = jnp.zeros_like(r_ref)

    km = k_ref[0].reshape(flat, HEAD_DIM).astype(BF16)
    vm = v_ref[0].reshape(flat, HEAD_DIM).astype(BF16)
    z = lax.dot_general(q_ref[0].astype(BF16), km, (((1,), (1,)), ((), ())),
                        preferred_element_type=F32)
    z = z * SM_SCALE + bias_ref[...]
    lane = lax.broadcasted_iota(jnp.int32, (N_HEADS, flat), 1)
    row = lax.broadcasted_iota(jnp.int32, (N_HEADS, flat), 0)
    own = (lane % N_HEADS) == row
    sp = jnp.where(own, _softplus(z), 0.0)
    stacked = jnp.concatenate([sp[:, j * LANES:(j + 1) * LANES] for j in range(tiles)], axis=0)
    hi = stacked.astype(BF16)
    lo = (stacked - hi.astype(F32)).astype(BF16)
    sums = jnp.dot(jnp.concatenate([hi, lo], axis=0), _suffix_and_total(LANES),
                   preferred_element_type=F32)
    sums = sums[:tiles * N_HEADS] + sums[tiles * N_HEADS:]
    run = jnp.broadcast_to(r_ref[...], (N_HEADS, LANES))
    ws = [None] * tiles
    for j in reversed(range(tiles)):
        rows = slice(j * N_HEADS, (j + 1) * N_HEADS)
        cols = slice(j * LANES, (j + 1) * LANES)
        later = sums[rows, :LANES] + run
        ws[j] = jnp.where(own[:, cols], jnp.exp(z[:, cols] - sp[:, cols] - later), 0.0)
        run = run + sums[rows, LANES:]
    r_ref[...] = run[:, :1]
    w = jnp.concatenate(ws, axis=1).astype(BF16)
    acc_ref[...] += jnp.dot(w, vm, preferred_element_type=F32)

    @pl.when(pi == pl.num_programs(1) - 1)
    def _():
        o_ref[0] = acc_ref[...]


def _sb_decode(q, cache_k, cache_v, page_table, bias):
    b = q.shape[0]
    page = cache_k.shape[1]
    n_pages = page_table.shape[1]

    def page_map(s, pi, pt):
        return (pt[s, n_pages - 1 - pi], 0, 0, 0)

    grid_spec = pltpu.PrefetchScalarGridSpec(
        num_scalar_prefetch=1,
        grid=(b, n_pages),
        in_specs=[
            pl.BlockSpec((1, N_HEADS, HEAD_DIM), lambda s, pi, pt: (s, 0, 0)),
            pl.BlockSpec((N_HEADS, 1), lambda s, pi, pt: (0, 0)),
            pl.BlockSpec((1, page, N_HEADS, HEAD_DIM), page_map),
            pl.BlockSpec((1, page, N_HEADS, HEAD_DIM), page_map),
        ],
        out_specs=pl.BlockSpec((1, N_HEADS, HEAD_DIM), lambda s, pi, pt: (s, 0, 0)),
        scratch_shapes=[pltpu.VMEM((N_HEADS, HEAD_DIM), F32), pltpu.VMEM((N_HEADS, 1), F32)],
    )
    out = pl.pallas_call(
        functools.partial(_sb_decode_kernel, page=page),
        out_shape=jax.ShapeDtypeStruct((b, N_HEADS, HEAD_DIM), F32),
        grid_spec=grid_spec,
        compiler_params=_params(("parallel", "arbitrary")),
        name="sb_decode",
    )(page_table, q.reshape(b, N_HEADS, HEAD_DIM), bias.reshape(N_HEADS, 1), cache_k, cache_v)
    return out.reshape(b, W_A)


def _pool_mix_gate(window_sums, p, count_inv, o_a, g, wp_ref, scale_ref, a_ref):
    a_ref[:, :W_A] = (o_a * _silu(g[:, :W_A])).astype(BF16)
    for gi in range(len(POOL_WINDOWS)):
        cols = slice(gi * G_B, (gi + 1) * G_B)
        pooled = window_sums[gi] * count_inv[gi] - p[:, cols]
        mixed = jnp.dot(pooled.astype(BF16), wp_ref[gi], preferred_element_type=F32)
        o_b = mixed * scale_ref[:, cols]
        a_ref[:, W_A + gi * G_B:W_A + (gi + 1) * G_B] = (
            o_b * _silu(g[:, W_A + gi * G_B:W_A + (gi + 1) * G_B])).astype(BF16)


def _ab_mix_prompt_kernel(p_ref, halo_ref, oa_ref, g_ref, wp_ref, scale_ref, a_ref, buf_ref, *, tm):
    i = pl.program_id(0)
    buf_ref[:HALO, :] = jnp.where(i > 0, halo_ref[...], 0.0)
    buf_ref[HALO:, :] = p_ref[...]
    pos = i * tm + lax.broadcasted_iota(jnp.int32, (tm, 1), 0)
    sums, invs = [], []
    for gi, wnd in enumerate(POOL_WINDOWS):
        cols = slice(gi * G_B, (gi + 1) * G_B)
        acc = buf_ref[HALO:, cols]
        for back in range(1, wnd):
            acc = acc + buf_ref[HALO - back:HALO - back + tm, cols]
        sums.append(acc)
        invs.append(1.0 / jnp.minimum(wnd, pos + 1).astype(F32))
    _pool_mix_gate(sums, p_ref[...], invs, oa_ref[...], g_ref[...], wp_ref, scale_ref, a_ref)


def _ab_mix_prompt(p, o_a, g, w_pool, scale, *, tm):
    m = p.shape[0]
    halo_blocks = tm // HALO
    return pl.pallas_call(
        functools.partial(_ab_mix_prompt_kernel, tm=tm),
        out_shape=jax.ShapeDtypeStruct((m, W_A + W_B), BF16),
        grid=(m // tm,),
        in_specs=[
            pl.BlockSpec((tm, W_B), lambda i: (i, 0)),
            pl.BlockSpec((HALO, W_B), lambda i: (jnp.maximum(i * halo_blocks - 1, 0), 0)),
            pl.BlockSpec((tm, W_A), lambda i: (i, 0)),
            pl.BlockSpec((tm, W_A + W_B), lambda i: (i, 0)),
            pl.BlockSpec((len(POOL_WINDOWS), G_B, G_B), lambda i: (0, 0, 0)),
            pl.BlockSpec((1, W_B), lambda i: (0, 0)),
        ],
        out_specs=pl.BlockSpec((tm, W_A + W_B), lambda i: (i, 0)),
        scratch_shapes=[pltpu.VMEM((tm + HALO, W_B), F32)],
        compiler_params=_params(("parallel",)),
        name="ab_mix_prompt",
    )(p, p, o_a, g, w_pool, scale.reshape(1, W_B))


def _ab_mix_sample_kernel(p_ref, prev_ref, oa_ref, g_ref, wp_ref, scale_ref, a_ref):
    p = p_ref[...]
    sums, invs = [], []
    for gi, wnd in enumerate(POOL_WINDOWS):
        cols = slice(gi * G_B, (gi + 1) * G_B)
        acc = p[:, cols]
        for back in range(1, wnd):
            acc = acc + prev_ref[POOL_BUF - back, :, cols]
        sums.append(acc)
        invs.append(1.0 / wnd)
    _pool_mix_gate(sums, p, invs, oa_ref[...], g_ref[...], wp_ref, scale_ref, a_ref)


def _ab_mix_sample(p, prev_t, o_a, g, w_pool, scale):
    m = p.shape[0]
    return pl.pallas_call(
        _ab_mix_sample_kernel,
        out_shape=jax.ShapeDtypeStruct((m, W_A + W_B), BF16),
        name="ab_mix_sample",
        compiler_params=pltpu.CompilerParams(vmem_limit_bytes=VMEM_LIMIT),
    )(p, prev_t, o_a, g, w_pool, scale.reshape(1, W_B))


def _out_proj_kernel(a_ref, w_ref, x_ref, y_ref):
    y_ref[...] = x_ref[...] + jnp.dot(a_ref[...], w_ref[...], preferred_element_type=F32)


def _out_proj(a, w, x, *, tm, tn):
    m, kdim = a.shape
    n = w.shape[1]
    return pl.pallas_call(
        _out_proj_kernel,
        out_shape=jax.ShapeDtypeStruct((m, n), F32),
        grid=(m // tm, n // tn),
        in_specs=[
            pl.BlockSpec((tm, kdim), lambda i, j: (i, 0)),
            pl.BlockSpec((kdim, tn), lambda i, j: (0, j)),
            pl.BlockSpec((tm, tn), lambda i, j: (i, j)),
        ],
        out_specs=pl.BlockSpec((tm, tn), lambda i, j: (i, j)),
        compiler_params=_params(("parallel", "parallel")),
        name="out_proj",
    )(a, w, x)


def _c_in_kernel(x_ref, gain_ref, w_ref, u_ref, v_ref, g_ref, h_ref, *, nb):
    j = pl.program_id(1)

    @pl.when(j == 0)
    def _():
        h_ref[...] = _rms_rows(x_ref[...], gain_ref[...]).astype(BF16)

    z = jnp.dot(h_ref[...], w_ref[...], preferred_element_type=F32)

    @pl.when(j < nb)
    def _():
        u_ref[...] = _gelu_tanh(z)

    @pl.when((j >= nb) & (j < 2 * nb))
    def _():
        v_ref[...] = _gelu_tanh(z)

    @pl.when(j >= 2 * nb)
    def _():
        g_ref[...] = z


def _c_in_proj(x, gain, w, *, tm, tn):
    m, d = x.shape
    n = w.shape[1]
    wc = n // 3
    nb = wc // tn

    def grp(first):
        return pl.BlockSpec((tm, tn), lambda i, j: (i, _group_col(j, first, nb)))

    return pl.pallas_call(
        functools.partial(_c_in_kernel, nb=nb),
        out_shape=tuple(jax.ShapeDtypeStruct((m, wc), F32) for _ in range(3)),
        grid=(m // tm, n // tn),
        in_specs=[
            pl.BlockSpec((tm, d), lambda i, j: (i, 0)),
            pl.BlockSpec((1, d), lambda i, j: (0, 0)),
            pl.BlockSpec((d, tn), lambda i, j: (0, j)),
        ],
        out_specs=(grp(0), grp(nb), grp(2 * nb)),
        scratch_shapes=[pltpu.VMEM((tm, d), BF16)],
        compiler_params=_params(("parallel", "arbitrary")),
        name="c_in_proj",
    )(x, gain.reshape(1, d), w)


def _c_gate_prompt_kernel(u_ref, vg_ref, g_ref, vgain_ref, ws_ref, bst_ref, a_ref, vrow_ref, *,
                          chunks):
    gc = u_ref.shape[1] // H_C
    r = lax.broadcasted_iota(jnp.int32, (CHUNK_C, CHUNK_C), 0)
    c = lax.broadcasted_iota(jnp.int32, (CHUNK_C, CHUNK_C), 1)
    lower = r >= c
    for ch in range(chunks):
        rows = slice(ch * CHUNK_C, (ch + 1) * CHUNK_C)
        v = _rms_rows(vg_ref[rows, :], vgain_ref[...])
        if ch == chunks - 1:
            vrow_ref[...] = v
        for hd in range(H_C):
            cols = slice(hd * gc, (hd + 1) * gc)
            ws = jnp.where(lower, ws_ref[hd], 0.0).astype(BF16)
            mixed = jnp.dot(ws, v[:, cols].astype(BF16), preferred_element_type=F32)
            mixed = mixed + bst_ref[:, hd:hd + 1]
            a_ref[rows, cols] = (u_ref[rows, cols] * mixed * _silu(g_ref[rows, cols])).astype(BF16)


def _c_gate_prompt(u, vg, g, v_gain, w_s, b_s, *, chunks):
    m, wc = u.shape
    tm = chunks * CHUNK_C
    row = pl.BlockSpec((tm, wc), lambda i: (i, 0))
    return pl.pallas_call(
        functools.partial(_c_gate_prompt_kernel, chunks=chunks),
        out_shape=(jax.ShapeDtypeStruct((m, wc), BF16),
                   jax.ShapeDtypeStruct((CHUNK_C, wc), F32)),
        grid=(m // tm,),
        in_specs=[
            row, row, row,
            pl.BlockSpec((1, wc), lambda i: (0, 0)),
            pl.BlockSpec((H_C, CHUNK_C, CHUNK_C), lambda i: (0, 0, 0)),
            pl.BlockSpec((CHUNK_C, H_C), lambda i: (0, 0)),
        ],
        out_specs=(row, pl.BlockSpec((CHUNK_C, wc), lambda i: (0, 0))),
        compiler_params=_params(("arbitrary",)),
        name="c_gate_prompt",
    )(u, vg, g, v_gain.reshape(1, wc), w_s, b_s.T)


def _c_gate_sample_kernel(u_ref, vg_ref, g_ref, vgain_ref, w00_ref, b0_ref, a_ref, vrow_ref):
    v = _rms_rows(vg_ref[...], vgain_ref[...])
    vrow_ref[...] = v
    mixed = w00_ref[...] * v + b0_ref[...]
    a_ref[...] = (u_ref[...] * mixed * _silu(g_ref[...])).astype(BF16)


def _c_gate_sample(u, vg, g, v_gain, w_s, b_s):
    m, wc = u.shape
    gc = wc // H_C
    w00 = jnp.repeat(w_s[:, 0, 0], gc).reshape(1, wc)
    b0 = jnp.repeat(b_s[:, 0], gc).reshape(1, wc)
    return pl.pallas_call(
        _c_gate_sample_kernel,
        out_shape=(jax.ShapeDtypeStruct((m, wc), BF16), jax.ShapeDtypeStruct((m, wc), F32)),
        name="c_gate_sample",
    )(u, vg, g, v_gain.reshape(1, wc), w00, b0)


def kernel(x_prompt, x_sample, cache_k, cache_v, state_pool, page_table, norm_gain, w_in_ab, q_gain,
           k_gain, sb_bias, w_pool, pool_scale, w_out_ab, w_in_c, v_gain, w_spatial, b_spatial,
           w_out_c):
    bp, s, d = x_prompt.shape
    bs, t, _ = x_sample.shape
    assert bp == 1 and t == 1
    xp = x_prompt.reshape(s, d)
    xs = x_sample.reshape(bs, d)

    w_in0 = w_in_ab[0].astype(BF16)
    w_pool0 = w_pool[0].astype(BF16)
    w_out0 = w_out_ab[0].astype(BF16)
    w_in1 = w_in_c[0].astype(BF16)
    w_out1 = w_out_c[0].astype(BF16)

    q, k, kb, v, vb, p, g = _ab_in_proj(xp, norm_gain[0], w_in0, q_gain[0], k_gain[0], tm=512, tn=512)
    o_a = _sb_prompt(q, kb, vb, sb_bias[0], tq=512, tk=256)
    a = _ab_mix_prompt(p, o_a, g, w_pool0, pool_scale[0], tm=256)
    xp1 = _out_proj(a, w_out0, xp, tm=512, tn=1024)
    new_k_p = k.reshape(1, bp, s, N_HEADS, HEAD_DIM)
    new_v_p = v.reshape(1, bp, s, N_HEADS, HEAD_DIM)
    new_pool_p = p[s - POOL_BUF:].reshape(1, bp, POOL_BUF, W_B)

    qs, ks, _, vs, _, ps, gs = _ab_in_proj(xs, norm_gain[0], w_in0, q_gain[0], k_gain[0],
                                           tm=bs, tn=512)
    oa_s = _sb_decode(qs.astype(F32), cache_k[0], cache_v[0], page_table, sb_bias[0])
    prev_t = jnp.swapaxes(state_pool[0], 0, 1)
    a_s = _ab_mix_sample(ps, prev_t, oa_s, gs, w_pool0, pool_scale[0])
    xs1 = _out_proj(a_s, w_out0, xs, tm=bs, tn=1024)
    new_k_s = ks.reshape(1, bs, t, N_HEADS, HEAD_DIM)
    new_v_s = vs.reshape(1, bs, t, N_HEADS, HEAD_DIM)
    new_pool_s = jnp.concatenate([state_pool[0][:, 1:], ps[:, None, :]], axis=1)[None]

    u, vg, g1 = _c_in_proj(xp1, norm_gain[1], w_in1, tm=512, tn=512)
    a1, v_last = _c_gate_prompt(u, vg, g1, v_gain[0], w_spatial[0], b_spatial[0], chunks=2)
    yp = _out_proj(a1, w_out1, xp1, tm=512, tn=1024)

    us, vgs, g1s = _c_in_proj(xs1, norm_gain[1], w_in1, tm=bs, tn=512)
    a1s, v_s = _c_gate_sample(us, vgs, g1s, v_gain[0], w_spatial[0], b_spatial[0])
    ys = _out_proj(a1s, w_out1, xs1, tm=bs, tn=1024)

    return (yp.reshape(bp, s, d), ys.reshape(bs, t, d), new_k_p, new_v_p, new_k_s, new_v_s,
            new_pool_p, new_pool_s, v_last.reshape(1, bp, CHUNK_C, d), v_s.reshape(1, bs, t, d))
```

```python
import functools
import math

import jax
import jax.numpy as jnp
from jax import lax
from jax.experimental import pallas as pl
from jax.experimental.pallas import tpu as pltpu

F32 = jnp.float32
BF16 = jnp.bfloat16

LANES = 128
HEAD_DIM = 128
N_HEADS = 8
W_A = N_HEADS * HEAD_DIM
POOL_WINDOWS = (2, 4, 8, 16)
G_B = 256
W_B = G_B * len(POOL_WINDOWS)
POOL_BUF = max(POOL_WINDOWS) - 1
HALO = 16
CHUNK_C = 128
H_C = 8
EPS = 1e-6
SM_SCALE = 1.0 / math.sqrt(HEAD_DIM)
LOG2E = math.log2(math.e)
GELU_C = math.sqrt(2.0 / math.pi)
VMEM_LIMIT = 56 * 1024 * 1024


def _params(semantics):
    return pltpu.CompilerParams(dimension_semantics=semantics, vmem_limit_bytes=VMEM_LIMIT)


def _softplus_2(z2):
    neg_abs = lax.bitcast_convert_type(
        lax.bitcast_convert_type(z2, jnp.uint32) | jnp.uint32(0x80000000), F32)
    return jnp.maximum(z2, 0.0) + jnp.log(1.0 + jnp.exp2(neg_abs)) * LOG2E


def _silu(g):
    return g * jax.nn.sigmoid(g)


def _gelu_tanh(x):
    return 0.5 * x * (1.0 + jnp.tanh(GELU_C * (x + 0.044715 * (x * x * x))))


def _rms_rows(xf, gain):
    ms = jnp.mean(xf * xf, axis=-1, keepdims=True)
    return xf * lax.rsqrt(ms + EPS) * gain


def _group_col(j, first, count):
    return jnp.clip(j - first, 0, count - 1)


def _head_norm(z, gain):
    cols = []
    for c in range(z.shape[1] // HEAD_DIM):
        zc = z[:, c * HEAD_DIM:(c + 1) * HEAD_DIM]
        cols.append(_rms_rows(zc, gain))
    return jnp.concatenate(cols, axis=1) if len(cols) > 1 else cols[0]


def _ab_in_kernel(x_ref, gain_ref, w_ref, qg_ref, kg_ref,
                  q_ref, k_ref, kb_ref, v_ref, vb_ref, p_ref, g_ref, h_ref, *, nb):
    j = pl.program_id(1)

    @pl.when(j == 0)
    def _():
        h_ref[...] = _rms_rows(x_ref[...], gain_ref[...]).astype(BF16)

    z = jnp.dot(h_ref[...], w_ref[...], preferred_element_type=F32)

    @pl.when(j < nb)
    def _():
        q_ref[...] = (_head_norm(z, qg_ref[...]) * (SM_SCALE * LOG2E)).astype(BF16)

    @pl.when((j >= nb) & (j < 2 * nb))
    def _():
        kn = _head_norm(z, kg_ref[...])
        k_ref[...] = kn
        kb_ref[...] = kn.astype(BF16)

    @pl.when((j >= 2 * nb) & (j < 3 * nb))
    def _():
        v_ref[...] = z
        vb_ref[...] = z.astype(BF16)

    @pl.when((j >= 3 * nb) & (j < 4 * nb))
    def _():
        p_ref[...] = z

    @pl.when(j >= 4 * nb)
    def _():
        g_ref[...] = z


def _ab_in_proj(x, gain, w, q_gain, k_gain, *, tm, tn):
    m, d = x.shape
    n = w.shape[1]
    nb = W_A // tn
    grid = (m // tm, n // tn)

    def grp(first, count):
        return pl.BlockSpec((tm, tn), lambda i, j: (i, _group_col(j, first, count)))

    out_shape = (
        jax.ShapeDtypeStruct((m, W_A), BF16),
        jax.ShapeDtypeStruct((m, W_A), F32),
        jax.ShapeDtypeStruct((m, W_A), BF16),
        jax.ShapeDtypeStruct((m, W_A), F32),
        jax.ShapeDtypeStruct((m, W_A), BF16),
        jax.ShapeDtypeStruct((m, W_B), F32),
        jax.ShapeDtypeStruct((m, W_A + W_B), F32),
    )
    out_specs = (grp(0, nb), grp(nb, nb), grp(nb, nb), grp(2 * nb, nb), grp(2 * nb, nb),
                 grp(3 * nb, nb), grp(4 * nb, 2 * nb))
    return pl.pallas_call(
        functools.partial(_ab_in_kernel, nb=nb),
        out_shape=out_shape,
        grid=grid,
        in_specs=[
            pl.BlockSpec((tm, d), lambda i, j: (i, 0)),
            pl.BlockSpec((1, d), lambda i, j: (0, 0)),
            pl.BlockSpec((d, tn), lambda i, j: (0, j)),
            pl.BlockSpec((1, HEAD_DIM), lambda i, j: (0, 0)),
            pl.BlockSpec((1, HEAD_DIM), lambda i, j: (0, 0)),
        ],
        out_specs=out_specs,
        scratch_shapes=[pltpu.VMEM((tm, d), BF16)],
        compiler_params=_params(("parallel", "arbitrary")),
        name="ab_in_proj",
    )(x, gain.reshape(1, d), w, q_gain.reshape(1, HEAD_DIM), k_gain.reshape(1, HEAD_DIM))


def _strict_upper(n):
    r = lax.broadcasted_iota(jnp.int32, (n, n), 0)
    c = lax.broadcasted_iota(jnp.int32, (n, n), 1)
    return jnp.where(r > c, 1.0, 0.0).astype(BF16)


def _sb_prompt_kernel(bias_ref, q_ref, k_ref, v_ref, o_ref, acc_ref, r_ref, *, th, tk):
    h = pl.program_id(0)
    i = pl.program_id(1)
    bias2 = bias_ref[h] * LOG2E
    upper = _strict_upper(tk)
    acc_ref[...] = jnp.zeros_like(acc_ref)
    r_ref[...] = jnp.zeros_like(r_ref)
    nsub = th // tk

    def tile(half, kt, masked):
        rows = slice(half * th, (half + 1) * th)
        ks = pl.multiple_of(kt * th, th)
        z2 = lax.dot_general(q_ref[rows, :], k_ref[pl.ds(ks, th), :], (((1,), (1,)), ((), ())),
                             preferred_element_type=F32) + bias2
        sp2 = _softplus_2(z2)
        ls2 = z2 - sp2
        if masked:
            t_pos = lax.broadcasted_iota(jnp.int32, (th, th), 0)
            s_pos = lax.broadcasted_iota(jnp.int32, (th, th), 1)
            valid = s_pos < t_pos
            sp2 = jnp.where(valid, sp2, 0.0)
        stacked = jnp.concatenate([sp2[:, c * tk:(c + 1) * tk] for c in range(nsub)], axis=0)
        later = jnp.dot(stacked.astype(BF16), upper, preferred_element_type=F32)
        run = r_ref[rows, :]
        ws = [None] * nsub
        for c in reversed(range(nsub)):
            offs = jnp.concatenate([run] * (tk // LANES), axis=1)
            ws[c] = jnp.exp2(ls2[:, c * tk:(c + 1) * tk] - later[c * th:(c + 1) * th] - offs)
            total = jnp.sum(sp2[:, c * tk:(c + 1) * tk], axis=-1, keepdims=True)
            run = run + jnp.broadcast_to(total, (th, LANES))
        r_ref[rows, :] = run
        w = jnp.concatenate(ws, axis=1)
        if masked:
            w = jnp.where(valid, w, 0.0)
        acc_ref[rows, :] += jnp.dot(w.astype(BF16), v_ref[pl.ds(ks, th), :],
                                    preferred_element_type=F32)

    tile(1, 2 * i + 1, True)
    tile(0, 2 * i, True)
    tile(1, 2 * i, False)

    def body(it, carry):
        kt = 2 * i - 1 - 2 * it
        tile(0, kt, False)
        tile(1, kt, False)
        tile(0, kt - 1, False)
        tile(1, kt - 1, False)
        return carry

    lax.fori_loop(0, i, body, 0)
    o_ref[...] = acc_ref[...]


def _sb_prompt(q, k, v, bias, *, th, tk):
    s = q.shape[0]
    tq = 2 * th
    grid = (N_HEADS, s // tq)
    return pl.pallas_call(
        functools.partial(_sb_prompt_kernel, th=th, tk=tk),
        out_shape=jax.ShapeDtypeStruct((s, W_A), F32),
        grid=grid,
        in_specs=[
            pl.BlockSpec(memory_space=pltpu.SMEM),
            pl.BlockSpec((tq, HEAD_DIM), lambda h, i: (i, h)),
            pl.BlockSpec((s, HEAD_DIM), lambda h, i: (0, h)),
            pl.BlockSpec((s, HEAD_DIM), lambda h, i: (0, h)),
        ],
        out_specs=pl.BlockSpec((tq, HEAD_DIM), lambda h, i: (i, h)),
        scratch_shapes=[pltpu.VMEM((tq, HEAD_DIM), F32), pltpu.VMEM((tq, LANES), F32)],
        compiler_params=_params(("parallel", "arbitrary")),
        name="sb_prompt",
    )(bias, q, k, v)


def _suffix_and_total(n):
    r = lax.broadcasted_iota(jnp.int32, (n, 2 * n), 0)
    c = lax.broadcasted_iota(jnp.int32, (n, 2 * n), 1)
    return jnp.where((r > c) | (c >= n), 1.0, 0.0).astype(BF16)


def _sb_decode_kernel(pt_ref, q_ref, bias_ref, *refs, page, group):
    del pt_ref
    k_refs, v_refs = refs[:group], refs[group:2 * group]
    o_ref, acc_ref, r_ref = refs[2 * group:]
    pi = pl.program_id(1)
    flat = page * N_HEADS
    tiles = flat // LANES

    @pl.when(pi == 0)
    def _():
        acc_ref[...] = jnp.zeros_like(acc_ref)
        r_ref[...] = jnp.zeros_like(r_ref)

    lane = lax.broadcasted_iota(jnp.int32, (N_HEADS, LANES), 1)
    row = lax.broadcasted_iota(jnp.int32, (N_HEADS, LANES), 0)
    own_t = (lane % N_HEADS) == row
    suffix_total = _suffix_and_total(LANES)
    qb = q_ref[0].astype(BF16)
    bias2 = bias_ref[...] * LOG2E

    km = jnp.concatenate([k_refs[g][0].reshape(flat, HEAD_DIM).astype(BF16) for g in range(group)],
                         axis=0)
    z2 = lax.dot_general(qb, km, (((1,), (1,)), ((), ())),
                         preferred_element_type=F32)
    z2 = z2 + bias2
    sp2 = _softplus_2(z2)
    ls2 = z2 - sp2
    n_tiles = group * tiles
    order = [g * tiles + j for g in range(group) for j in reversed(range(tiles))]
    sp_t =[jnp.where(own_t, sp2[:, t * LANES:(t + 1) * LANES], 0.0) for t in order]
    stacked = jnp.concatenate(sp_t, axis=0)
    hi = stacked.astype(BF16)
    lo = (stacked - hi.astype(F32)).astype(BF16)
    s = jnp.dot(jnp.concatenate([hi, lo], axis=0), suffix_total, preferred_element_type=F32)
    s = s[:n_tiles * N_HEADS] + s[n_tiles * N_HEADS:]

    run = jnp.broadcast_to(r_ref[...], (N_HEADS, LANES))
    ws = [None] * n_tiles
    for n, t in enumerate(order):
        rows = slice(n * N_HEADS, (n + 1) * N_HEADS)
        later = s[rows, :LANES] + run
        ws[t] = jnp.where(own_t, jnp.exp2(ls2[:, t * LANES:(t + 1) * LANES] - later), 0.0)
        run = run + s[rows, LANES:]
    r_ref[...] = run[:, :1]
    w = jnp.concatenate(ws, axis=1).astype(BF16)
    vm = jnp.concatenate([v_refs[g][0].reshape(flat, HEAD_DIM).astype(BF16) for g in range(group)],
                         axis=0)
    acc = acc_ref[...] + jnp.dot(w, vm, preferred_element_type=F32)
    acc_ref[...] = acc

    @pl.when(pi == pl.num_programs(1) - 1)
    def _():
        o_ref[0] = acc


def _sb_decode(q, cache_k, cache_v, page_table, bias, *, group):
    b = q.shape[0]
    page = cache_k.shape[1]
    n_pages = page_table.shape[1]

    def page_spec(g):
        return pl.BlockSpec(
            (1, page, N_HEADS, HEAD_DIM),
            lambda s, pi, pt: (pt[s, n_pages - 1 - (pi * group + g)], 0, 0, 0))

    pages = [page_spec(g) for g in range(group)]
    grid_spec = pltpu.PrefetchScalarGridSpec(
        num_scalar_prefetch=1,
        grid=(b, n_pages // group),
        in_specs=[
            pl.BlockSpec((1, N_HEADS, HEAD_DIM), lambda s, pi, pt: (s, 0, 0)),
            pl.BlockSpec((N_HEADS, 1), lambda s, pi, pt: (0, 0)),
        ] + pages + pages,
        out_specs=pl.BlockSpec((1, N_HEADS, HEAD_DIM), lambda s, pi, pt: (s, 0, 0)),
        scratch_shapes=[pltpu.VMEM((N_HEADS, HEAD_DIM), F32), pltpu.VMEM((N_HEADS, 1), F32)],
    )
    out = pl.pallas_call(
        functools.partial(_sb_decode_kernel, page=page, group=group),
        out_shape=jax.ShapeDtypeStruct((b, N_HEADS, HEAD_DIM), F32),
        grid_spec=grid_spec,
        compiler_params=_params(("parallel", "arbitrary")),
        name="sb_decode",
    )(page_table, q.reshape(b, N_HEADS, HEAD_DIM), bias.reshape(N_HEADS, 1),
      *([cache_k] * group), *([cache_v] * group))
    return out.reshape(b, W_A)


def _pool_mix_gate(window_sums, p, count_inv, o_a, g, wp_ref, scale_ref, a_ref):
    a_ref[:, :W_A] = (o_a * _silu(g[:, :W_A])).astype(BF16)
    for gi in range(len(POOL_WINDOWS)):
        cols = slice(gi * G_B, (gi + 1) * G_B)
        pooled = window_sums[gi] * count_inv[gi] - p[:, cols]
        mixed = jnp.dot(pooled.astype(BF16), wp_ref[gi], preferred_element_type=F32)
        o_b = mixed * scale_ref[:, cols]
        a_ref[:, W_A + gi * G_B:W_A + (gi + 1) * G_B] = (
            o_b * _silu(g[:, W_A + gi * G_B:W_A + (gi + 1) * G_B])).astype(BF16)


def _ab_mix_prompt_kernel(p_ref, halo_ref, oa_ref, g_ref, wp_ref, scale_ref, a_ref, buf_ref, *, tm):
    i = pl.program_id(0)
    buf_ref[:HALO, :] = jnp.where(i > 0, halo_ref[...], 0.0)
    buf_ref[HALO:, :] = p_ref[...]
    pos = i * tm + lax.broadcasted_iota(jnp.int32, (tm, 1), 0)
    sums, invs = [], []
    for gi, wnd in enumerate(POOL_WINDOWS):
        cols = slice(gi * G_B, (gi + 1) * G_B)
        acc = buf_ref[HALO:, cols]
        for back in range(1, wnd):
            acc = acc + buf_ref[HALO - back:HALO - back + tm, cols]
        sums.append(acc)
        invs.append(1.0 / jnp.minimum(wnd, pos + 1).astype(F32))
    _pool_mix_gate(sums, p_ref[...], invs, oa_ref[...], g_ref[...], wp_ref, scale_ref, a_ref)


def _ab_mix_prompt(p, o_a, g, w_pool, scale, *, tm):
    m = p.shape[0]
    halo_blocks = tm // HALO
    return pl.pallas_call(
        functools.partial(_ab_mix_prompt_kernel, tm=tm),
        out_shape=jax.ShapeDtypeStruct((m, W_A + W_B), BF16),
        grid=(m // tm,),
        in_specs=[
            pl.BlockSpec((tm, W_B), lambda i: (i, 0)),
            pl.BlockSpec((HALO, W_B), lambda i: (jnp.maximum(i * halo_blocks - 1, 0), 0)),
            pl.BlockSpec((tm, W_A), lambda i: (i, 0)),
            pl.BlockSpec((tm, W_A + W_B), lambda i: (i, 0)),
            pl.BlockSpec((len(POOL_WINDOWS), G_B, G_B), lambda i: (0, 0, 0)),
            pl.BlockSpec((1, W_B), lambda i: (0, 0)),
        ],
        out_specs=pl.BlockSpec((tm, W_A + W_B), lambda i: (i, 0)),
        scratch_shapes=[pltpu.VMEM((tm + HALO, W_B), F32)],
        compiler_params=_params(("parallel",)),
        name="ab_mix_prompt",
    )(p, p, o_a, g, w_pool, scale.reshape(1, W_B))


def _ab_mix_sample_kernel(p_ref, prev_ref, oa_ref, g_ref, wp_ref, scale_ref, a_ref):
    p = p_ref[...]
    sums, invs = [], []
    for gi, wnd in enumerate(POOL_WINDOWS):
        cols = slice(gi * G_B, (gi + 1) * G_B)
        acc = p[:, cols]
        for back in range(1, wnd):
            acc = acc + prev_ref[POOL_BUF - back, :, cols]
        sums.append(acc)
        invs.append(1.0 / wnd)
    _pool_mix_gate(sums, p, invs, oa_ref[...], g_ref[...], wp_ref, scale_ref, a_ref)


def _ab_mix_sample(p, prev_t, o_a, g, w_pool, scale):
    m = p.shape[0]
    return pl.pallas_call(
        _ab_mix_sample_kernel,
        out_shape=jax.ShapeDtypeStruct((m, W_A + W_B), BF16),
        name="ab_mix_sample",
        compiler_params=pltpu.CompilerParams(vmem_limit_bytes=VMEM_LIMIT),
    )(p, prev_t, o_a, g, w_pool, scale.reshape(1, W_B))


def _out_proj_kernel(a_ref, w_ref, x_ref, y_ref):
    y_ref[...] = x_ref[...] + jnp.dot(a_ref[...], w_ref[...], preferred_element_type=F32)


def _out_proj(a, w, x, *, tm, tn):
    m, kdim = a.shape
    n = w.shape[1]
    return pl.pallas_call(
        _out_proj_kernel,
        out_shape=jax.ShapeDtypeStruct((m, n), F32),
        grid=(m // tm, n // tn),
        in_specs=[
            pl.BlockSpec((tm, kdim), lambda i, j: (i, 0)),
            pl.BlockSpec((kdim, tn), lambda i, j: (0, j)),
            pl.BlockSpec((tm, tn), lambda i, j: (i, j)),
        ],
        out_specs=pl.BlockSpec((tm, tn), lambda i, j: (i, j)),
        compiler_params=_params(("parallel", "parallel")),
        name="out_proj",
    )(a, w, x)


def _c_in_kernel(x_ref, gain_ref, w_ref, u_ref, v_ref, g_ref, h_ref, *, nb):
    j = pl.program_id(1)

    @pl.when(j == 0)
    def _():
        h_ref[...] = _rms_rows(x_ref[...], gain_ref[...]).astype(BF16)

    z = jnp.dot(h_ref[...], w_ref[...], preferred_element_type=F32)

    @pl.when(j < nb)
    def _():
        u_ref[...] = _gelu_tanh(z)

    @pl.when((j >= nb) & (j < 2 * nb))
    def _():
        v_ref[...] = _gelu_tanh(z)

    @pl.when(j >= 2 * nb)
    def _():
        g_ref[...] = z


def _c_in_proj(x, gain, w, *, tm, tn):
    m, d = x.shape
    n = w.shape[1]
    wc = n // 3
    nb = wc // tn

    def grp(first):
        return pl.BlockSpec((tm, tn), lambda i, j: (i, _group_col(j, first, nb)))

    return pl.pallas_call(
        functools.partial(_c_in_kernel, nb=nb),
        out_shape=tuple(jax.ShapeDtypeStruct((m, wc), F32) for _ in range(3)),
        grid=(m // tm, n // tn),
        in_specs=[
            pl.BlockSpec((tm, d), lambda i, j: (i, 0)),
            pl.BlockSpec((1, d), lambda i, j: (0, 0)),
            pl.BlockSpec((d, tn), lambda i, j: (0, j)),
        ],
        out_specs=(grp(0), grp(nb), grp(2 * nb)),
        scratch_shapes=[pltpu.VMEM((tm, d), BF16)],
        compiler_params=_params(("parallel", "arbitrary")),
        name="c_in_proj",
    )(x, gain.reshape(1, d), w)


def _c_gate_prompt_kernel(u_ref, vg_ref, g_ref, vgain_ref, ws_ref, bst_ref, a_ref, vrow_ref, *,
                          chunks):
    gc = u_ref.shape[1] // H_C
    r = lax.broadcasted_iota(jnp.int32, (CHUNK_C, CHUNK_C), 0)
    c = lax.broadcasted_iota(jnp.int32, (CHUNK_C, CHUNK_C), 1)
    lower = r >= c
    for ch in range(chunks):
        rows = slice(ch * CHUNK_C, (ch + 1) * CHUNK_C)
        v = _rms_rows(vg_ref[rows, :], vgain_ref[...])
        if ch == chunks - 1:
            vrow_ref[...] = v
        for hd in range(H_C):
            cols = slice(hd * gc, (hd + 1) * gc)
            ws = jnp.where(lower, ws_ref[hd], 0.0).astype(BF16)
            mixed = jnp.dot(ws, v[:, cols].astype(BF16), preferred_element_type=F32)
            mixed = mixed + bst_ref[:, hd:hd + 1]
            a_ref[rows, cols] = (u_ref[rows, cols] * mixed * _silu(g_ref[rows, cols])).astype(BF16)


def _c_gate_prompt(u, vg, g, v_gain, w_s, b_s, *, chunks):
    m, wc = u.shape
    tm = chunks * CHUNK_C
    row = pl.BlockSpec((tm, wc), lambda i: (i, 0))
    return pl.pallas_call(
        functools.partial(_c_gate_prompt_kernel, chunks=chunks),
        out_shape=(jax.ShapeDtypeStruct((m, wc), BF16),
                   jax.ShapeDtypeStruct((CHUNK_C, wc), F32)),
        grid=(m // tm,),
        in_specs=[
            row, row, row,
            pl.BlockSpec((1, wc), lambda i: (0, 0)),
            pl.BlockSpec((H_C, CHUNK_C, CHUNK_C), lambda i: (0, 0, 0)),
            pl.BlockSpec((CHUNK_C, H_C), lambda i: (0, 0)),
        ],
        out_specs=(row, pl.BlockSpec((CHUNK_C, wc), lambda i: (0, 0))),
        compiler_params=_params(("arbitrary",)),
        name="c_gate_prompt",
    )(u, vg, g, v_gain.reshape(1, wc), w_s, b_s.T)


def _c_gate_sample_kernel(u_ref, vg_ref, g_ref, vgain_ref, w00_ref, b0_ref, a_ref, vrow_ref):
    v = _rms_rows(vg_ref[...], vgain_ref[...])
    vrow_ref[...] = v
    mixed = w00_ref[...] * v + b0_ref[...]
    a_ref[...] = (u_ref[...] * mixed * _silu(g_ref[...])).astype(BF16)


def _c_gate_sample(u, vg, g, v_gain, w_s, b_s):
    m, wc = u.shape
    gc = wc // H_C
    w00 = jnp.repeat(w_s[:, 0, 0], gc).reshape(1, wc)
    b0 = jnp.repeat(b_s[:, 0], gc).reshape(1, wc)
    return pl.pallas_call(
        _c_gate_sample_kernel,
        out_shape=(jax.ShapeDtypeStruct((m, wc), BF16), jax.ShapeDtypeStruct((m, wc), F32)),
        name="c_gate_sample",
    )(u, vg, g, v_gain.reshape(1, wc), w00, b0)


def kernel(x_prompt, x_sample, cache_k, cache_v, state_pool, page_table, norm_gain, w_in_ab, q_gain,
           k_gain, sb_bias, w_pool, pool_scale, w_out_ab, w_in_c, v_gain, w_spatial, b_spatial,
           w_out_c):
    bp, s, d = x_prompt.shape
    bs, t, _ = x_sample.shape
    assert bp == 1 and t == 1
    xp = x_prompt.reshape(s, d)
    xs = x_sample.reshape(bs, d)

    w_in0 = w_in_ab[0].astype(BF16)
    w_pool0 = w_pool[0].astype(BF16)
    w_out0 = w_out_ab[0].astype(BF16)
    w_in1 = w_in_c[0].astype(BF16)
    w_out1 = w_out_c[0].astype(BF16)

    q, k, kb, v, vb, p, g = _ab_in_proj(xp, norm_gain[0], w_in0, q_gain[0], k_gain[0], tm=512, tn=512)
    o_a = _sb_prompt(q, kb, vb, sb_bias[0], th=512, tk=256)
    a = _ab_mix_prompt(p, o_a, g, w_pool0, pool_scale[0], tm=256)
    xp1 = _out_proj(a, w_out0, xp, tm=512, tn=1024)
    new_k_p = k.reshape(1, bp, s, N_HEADS, HEAD_DIM)
    new_v_p = v.reshape(1, bp, s, N_HEADS, HEAD_DIM)
    new_pool_p = p[s - POOL_BUF:].reshape(1, bp, POOL_BUF, W_B)

    qs, ks, _, vs, _, ps, gs = _ab_in_proj(xs, norm_gain[0], w_in0, q_gain[0], k_gain[0],
                                           tm=bs, tn=512)
    oa_s = _sb_decode(qs, cache_k[0], cache_v[0], page_table, sb_bias[0], group=8)
    prev_t = jnp.swapaxes(state_pool[0], 0, 1)
    a_s = _ab_mix_sample(ps, prev_t, oa_s, gs, w_pool0, pool_scale[0])
    xs1 = _out_proj(a_s, w_out0, xs, tm=bs, tn=1024)
    new_k_s = ks.reshape(1, bs, t, N_HEADS, HEAD_DIM)
    new_v_s = vs.reshape(1, bs, t, N_HEADS, HEAD_DIM)
    new_pool_s = jnp.concatenate([state_pool[0][:, 1:], ps[:, None, :]], axis=1)[None]

    u, vg, g1 = _c_in_proj(xp1, norm_gain[1], w_in1, tm=512, tn=512)
    a1, v_last = _c_gate_prompt(u, vg, g1, v_gain[0], w_spatial[0], b_spatial[0], chunks=2)
    yp = _out_proj(a1, w_out1, xp1, tm=512, tn=1024)

    us, vgs, g1s = _c_in_proj(xs1, norm_gain[1], w_in1, tm=bs, tn=512)
    a1s, v_s = _c_gate_sample(us, vgs, g1s, v_gain[0], w_spatial[0], b_spatial[0])
    ys = _out_proj(a1s, w_out1, xs1, tm=bs, tn=1024)

    return (yp.reshape(bp, s, d), ys.reshape(bs, t, d), new_k_p, new_v_p, new_k_s, new_v_s,
            new_pool_p, new_pool_s, v_last.reshape(1, bp, CHUNK_C, d), v_s.reshape(1, bs, t, d))
```

```python
import functools
import math

import jax
import jax.numpy as jnp
from jax import lax
from jax.experimental import pallas as pl
from jax.experimental.pallas import tpu as pltpu

F32 = jnp.float32
BF16 = jnp.bfloat16

LANES = 128
HEAD_DIM = 128
N_HEADS = 8
W_A = N_HEADS * HEAD_DIM
POOL_WINDOWS = (2, 4, 8, 16)
G_B = 256
W_B = G_B * len(POOL_WINDOWS)
W_AB = W_A + W_B
POOL_BUF = max(POOL_WINDOWS) - 1
HALO = 16
CHUNK_C = 128
H_C = 8
EPS = 1e-6
SM_SCALE = 1.0 / math.sqrt(HEAD_DIM)
LOG2E = math.log2(math.e)
GELU_C = math.sqrt(2.0 / math.pi)
VMEM_LIMIT = 56 * 1024 * 1024

IN_PROJ_TM, IN_PROJ_TN = 1024, 512
IN_PROJ_ROW_CHUNK = 256
OUT_PROJ_TM, OUT_PROJ_TN = 1024, 1024
ATTN_HALF_ROWS, ATTN_SUFFIX_BLOCK = 512, 256
DECODE_PAGES_PER_STEP = 8
AB_MIX_TM = 256
C_GATE_CHUNKS = 2


def _params(semantics):
    return pltpu.CompilerParams(dimension_semantics=semantics, vmem_limit_bytes=VMEM_LIMIT)


def _softplus_2(z2):
    neg_abs = lax.bitcast_convert_type(
        lax.bitcast_convert_type(z2, jnp.uint32) | jnp.uint32(0x80000000), F32)
    return jnp.maximum(z2, 0.0) + jnp.log(1.0 + jnp.exp2(neg_abs)) * LOG2E


def _silu(g):
    return g * jax.nn.sigmoid(g)


def _gelu_tanh(x):
    return 0.5 * x * (1.0 + jnp.tanh(GELU_C * (x + 0.044715 * (x * x * x))))


def _rms_rows(xf, gain):
    ms = jnp.mean(xf * xf, axis=-1, keepdims=True)
    return xf * lax.rsqrt(ms + EPS) * gain


def _in_proj_kernel(x_ref, gain_ref, w_ref, *refs, norm_blocks, gelu_blocks, bf16_blocks):
    if norm_blocks:
        colgain_ref, refs = refs[0], refs[1:]
    if bf16_blocks:
        zf_ref, zb_ref, h_ref = refs
    else:
        zf_ref, h_ref = refs
    j = pl.program_id(1)

    @pl.when(j == 0)
    def _():
        h_ref[...] = _rms_rows(x_ref[...], gain_ref[...]).astype(BF16)

    tm = h_ref.shape[0]
    step = min(tm, IN_PROJ_ROW_CHUNK)
    for r0 in range(0, tm, step):
        rows = slice(r0, r0 + step)
        z = jnp.dot(h_ref[rows, :], w_ref[...], preferred_element_type=F32)
        if norm_blocks:
            is_norm = j < norm_blocks
            cols = []
            for c in range(z.shape[1] // HEAD_DIM):
                zc = z[:, c * HEAD_DIM:(c + 1) * HEAD_DIM]
                rs = lax.rsqrt(jnp.mean(zc * zc, axis=-1, keepdims=True) + EPS)
                cols.append(zc * jnp.where(is_norm, rs, 1.0))
            z = jnp.concatenate(cols, axis=1) * colgain_ref[...]
        if gelu_blocks:
            z = jnp.where(j < gelu_blocks, _gelu_tanh(z), z)
        zf_ref[rows, :] = z
        if bf16_blocks:
            zb_ref[rows, :] = z.astype(BF16)


def _in_proj(x, gain, w, *, tm, tn, colgain=None, norm_cols=0, gelu_cols=0, bf16_cols=0, name):
    m, d = x.shape
    n = w.shape[1]
    bf16_blocks = bf16_cols // tn
    in_specs = [
        pl.BlockSpec((tm, d), lambda i, j: (i, 0)),
        pl.BlockSpec((1, d), lambda i, j: (0, 0)),
        pl.BlockSpec((d, tn), lambda i, j: (0, j)),
    ]
    args = [x, gain.reshape(1, d), w]
    if norm_cols:
        in_specs.append(pl.BlockSpec((1, tn), lambda i, j: (0, j)))
        args.append(colgain.reshape(1, n))
    out_shape = [jax.ShapeDtypeStruct((m, n), F32)]
    out_specs = [pl.BlockSpec((tm, tn), lambda i, j: (i, j))]
    if bf16_blocks:
        out_shape.append(jax.ShapeDtypeStruct((m, bf16_cols + tn), BF16))
        out_specs.append(pl.BlockSpec((tm, tn), lambda i, j: (i, jnp.minimum(j, bf16_blocks))))
    return pl.pallas_call(
        functools.partial(_in_proj_kernel, norm_blocks=norm_cols // tn, gelu_blocks=gelu_cols // tn,
                          bf16_blocks=bf16_blocks),
        out_shape=tuple(out_shape),
        grid=(m // tm, n // tn),
        in_specs=in_specs,
        out_specs=tuple(out_specs),
        scratch_shapes=[pltpu.VMEM((tm, d), BF16)],
        compiler_params=_params(("parallel", "arbitrary")),
        name=name,
    )(*args)


def _strict_upper(n):
    r = lax.broadcasted_iota(jnp.int32, (n, n), 0)
    c = lax.broadcasted_iota(jnp.int32, (n, n), 1)
    return jnp.where(r > c, 1.0, 0.0).astype(BF16)


def _sb_prompt_kernel(bias_ref, q_ref, k_ref, v_ref, o_ref, acc_ref, r_ref, *, th, tk):
    h = pl.program_id(0)
    i = pl.program_id(1)
    bias2 = bias_ref[h] * LOG2E
    upper = _strict_upper(tk)
    acc_ref[...] = jnp.zeros_like(acc_ref)
    r_ref[...] = jnp.zeros_like(r_ref)
    nsub = th // tk

    def tile(half, kt, masked):
        rows = slice(half * th, (half + 1) * th)
        ks = pl.multiple_of(kt * th, th)
        z2 = lax.dot_general(q_ref[rows, :], k_ref[pl.ds(ks, th), :], (((1,), (1,)), ((), ())),
                             preferred_element_type=F32) + bias2
        sp2 = _softplus_2(z2)
        ls2 = z2 - sp2
        if masked:
            t_pos = lax.broadcasted_iota(jnp.int32, (th, th), 0)
            s_pos = lax.broadcasted_iota(jnp.int32, (th, th), 1)
            valid = s_pos < t_pos
            sp2 = jnp.where(valid, sp2, 0.0)
        stacked = jnp.concatenate([sp2[:, c * tk:(c + 1) * tk] for c in range(nsub)], axis=0)
        later = jnp.dot(stacked.astype(BF16), upper, preferred_element_type=F32)
        run = r_ref[rows, :]
        ws = [None] * nsub
        for c in reversed(range(nsub)):
            offs = jnp.concatenate([run] * (tk // LANES), axis=1)
            ws[c] = jnp.exp2(ls2[:, c * tk:(c + 1) * tk] - later[c * th:(c + 1) * th] - offs)
            total = jnp.sum(sp2[:, c * tk:(c + 1) * tk], axis=-1, keepdims=True)
            run = run + jnp.broadcast_to(total, (th, LANES))
        r_ref[rows, :] = run
        w = jnp.concatenate(ws, axis=1)
        if masked:
            w = jnp.where(valid, w, 0.0)
        acc_ref[rows, :] += jnp.dot(w.astype(BF16), v_ref[pl.ds(ks, th), :],
                                    preferred_element_type=F32)

    tile(1, 2 * i + 1, True)
    tile(0, 2 * i, True)
    tile(1, 2 * i, False)

    def body(it, carry):
        kt = 2 * i - 1 - 2 * it
        tile(0, kt, False)
        tile(1, kt, False)
        tile(0, kt - 1, False)
        tile(1, kt - 1, False)
        return carry

    lax.fori_loop(0, i, body, 0)
    o_ref[...] = acc_ref[...]


def _sb_prompt(qkv, bias, *, th, tk):
    s = qkv.shape[0]
    tq = 2 * th
    return pl.pallas_call(
        functools.partial(_sb_prompt_kernel, th=th, tk=tk),
        out_shape=jax.ShapeDtypeStruct((s, W_A), F32),
        grid=(N_HEADS, s // tq),
        in_specs=[
            pl.BlockSpec(memory_space=pltpu.SMEM),
            pl.BlockSpec((tq, HEAD_DIM), lambda h, i: (i, h)),
            pl.BlockSpec((s, HEAD_DIM), lambda h, i: (0, N_HEADS + h)),
            pl.BlockSpec((s, HEAD_DIM), lambda h, i: (0, 2 * N_HEADS + h)),
        ],
        out_specs=pl.BlockSpec((tq, HEAD_DIM), lambda h, i: (i, h)),
        scratch_shapes=[pltpu.VMEM((tq, HEAD_DIM), F32), pltpu.VMEM((tq, LANES), F32)],
        compiler_params=_params(("parallel", "arbitrary")),
        name="sb_prompt",
    )(bias, qkv, qkv, qkv)


def _suffix_and_total(n):
    r = lax.broadcasted_iota(jnp.int32, (n, 2 * n), 0)
    c = lax.broadcasted_iota(jnp.int32, (n, 2 * n), 1)
    return jnp.where((r > c) | (c >= n), 1.0, 0.0).astype(BF16)


def _sb_decode_kernel(pt_ref, q_ref, bias_ref, *refs, page, group):
    del pt_ref
    k_refs, v_refs = refs[:group], refs[group:2 * group]
    o_ref, acc_ref, r_ref = refs[2 * group:]
    pi = pl.program_id(1)
    flat = page * N_HEADS
    tiles = flat // LANES

    @pl.when(pi == 0)
    def _():
        acc_ref[...] = jnp.zeros_like(acc_ref)
        r_ref[...] = jnp.zeros_like(r_ref)

    lane = lax.broadcasted_iota(jnp.int32, (N_HEADS, LANES), 1)
    row = lax.broadcasted_iota(jnp.int32, (N_HEADS, LANES), 0)
    own_t = (lane % N_HEADS) == row
    suffix_total = _suffix_and_total(LANES)
    qb = q_ref[0].astype(BF16)
    bias2 = bias_ref[...] * LOG2E

    km = jnp.concatenate([k_refs[g][0].reshape(flat, HEAD_DIM).astype(BF16) for g in range(group)],
                         axis=0)
    z2 = lax.dot_general(qb, km, (((1,), (1,)), ((), ())),
                         preferred_element_type=F32)
    z2 = z2 + bias2
    sp2 = _softplus_2(z2)
    ls2 = z2 - sp2
    n_tiles = group * tiles
    order = [g * tiles + j for g in range(group) for j in reversed(range(tiles))]
    sp_t = [jnp.where(own_t, sp2[:, t * LANES:(t + 1) * LANES], 0.0) for t in order]
    stacked = jnp.concatenate(sp_t, axis=0)
    hi = stacked.astype(BF16)
    lo = (stacked - hi.astype(F32)).astype(BF16)
    s = jnp.dot(jnp.concatenate([hi, lo], axis=0), suffix_total, preferred_element_type=F32)
    s = s[:n_tiles * N_HEADS] + s[n_tiles * N_HEADS:]

    run = jnp.broadcast_to(r_ref[...], (N_HEADS, LANES))
    ws = [None] * n_tiles
    for n, t in enumerate(order):
        rows = slice(n * N_HEADS, (n + 1) * N_HEADS)
        later = s[rows, :LANES] + run
        ws[t] = jnp.where(own_t, jnp.exp2(ls2[:, t * LANES:(t + 1) * LANES] - later), 0.0)
        run = run + s[rows, LANES:]
    r_ref[...] = run[:, :1]
    w = jnp.concatenate(ws, axis=1).astype(BF16)
    vm = jnp.concatenate([v_refs[g][0].reshape(flat, HEAD_DIM).astype(BF16) for g in range(group)],
                         axis=0)
    acc = acc_ref[...] + jnp.dot(w, vm, preferred_element_type=F32)
    acc_ref[...] = acc

    @pl.when(pi == pl.num_programs(1) - 1)
    def _():
        o_ref[0] = acc


def _sb_decode(q, cache_k, cache_v, page_table, bias, *, group):
    b = q.shape[0]
    page = cache_k.shape[1]
    n_pages = page_table.shape[1]

    def page_spec(g):
        return pl.BlockSpec(
            (1, page, N_HEADS, HEAD_DIM),
            lambda s, pi, pt: (pt[s, n_pages - 1 - (pi * group + g)], 0, 0, 0))

    pages = [page_spec(g) for g in range(group)]
    grid_spec = pltpu.PrefetchScalarGridSpec(
        num_scalar_prefetch=1,
        grid=(b, n_pages // group),
        in_specs=[
            pl.BlockSpec((1, N_HEADS, HEAD_DIM), lambda s, pi, pt: (s, 0, 0)),
            pl.BlockSpec((N_HEADS, 1), lambda s, pi, pt: (0, 0)),
        ] + pages + pages,
        out_specs=pl.BlockSpec((1, N_HEADS, HEAD_DIM), lambda s, pi, pt: (s, 0, 0)),
        scratch_shapes=[pltpu.VMEM((N_HEADS, HEAD_DIM), F32), pltpu.VMEM((N_HEADS, 1), F32)],
    )
    out = pl.pallas_call(
        functools.partial(_sb_decode_kernel, page=page, group=group),
        out_shape=jax.ShapeDtypeStruct((b, N_HEADS, HEAD_DIM), F32),
        grid_spec=grid_spec,
        compiler_params=_params(("parallel", "arbitrary")),
        name="sb_decode",
    )(page_table, q.reshape(b, N_HEADS, HEAD_DIM), bias.reshape(N_HEADS, 1),
      *([cache_k] * group), *([cache_v] * group))
    return out.reshape(b, W_A)


def _pool_mix_gate(window_sums, p, count_inv, o_a, g, wp_ref, scale_ref, a_ref):
    a_ref[:, :W_A] = (o_a * _silu(g[:, :W_A])).astype(BF16)
    for gi in range(len(POOL_WINDOWS)):
        cols = slice(gi * G_B, (gi + 1) * G_B)
        pooled = window_sums[gi] * count_inv[gi] - p[:, cols]
        mixed = jnp.dot(pooled.astype(BF16), wp_ref[gi], preferred_element_type=F32)
        o_b = mixed * scale_ref[:, cols]
        a_ref[:, W_A + gi * G_B:W_A + (gi + 1) * G_B] = (
            o_b * _silu(g[:, W_A + gi * G_B:W_A + (gi + 1) * G_B])).astype(BF16)


def _ab_mix_prompt_kernel(p_ref, halo_ref, oa_ref, g_ref, wp_ref, scale_ref, a_ref, buf_ref, *, tm):
    i = pl.program_id(0)
    buf_ref[:HALO, :] = jnp.where(i > 0, halo_ref[...], 0.0)
    buf_ref[HALO:, :] = p_ref[...]
    pos = i * tm + lax.broadcasted_iota(jnp.int32, (tm, 1), 0)
    sums, invs = [], []
    for gi, wnd in enumerate(POOL_WINDOWS):
        cols = slice(gi * G_B, (gi + 1) * G_B)
        acc = buf_ref[HALO:, cols]
        for back in range(1, wnd):
            acc = acc + buf_ref[HALO - back:HALO - back + tm, cols]
        sums.append(acc)
        invs.append(1.0 / jnp.minimum(wnd, pos + 1).astype(F32))
    _pool_mix_gate(sums, p_ref[...], invs, oa_ref[...], g_ref[...], wp_ref, scale_ref, a_ref)


def _ab_mix_prompt(z, o_a, w_pool, scale, *, tm):
    m = z.shape[0]
    halo_blocks = tm // HALO
    p_col = 3 * W_A // W_B
    g_col = (3 * W_A + W_B) // W_AB
    return pl.pallas_call(
        functools.partial(_ab_mix_prompt_kernel, tm=tm),
        out_shape=jax.ShapeDtypeStruct((m, W_AB), BF16),
        grid=(m // tm,),
        in_specs=[
            pl.BlockSpec((tm, W_B), lambda i: (i, p_col)),
            pl.BlockSpec((HALO, W_B), lambda i: (jnp.maximum(i * halo_blocks - 1, 0), p_col)),
            pl.BlockSpec((tm, W_A), lambda i: (i, 0)),
            pl.BlockSpec((tm, W_AB), lambda i: (i, g_col)),
            pl.BlockSpec((len(POOL_WINDOWS), G_B, G_B), lambda i: (0, 0, 0)),
            pl.BlockSpec((1, W_B), lambda i: (0, 0)),
        ],
        out_specs=pl.BlockSpec((tm, W_AB), lambda i: (i, 0)),
        scratch_shapes=[pltpu.VMEM((tm + HALO, W_B), F32)],
        compiler_params=_params(("parallel",)),
        name="ab_mix_prompt",
    )(z, z, o_a, z, w_pool, scale.reshape(1, W_B))


def _ab_mix_sample_kernel(p_ref, prev_ref, oa_ref, g_ref, wp_ref, scale_ref, a_ref):
    p = p_ref[...]
    sums, invs = [], []
    for gi, wnd in enumerate(POOL_WINDOWS):
        cols = slice(gi * G_B, (gi + 1) * G_B)
        acc = p[:, cols]
        for back in range(1, wnd):
            acc = acc + prev_ref[POOL_BUF - back, :, cols]
        sums.append(acc)
        invs.append(1.0 / wnd)
    _pool_mix_gate(sums, p, invs, oa_ref[...], g_ref[...], wp_ref, scale_ref, a_ref)


def _ab_mix_sample(p, prev_t, o_a, g, w_pool, scale):
    m = p.shape[0]
    return pl.pallas_call(
        _ab_mix_sample_kernel,
        out_shape=jax.ShapeDtypeStruct((m, W_AB), BF16),
        name="ab_mix_sample",
        compiler_params=pltpu.CompilerParams(vmem_limit_bytes=VMEM_LIMIT),
    )(p, prev_t, o_a, g, w_pool, scale.reshape(1, W_B))


def _out_proj_kernel(a_ref, w_ref, x_ref, y_ref):
    y_ref[...] = x_ref[...] + jnp.dot(a_ref[...], w_ref[...], preferred_element_type=F32)


def _out_proj(a, w, x, *, tm, tn):
    m, kdim = a.shape
    n = w.shape[1]
    return pl.pallas_call(
        _out_proj_kernel,
        out_shape=jax.ShapeDtypeStruct((m, n), F32),
        grid=(m // tm, n // tn),
        in_specs=[
            pl.BlockSpec((tm, kdim), lambda i, j: (i, 0)),
            pl.BlockSpec((kdim, tn), lambda i, j: (0, j)),
            pl.BlockSpec((tm, tn), lambda i, j: (i, j)),
        ],
        out_specs=pl.BlockSpec((tm, tn), lambda i, j: (i, j)),
        compiler_params=_params(("parallel", "parallel")),
        name="out_proj",
    )(a, w, x)


def _c_gate_prompt_kernel(u_ref, vg_ref, g_ref, vgain_ref, ws_ref, bst_ref, a_ref, vrow_ref, *,
                          chunks):
    gc = u_ref.shape[1] // H_C
    r = lax.broadcasted_iota(jnp.int32, (CHUNK_C, CHUNK_C), 0)
    c = lax.broadcasted_iota(jnp.int32, (CHUNK_C, CHUNK_C), 1)
    lower = r >= c
    for ch in range(chunks):
        rows = slice(ch * CHUNK_C, (ch + 1) * CHUNK_C)
        v = _rms_rows(vg_ref[rows, :], vgain_ref[...])
        if ch == chunks - 1:
            vrow_ref[...] = v
        for hd in range(H_C):
            cols = slice(hd * gc, (hd + 1) * gc)
            ws = jnp.where(lower, ws_ref[hd], 0.0).astype(BF16)
            mixed = jnp.dot(ws, v[:, cols].astype(BF16), preferred_element_type=F32)
            mixed = mixed + bst_ref[:, hd:hd + 1]
            a_ref[rows, cols] = (u_ref[rows, cols] * mixed * _silu(g_ref[rows, cols])).astype(BF16)


def _c_gate_prompt(z, v_gain, w_s, b_s, *, chunks):
    m = z.shape[0]
    wc = z.shape[1] // 3
    tm = chunks * CHUNK_C
    return pl.pallas_call(
        functools.partial(_c_gate_prompt_kernel, chunks=chunks),
        out_shape=(jax.ShapeDtypeStruct((m, wc), BF16),
                   jax.ShapeDtypeStruct((CHUNK_C, wc), F32)),
        grid=(m // tm,),
        in_specs=[
            pl.BlockSpec((tm, wc), lambda i: (i, 0)),
            pl.BlockSpec((tm, wc), lambda i: (i, 1)),
            pl.BlockSpec((tm, wc), lambda i: (i, 2)),
            pl.BlockSpec((1, wc), lambda i: (0, 0)),
            pl.BlockSpec((H_C, CHUNK_C, CHUNK_C), lambda i: (0, 0, 0)),
            pl.BlockSpec((CHUNK_C, H_C), lambda i: (0, 0)),
        ],
        out_specs=(pl.BlockSpec((tm, wc), lambda i: (i, 0)),
                   pl.BlockSpec((CHUNK_C, wc), lambda i: (0, 0))),
        compiler_params=_params(("arbitrary",)),
        name="c_gate_prompt",
    )(z, z, z, v_gain.reshape(1, wc), w_s, b_s.T)


def _c_gate_sample_kernel(u_ref, vg_ref, g_ref, vgain_ref, w00_ref, b0_ref, a_ref, vrow_ref):
    v = _rms_rows(vg_ref[...], vgain_ref[...])
    vrow_ref[...] = v
    mixed = w00_ref[...] * v + b0_ref[...]
    a_ref[...] = (u_ref[...] * mixed * _silu(g_ref[...])).astype(BF16)


def _c_gate_sample(u, vg, g, v_gain, w_s, b_s):
    m, wc = u.shape
    gc = wc // H_C
    w00 = jnp.repeat(w_s[:, 0, 0], gc).reshape(1, wc)
    b0 = jnp.repeat(b_s[:, 0], gc).reshape(1, wc)
    return pl.pallas_call(
        _c_gate_sample_kernel,
        out_shape=(jax.ShapeDtypeStruct((m, wc), BF16), jax.ShapeDtypeStruct((m, wc), F32)),
        name="c_gate_sample",
    )(u, vg, g, v_gain.reshape(1, wc), w00, b0)


def kernel(x_prompt, x_sample, cache_k, cache_v, state_pool, page_table, norm_gain, w_in_ab, q_gain,
           k_gain, sb_bias, w_pool, pool_scale, w_out_ab, w_in_c, v_gain, w_spatial, b_spatial,
           w_out_c):
    bp, s, d = x_prompt.shape
    bs, t, _ = x_sample.shape
    assert bp == 1 and t == 1
    xp = x_prompt.reshape(s, d)
    xs = x_sample.reshape(bs, d)

    w_in0 = w_in_ab[0].astype(BF16)
    w_pool0 = w_pool[0].astype(BF16)
    w_out0 = w_out_ab[0].astype(BF16)
    w_in1 = w_in_c[0].astype(BF16)
    w_out1 = w_out_c[0].astype(BF16)
    colgain = jnp.concatenate([jnp.tile(q_gain[0] * (SM_SCALE * LOG2E), N_HEADS),
                               jnp.tile(k_gain[0], N_HEADS),
                               jnp.ones((W_A + W_B + W_AB,), F32)])
    ab_in = functools.partial(_in_proj, gain=norm_gain[0], w=w_in0, tn=IN_PROJ_TN, colgain=colgain,
                              norm_cols=2 * W_A, bf16_cols=3 * W_A, name="ab_in_proj")
    c_in = functools.partial(_in_proj, gain=norm_gain[1], w=w_in1, tn=IN_PROJ_TN,
                             gelu_cols=2 * d, name="c_in_proj")

    z0, qkv = ab_in(xp, tm=IN_PROJ_TM)
    o_a = _sb_prompt(qkv, sb_bias[0], th=ATTN_HALF_ROWS, tk=ATTN_SUFFIX_BLOCK)
    a = _ab_mix_prompt(z0, o_a, w_pool0, pool_scale[0], tm=AB_MIX_TM)
    xp1 = _out_proj(a, w_out0, xp, tm=OUT_PROJ_TM, tn=OUT_PROJ_TN)
    new_k_p = z0[:, W_A:2 * W_A].reshape(1, bp, s, N_HEADS, HEAD_DIM)
    new_v_p = z0[:, 2 * W_A:3 * W_A].reshape(1, bp, s, N_HEADS, HEAD_DIM)
    new_pool_p = z0[s - POOL_BUF:, 3 * W_A:3 * W_A + W_B].reshape(1, bp, POOL_BUF, W_B)

    z0s, qkv_s = ab_in(xs, tm=bs)
    ps = z0s[:, 3 * W_A:3 * W_A + W_B]
    oa_s = _sb_decode(qkv_s[:, :W_A], cache_k[0], cache_v[0], page_table, sb_bias[0],
                      group=DECODE_PAGES_PER_STEP)
    prev_t = jnp.swapaxes(state_pool[0], 0, 1)
    a_s = _ab_mix_sample(ps, prev_t, oa_s, z0s[:, 3 * W_A + W_B:], w_pool0, pool_scale[0])
    xs1 = _out_proj(a_s, w_out0, xs, tm=bs, tn=OUT_PROJ_TN)
    new_k_s = z0s[:, W_A:2 * W_A].reshape(1, bs, t, N_HEADS, HEAD_DIM)
    new_v_s = z0s[:, 2 * W_A:3 * W_A].reshape(1, bs, t, N_HEADS, HEAD_DIM)
    new_pool_s = jnp.concatenate([state_pool[0][:, 1:], ps[:, None, :]], axis=1)[None]

    (z1,) = c_in(xp1, tm=IN_PROJ_TM)
    a1, v_last = _c_gate_prompt(z1, v_gain[0], w_spatial[0], b_spatial[0], chunks=C_GATE_CHUNKS)
    yp = _out_proj(a1, w_out1, xp1, tm=OUT_PROJ_TM, tn=OUT_PROJ_TN)

    (z1s,) = c_in(xs1, tm=bs)
    a1s, v_s = _c_gate_sample(z1s[:, :d], z1s[:, d:2 * d], z1s[:, 2 * d:], v_gain[0], w_spatial[0],
                              b_spatial[0])
    ys = _out_proj(a1s, w_out1, xs1, tm=bs, tn=OUT_PROJ_TN)

    return (yp.reshape(bp, s, d), ys.reshape(bs, t, d), new_k_p, new_v_p, new_k_s, new_v_s,
            new_pool_p, new_pool_s, v_last.reshape(1, bp, CHUNK_C, d), v_s.reshape(1, bs, t, d))
```

```python
import functools
import math

import jax
import jax.numpy as jnp
from jax import lax
from jax.experimental import pallas as pl
from jax.experimental.pallas import tpu as pltpu

F32 = jnp.float32
BF16 = jnp.bfloat16

LANES = 128
HEAD_DIM = 128
N_HEADS = 8
W_A = N_HEADS * HEAD_DIM
POOL_WINDOWS = (2, 4, 8, 16)
G_B = 256
W_B = G_B * len(POOL_WINDOWS)
W_AB = W_A + W_B
POOL_BUF = max(POOL_WINDOWS) - 1
HALO = 16
CHUNK_C = 128
H_C = 8
EPS = 1e-6
SM_SCALE = 1.0 / math.sqrt(HEAD_DIM)
LOG2E = math.log2(math.e)
GELU_C = math.sqrt(2.0 / math.pi)
VMEM_LIMIT = 56 * 1024 * 1024

IN_PROJ_TM, IN_PROJ_TN = 1024, 512
IN_PROJ_ROW_CHUNK = 256
OUT_PROJ_TM, OUT_PROJ_TN = 1024, 1024
ATTN_HALF_ROWS, ATTN_SUFFIX_BLOCK = 512, 256
DECODE_PAGES_PER_STEP = 8
AB_MIX_TM = 256
C_GATE_CHUNKS = 2


def _params(semantics):
    return pltpu.CompilerParams(dimension_semantics=semantics, vmem_limit_bytes=VMEM_LIMIT)


def _softplus_2(z2):
    neg_abs = lax.bitcast_convert_type(
        lax.bitcast_convert_type(z2, jnp.uint32) | jnp.uint32(0x80000000), F32)
    return jnp.maximum(z2, 0.0) + jnp.log(1.0 + jnp.exp2(neg_abs)) * LOG2E


def _silu(g):
    return g * jax.nn.sigmoid(g)


def _gelu_tanh(x):
    return 0.5 * x * (1.0 + jnp.tanh(GELU_C * (x + 0.044715 * (x * x * x))))


def _rms_rows(xf, gain):
    ms = jnp.mean(xf * xf, axis=-1, keepdims=True)
    return xf * lax.rsqrt(ms + EPS) * gain


def _in_proj_kernel(x_ref, gain_ref, w_ref, *refs, norm_blocks, gelu_blocks, bf16_blocks):
    if norm_blocks:
        colgain_ref, refs = refs[0], refs[1:]
    if bf16_blocks:
        zf_ref, zb_ref, h_ref = refs
    else:
        zf_ref, h_ref = refs
    j = pl.program_id(1)

    @pl.when(j == 0)
    def _():
        h_ref[...] = _rms_rows(x_ref[...], gain_ref[...]).astype(BF16)

    tm = h_ref.shape[0]
    step = min(tm, IN_PROJ_ROW_CHUNK)
    for r0 in range(0, tm, step):
        rows = slice(r0, r0 + step)
        z = jnp.dot(h_ref[rows, :], w_ref[...], preferred_element_type=F32)
        if norm_blocks:
            is_norm = j < norm_blocks
            cols = []
            for c in range(z.shape[1] // HEAD_DIM):
                zc = z[:, c * HEAD_DIM:(c + 1) * HEAD_DIM]
                rs = lax.rsqrt(jnp.mean(zc * zc, axis=-1, keepdims=True) + EPS)
                cols.append(zc * jnp.where(is_norm, rs, 1.0))
            z = jnp.concatenate(cols, axis=1) * colgain_ref[...]
        if gelu_blocks:
            z = jnp.where(j < gelu_blocks, _gelu_tanh(z), z)
        zf_ref[rows, :] = z
        if bf16_blocks:
            zb_ref[rows, :] = z.astype(BF16)


def _in_proj(x, gain, w, *, tm, tn, colgain=None, norm_cols=0, gelu_cols=0, bf16_cols=0, name):
    m, d = x.shape
    n = w.shape[1]
    bf16_blocks = bf16_cols // tn
    in_specs = [
        pl.BlockSpec((tm, d), lambda i, j: (i, 0)),
        pl.BlockSpec((1, d), lambda i, j: (0, 0)),
        pl.BlockSpec((d, tn), lambda i, j: (0, j)),
    ]
    args = [x, gain.reshape(1, d), w]
    if norm_cols:
        in_specs.append(pl.BlockSpec((1, tn), lambda i, j: (0, j)))
        args.append(colgain.reshape(1, n))
    out_shape = [jax.ShapeDtypeStruct((m, n), F32)]
    out_specs = [pl.BlockSpec((tm, tn), lambda i, j: (i, j))]
    if bf16_blocks:
        out_shape.append(jax.ShapeDtypeStruct((m, bf16_cols + tn), BF16))
        out_specs.append(pl.BlockSpec((tm, tn), lambda i, j: (i, jnp.minimum(j, bf16_blocks))))
    return pl.pallas_call(
        functools.partial(_in_proj_kernel, norm_blocks=norm_cols // tn, gelu_blocks=gelu_cols // tn,
                          bf16_blocks=bf16_blocks),
        out_shape=tuple(out_shape),
        grid=(m // tm, n // tn),
        in_specs=in_specs,
        out_specs=tuple(out_specs),
        scratch_shapes=[pltpu.VMEM((tm, d), BF16)],
        compiler_params=_params(("parallel", "arbitrary")),
        name=name,
    )(*args)


def _strict_upper(n):
    r = lax.broadcasted_iota(jnp.int32, (n, n), 0)
    c = lax.broadcasted_iota(jnp.int32, (n, n), 1)
    return jnp.where(r > c, 1.0, 0.0).astype(BF16)


def _sb_attention_kernel(pt_ref, bias_ref, q_ref, k_ref, v_ref, qs_ref, bias_col_ref, ck_ref, cv_ref,
                         o_ref, os_ref, acc_ref, r_ref, kbuf_ref, vbuf_ref, sem_ref, accs_ref,
                         rs_ref, *, th, tk, group):
    h = pl.program_id(0)
    i = pl.program_id(1)
    n_seq, n_pages = pt_ref.shape
    groups_per_seq = n_pages // group
    n_groups = n_seq * groups_per_seq
    groups_per_head = n_groups // pl.num_programs(0)
    first_step = (h == 0) & (i == 0)
    last_step = (h == pl.num_programs(0) - 1) & (i == pl.num_programs(1) - 1)

    def page_copies(n, slot):
        seq = n // groups_per_seq
        pg = n % groups_per_seq
        copies = []
        for g in range(group):
            page = pt_ref[seq, n_pages - 1 - (pg * group + g)]
            copies.append(pltpu.make_async_copy(ck_ref.at[page], kbuf_ref.at[slot, g],
                                                sem_ref.at[slot, 0]))
            copies.append(pltpu.make_async_copy(cv_ref.at[page], vbuf_ref.at[slot, g],
                                                sem_ref.at[slot, 1]))
        return copies

    def decode_fetch(n):
        slot = n % 2
        for c in page_copies(n, slot):
            c.wait()
        for c in page_copies(jnp.minimum(n + 1, n_groups - 1), 1 - slot):
            c.start()

    def decode_qk(n):
        slot = n % 2
        return _decode_qk(qs_ref[n // groups_per_seq], bias_col_ref[...] * LOG2E,
                          [kbuf_ref[slot, g] for g in range(group)])

    def decode_pv(n, ls2, sums):
        slot = n % 2
        seq = n // groups_per_seq
        fresh = (n % groups_per_seq) == 0
        acc0 = jnp.where(fresh, 0.0, accs_ref[...])
        run0 = jnp.where(fresh, 0.0, rs_ref[...])
        acc, run = _decode_pv(ls2, sums, [vbuf_ref[slot, g] for g in range(group)], acc0, run0)
        accs_ref[...] = acc
        rs_ref[...] = run
        os_ref[seq] = acc

    @pl.when(first_step)
    def _():
        accs_ref[...] = jnp.zeros_like(accs_ref)
        rs_ref[...] = jnp.zeros_like(rs_ref)
        for c in page_copies(0, 0):
            c.start()

    bias2 = bias_ref[h] * LOG2E
    upper = _strict_upper(tk)
    nsub = th // tk
    half_rows = [slice(half * th, (half + 1) * th) for half in range(2)]
    even = 1 - i % 2
    group_base = h * groups_per_head + (i * (i - 1)) // 2 + (i + 1) // 2

    def qk_stage(half, kt, masked):
        ks = pl.multiple_of(kt * th, th)
        z2 = lax.dot_general(q_ref[half_rows[half], :], k_ref[pl.ds(ks, th), :],
                             (((1,), (1,)), ((), ())), preferred_element_type=F32) + bias2
        sp2 = _softplus_2(z2)
        ls2 = z2 - sp2
        valid = None
        if masked:
            t_pos = lax.broadcasted_iota(jnp.int32, (th, th), 0)
            s_pos = lax.broadcasted_iota(jnp.int32, (th, th), 1)
            valid = s_pos < t_pos
            sp2 = jnp.where(valid, sp2, 0.0)
        blocks = [sp2[:, c * tk:(c + 1) * tk] for c in range(nsub)]
        totals = [jnp.broadcast_to(jnp.sum(b, axis=-1, keepdims=True), (th, LANES)) for b in blocks]
        return ks, ls2, jnp.concatenate(blocks, axis=0).astype(BF16), totals, valid

    def suffix_stage(st):
        return jnp.dot(st[2], upper, preferred_element_type=F32)

    def pv_stage(st, later, run):
        ks, ls2, _, totals, valid = st
        ws = [None] * nsub
        for c in reversed(range(nsub)):
            offs = jnp.concatenate([run] * (tk // LANES), axis=1)
            ws[c] = jnp.exp2(ls2[:, c * tk:(c + 1) * tk] - later[c * th:(c + 1) * th] - offs)
            run = run + totals[c]
        w = jnp.concatenate(ws, axis=1)
        if valid is not None:
            w = jnp.where(valid, w, 0.0)
        return jnp.dot(w.astype(BF16), v_ref[pl.ds(ks, th), :], preferred_element_type=F32), run

    sts = [qk_stage(1, 2 * i + 1, True), qk_stage(0, 2 * i, True), qk_stage(1, 2 * i, False)]
    laters = [suffix_stage(st) for st in sts]
    zero = jnp.zeros((th, LANES), F32)
    pv_hi, run1 = pv_stage(sts[0], laters[0], zero)
    pv0, run0 = pv_stage(sts[1], laters[1], zero)
    pv_lo, run1 = pv_stage(sts[2], laters[2], run1)
    acc_ref[half_rows[0], :] = pv0
    acc_ref[half_rows[1], :] = pv_hi + pv_lo
    r_ref[half_rows[0], :] = run0
    r_ref[half_rows[1], :] = run1

    @pl.when(even == 1)
    def _():
        decode_fetch(group_base)
        ls2, stacked = decode_qk(group_base)
        decode_pv(group_base, ls2, _decode_suffix(stacked))

    def body(it, carry):
        kt = 2 * i - 1 - 2 * it
        n = group_base + even + it
        decode_fetch(n)
        sts = [qk_stage(0, kt, False), qk_stage(1, kt, False),
               qk_stage(0, kt - 1, False), qk_stage(1, kt - 1, False)]
        d_ls2, d_stacked = decode_qk(n)
        laters = [suffix_stage(st) for st in sts]
        d_sums = _decode_suffix(d_stacked)
        pvs = [None] * 4
        runs = [r_ref[rows, :] for rows in half_rows]
        for t in range(4):
            pvs[t], runs[t % 2] = pv_stage(sts[t], laters[t], runs[t % 2])
        decode_pv(n, d_ls2, d_sums)
        for half, rows in enumerate(half_rows):
            acc_ref[rows, :] += pvs[half] + pvs[half + 2]
            r_ref[rows, :] = runs[half]
        return carry

    lax.fori_loop(0, i, body, 0)
    o_ref[...] = acc_ref[...]

    @pl.when(last_step)
    def _():
        for c in page_copies(n_groups - 1, n_groups % 2):
            c.wait()


def _sb_attention(qkv, q_s, cache_k, cache_v, page_table, bias, *, th, tk, group):
    s = qkv.shape[0]
    b = q_s.shape[0]
    page = cache_k.shape[1]
    n_pages = page_table.shape[1]
    tq = 2 * th
    steps = s // tq
    assert b * (n_pages // group) == N_HEADS * (steps * (steps - 1) // 2 + (steps + 1) // 2)
    grid_spec = pltpu.PrefetchScalarGridSpec(
        num_scalar_prefetch=1,
        grid=(N_HEADS, steps),
        in_specs=[
            pl.BlockSpec(memory_space=pltpu.SMEM),
            pl.BlockSpec((tq, HEAD_DIM), lambda h, i, pt: (i, h)),
            pl.BlockSpec((s, HEAD_DIM), lambda h, i, pt: (0, N_HEADS + h)),
            pl.BlockSpec((s, HEAD_DIM), lambda h, i, pt: (0, 2 * N_HEADS + h)),
            pl.BlockSpec((b, N_HEADS, HEAD_DIM), lambda h, i, pt: (0, 0, 0)),
            pl.BlockSpec((N_HEADS, 1), lambda h, i, pt: (0, 0)),
            pl.BlockSpec(memory_space=pl.ANY),
            pl.BlockSpec(memory_space=pl.ANY),
        ],
        out_specs=(pl.BlockSpec((tq, HEAD_DIM), lambda h, i, pt: (i, h)),
                   pl.BlockSpec((b, N_HEADS, HEAD_DIM), lambda h, i, pt: (0, 0, 0))),
        scratch_shapes=[
            pltpu.VMEM((tq, HEAD_DIM), F32),
            pltpu.VMEM((tq, LANES), F32),
            pltpu.VMEM((2, group, page, N_HEADS, HEAD_DIM), F32),
            pltpu.VMEM((2, group, page, N_HEADS, HEAD_DIM), F32),
            pltpu.SemaphoreType.DMA((2, 2)),
            pltpu.VMEM((N_HEADS, HEAD_DIM), F32),
            pltpu.VMEM((N_HEADS, LANES), F32),
        ],
    )
    o_a, o_s = pl.pallas_call(
        functools.partial(_sb_attention_kernel, th=th, tk=tk, group=group),
        out_shape=(jax.ShapeDtypeStruct((s, W_A), F32),
                   jax.ShapeDtypeStruct((b, N_HEADS, HEAD_DIM), F32)),
        grid_spec=grid_spec,
        compiler_params=_params(("arbitrary", "arbitrary")),
        name="sb_attention",
    )(page_table, bias, qkv, qkv, qkv, q_s.reshape(b, N_HEADS, HEAD_DIM).astype(F32),
      bias.reshape(N_HEADS, 1), cache_k, cache_v)
    return o_a, o_s.reshape(b, W_A)


def _suffix_and_total(n):
    r = lax.broadcasted_iota(jnp.int32, (n, 2 * n), 0)
    c = lax.broadcasted_iota(jnp.int32, (n, 2 * n), 1)
    return jnp.where((r > c) | (c >= n), 1.0, 0.0).astype(BF16)


def _own_lanes():
    lane = lax.broadcasted_iota(jnp.int32, (N_HEADS, LANES), 1)
    row = lax.broadcasted_iota(jnp.int32, (N_HEADS, LANES), 0)
    return (lane % N_HEADS) == row


def _newest_first(n_tiles, tiles_per_page):
    return [g * tiles_per_page + j for g in range(n_tiles // tiles_per_page)
            for j in reversed(range(tiles_per_page))]


def _decode_qk(q, bias2, k_pages):
    flat = k_pages[0].shape[0] * N_HEADS
    km = jnp.concatenate([kp.reshape(flat, HEAD_DIM).astype(BF16) for kp in k_pages], axis=0)
    z2 = lax.dot_general(q.astype(BF16), km, (((1,), (1,)), ((), ())),
                         preferred_element_type=F32)
    z2 = z2 + bias2
    sp2 = _softplus_2(z2)
    ls2 = z2 - sp2
    own_t = _own_lanes()
    order = _newest_first(z2.shape[1] // LANES, flat // LANES)
    stacked = jnp.concatenate(
        [jnp.where(own_t, sp2[:, t * LANES:(t + 1) * LANES], 0.0) for t in order], axis=0)
    hi = stacked.astype(BF16)
    lo = (stacked - hi.astype(F32)).astype(BF16)
    return ls2, jnp.concatenate([hi, lo], axis=0)


def _decode_suffix(stacked):
    s = jnp.dot(stacked, _suffix_and_total(LANES), preferred_element_type=F32)
    half = s.shape[0] // 2
    return s[:half] + s[half:]


def _decode_pv(ls2, sums, v_pages, acc, run):
    flat = v_pages[0].shape[0] * N_HEADS
    own_t = _own_lanes()
    n_tiles = ls2.shape[1] // LANES
    ws = [None] * n_tiles
    for n, t in enumerate(_newest_first(n_tiles, flat // LANES)):
        rows = slice(n * N_HEADS, (n + 1) * N_HEADS)
        later = sums[rows, :LANES] + run
        ws[t] = jnp.where(own_t, jnp.exp2(ls2[:, t * LANES:(t + 1) * LANES] - later), 0.0)
        run = run + sums[rows, LANES:]
    w = jnp.concatenate(ws, axis=1).astype(BF16)
    vm = jnp.concatenate([vp.reshape(flat, HEAD_DIM).astype(BF16) for vp in v_pages], axis=0)
    return acc + jnp.dot(w, vm, preferred_element_type=F32), run


def _pool_mix_gate(window_sums, p, count_inv, o_a, g, wp_ref, scale_ref, a_ref):
    a_ref[:, :W_A] = (o_a * _silu(g[:, :W_A])).astype(BF16)
    for gi in range(len(POOL_WINDOWS)):
        cols = slice(gi * G_B, (gi + 1) * G_B)
        pooled = window_sums[gi] * count_inv[gi] - p[:, cols]
        mixed = jnp.dot(pooled.astype(BF16), wp_ref[gi], preferred_element_type=F32)
        o_b = mixed * scale_ref[:, cols]
        a_ref[:, W_A + gi * G_B:W_A + (gi + 1) * G_B] = (
            o_b * _silu(g[:, W_A + gi * G_B:W_A + (gi + 1) * G_B])).astype(BF16)


def _ab_mix_prompt_kernel(p_ref, halo_ref, oa_ref, g_ref, wp_ref, scale_ref, a_ref, buf_ref, *, tm):
    i = pl.program_id(0)
    buf_ref[:HALO, :] = jnp.where(i > 0, halo_ref[...], 0.0)
    buf_ref[HALO:, :] = p_ref[...]
    pos = i * tm + lax.broadcasted_iota(jnp.int32, (tm, 1), 0)
    sums, invs = [], []
    for gi, wnd in enumerate(POOL_WINDOWS):
        cols = slice(gi * G_B, (gi + 1) * G_B)
        acc = buf_ref[HALO:, cols]
        for back in range(1, wnd):
            acc = acc + buf_ref[HALO - back:HALO - back + tm, cols]
        sums.append(acc)
        invs.append(1.0 / jnp.minimum(wnd, pos + 1).astype(F32))
    _pool_mix_gate(sums, p_ref[...], invs, oa_ref[...], g_ref[...], wp_ref, scale_ref, a_ref)


def _ab_mix_prompt(z, o_a, w_pool, scale, *, tm):
    m = z.shape[0]
    halo_blocks = tm // HALO
    p_col = 3 * W_A // W_B
    g_col = (3 * W_A + W_B) // W_AB
    return pl.pallas_call(
        functools.partial(_ab_mix_prompt_kernel, tm=tm),
        out_shape=jax.ShapeDtypeStruct((m, W_AB), BF16),
        grid=(m // tm,),
        in_specs=[
            pl.BlockSpec((tm, W_B), lambda i: (i, p_col)),
            pl.BlockSpec((HALO, W_B), lambda i: (jnp.maximum(i * halo_blocks - 1, 0), p_col)),
            pl.BlockSpec((tm, W_A), lambda i: (i, 0)),
            pl.BlockSpec((tm, W_AB), lambda i: (i, g_col)),
            pl.BlockSpec((len(POOL_WINDOWS), G_B, G_B), lambda i: (0, 0, 0)),
            pl.BlockSpec((1, W_B), lambda i: (0, 0)),
        ],
        out_specs=pl.BlockSpec((tm, W_AB), lambda i: (i, 0)),
        scratch_shapes=[pltpu.VMEM((tm + HALO, W_B), F32)],
        compiler_params=_params(("parallel",)),
        name="ab_mix_prompt",
    )(z, z, o_a, z, w_pool, scale.reshape(1, W_B))


def _ab_mix_sample_kernel(p_ref, prev_ref, oa_ref, g_ref, wp_ref, scale_ref, a_ref):
    p = p_ref[...]
    sums, invs = [], []
    for gi, wnd in enumerate(POOL_WINDOWS):
        cols = slice(gi * G_B, (gi + 1) * G_B)
        acc = p[:, cols]
        for back in range(1, wnd):
            acc = acc + prev_ref[POOL_BUF - back, :, cols]
        sums.append(acc)
        invs.append(1.0 / wnd)
    _pool_mix_gate(sums, p, invs, oa_ref[...], g_ref[...], wp_ref, scale_ref, a_ref)


def _ab_mix_sample(p, prev_t, o_a, g, w_pool, scale):
    m = p.shape[0]
    return pl.pallas_call(
        _ab_mix_sample_kernel,
        out_shape=jax.ShapeDtypeStruct((m, W_AB), BF16),
        name="ab_mix_sample",
        compiler_params=pltpu.CompilerParams(vmem_limit_bytes=VMEM_LIMIT),
    )(p, prev_t, o_a, g, w_pool, scale.reshape(1, W_B))


def _out_proj_kernel(a_ref, w_ref, x_ref, y_ref):
    y_ref[...] = x_ref[...] + jnp.dot(a_ref[...], w_ref[...], preferred_element_type=F32)


def _out_proj(a, w, x, *, tm, tn):
    m, kdim = a.shape
    n = w.shape[1]
    return pl.pallas_call(
        _out_proj_kernel,
        out_shape=jax.ShapeDtypeStruct((m, n), F32),
        grid=(m // tm, n // tn),
        in_specs=[
            pl.BlockSpec((tm, kdim), lambda i, j: (i, 0)),
            pl.BlockSpec((kdim, tn), lambda i, j: (0, j)),
            pl.BlockSpec((tm, tn), lambda i, j: (i, j)),
        ],
        out_specs=pl.BlockSpec((tm, tn), lambda i, j: (i, j)),
        compiler_params=_params(("parallel", "parallel")),
        name="out_proj",
    )(a, w, x)


def _c_gate_prompt_kernel(u_ref, vg_ref, g_ref, vgain_ref, ws_ref, bst_ref, a_ref, vrow_ref, *,
                          chunks):
    gc = u_ref.shape[1] // H_C
    r = lax.broadcasted_iota(jnp.int32, (CHUNK_C, CHUNK_C), 0)
    c = lax.broadcasted_iota(jnp.int32, (CHUNK_C, CHUNK_C), 1)
    lower = r >= c
    for ch in range(chunks):
        rows = slice(ch * CHUNK_C, (ch + 1) * CHUNK_C)
        v = _rms_rows(vg_ref[rows, :], vgain_ref[...])
        if ch == chunks - 1:
            vrow_ref[...] = v
        for hd in range(H_C):
            cols = slice(hd * gc, (hd + 1) * gc)
            ws = jnp.where(lower, ws_ref[hd], 0.0).astype(BF16)
            mixed = jnp.dot(ws, v[:, cols].astype(BF16), preferred_element_type=F32)
            mixed = mixed + bst_ref[:, hd:hd + 1]
            a_ref[rows, cols] = (u_ref[rows, cols] * mixed * _silu(g_ref[rows, cols])).astype(BF16)


def _c_gate_prompt(z, v_gain, w_s, b_s, *, chunks):
    m = z.shape[0]
    wc = z.shape[1] // 3
    tm = chunks * CHUNK_C
    return pl.pallas_call(
        functools.partial(_c_gate_prompt_kernel, chunks=chunks),
        out_shape=(jax.ShapeDtypeStruct((m, wc), BF16),
                   jax.ShapeDtypeStruct((CHUNK_C, wc), F32)),
        grid=(m // tm,),
        in_specs=[
            pl.BlockSpec((tm, wc), lambda i: (i, 0)),
            pl.BlockSpec((tm, wc), lambda i: (i, 1)),
            pl.BlockSpec((tm, wc), lambda i: (i, 2)),
            pl.BlockSpec((1, wc), lambda i: (0, 0)),
            pl.BlockSpec((H_C, CHUNK_C, CHUNK_C), lambda i: (0, 0, 0)),
            pl.BlockSpec((CHUNK_C, H_C), lambda i: (0, 0)),
        ],
        out_specs=(pl.BlockSpec((tm, wc), lambda i: (i, 0)),
                   pl.BlockSpec((CHUNK_C, wc), lambda i: (0, 0))),
        compiler_params=_params(("arbitrary",)),
        name="c_gate_prompt",
    )(z, z, z, v_gain.reshape(1, wc), w_s, b_s.T)


def _c_gate_sample_kernel(u_ref, vg_ref, g_ref, vgain_ref, w00_ref, b0_ref, a_ref, vrow_ref):
    v = _rms_rows(vg_ref[...], vgain_ref[...])
    vrow_ref[...] = v
    mixed = w00_ref[...] * v + b0_ref[...]
    a_ref[...] = (u_ref[...] * mixed * _silu(g_ref[...])).astype(BF16)


def _c_gate_sample(u, vg, g, v_gain, w_s, b_s):
    m, wc = u.shape
    gc = wc // H_C
    w00 = jnp.repeat(w_s[:, 0, 0], gc).reshape(1, wc)
    b0 = jnp.repeat(b_s[:, 0], gc).reshape(1, wc)
    return pl.pallas_call(
        _c_gate_sample_kernel,
        out_shape=(jax.ShapeDtypeStruct((m, wc), BF16), jax.ShapeDtypeStruct((m, wc), F32)),
        name="c_gate_sample",
    )(u, vg, g, v_gain.reshape(1, wc), w00, b0)


def kernel(x_prompt, x_sample, cache_k, cache_v, state_pool, page_table, norm_gain, w_in_ab, q_gain,
           k_gain, sb_bias, w_pool, pool_scale, w_out_ab, w_in_c, v_gain, w_spatial, b_spatial,
           w_out_c):
    bp, s, d = x_prompt.shape
    bs, t, _ = x_sample.shape
    assert bp == 1 and t == 1
    xp = x_prompt.reshape(s, d)
    xs = x_sample.reshape(bs, d)

    w_in0 = w_in_ab[0].astype(BF16)
    w_pool0 = w_pool[0].astype(BF16)
    w_out0 = w_out_ab[0].astype(BF16)
    w_in1 = w_in_c[0].astype(BF16)
    w_out1 = w_out_c[0].astype(BF16)
    colgain = jnp.concatenate([jnp.tile(q_gain[0] * (SM_SCALE * LOG2E), N_HEADS),
                               jnp.tile(k_gain[0], N_HEADS),
                               jnp.ones((W_A + W_B + W_AB,), F32)])
    ab_in = functools.partial(_in_proj, gain=norm_gain[0], w=w_in0, tn=IN_PROJ_TN, colgain=colgain,
                              norm_cols=2 * W_A, bf16_cols=3 * W_A, name="ab_in_proj")
    c_in = functools.partial(_in_proj, gain=norm_gain[1], w=w_in1, tn=IN_PROJ_TN,
                             gelu_cols=2 * d, name="c_in_proj")

    z0, qkv = ab_in(xp, tm=IN_PROJ_TM)
    z0s, qkv_s = ab_in(xs, tm=bs)
    o_a, oa_s = _sb_attention(qkv, qkv_s[:, :W_A], cache_k[0], cache_v[0], page_table, sb_bias[0],
                              th=ATTN_HALF_ROWS, tk=ATTN_SUFFIX_BLOCK, group=DECODE_PAGES_PER_STEP)

    a = _ab_mix_prompt(z0, o_a, w_pool0, pool_scale[0], tm=AB_MIX_TM)
    xp1 = _out_proj(a, w_out0, xp, tm=OUT_PROJ_TM, tn=OUT_PROJ_TN)
    new_k_p = z0[:, W_A:2 * W_A].reshape(1, bp, s, N_HEADS, HEAD_DIM)
    new_v_p = z0[:, 2 * W_A:3 * W_A].reshape(1, bp, s, N_HEADS, HEAD_DIM)
    new_pool_p = z0[s - POOL_BUF:, 3 * W_A:3 * W_A + W_B].reshape(1, bp, POOL_BUF, W_B)

    ps = z0s[:, 3 * W_A:3 * W_A + W_B]
    prev_t = jnp.swapaxes(state_pool[0], 0, 1)
    a_s = _ab_mix_sample(ps, prev_t, oa_s, z0s[:, 3 * W_A + W_B:], w_pool0, pool_scale[0])
    xs1 = _out_proj(a_s, w_out0, xs, tm=bs, tn=OUT_PROJ_TN)
    new_k_s = z0s[:, W_A:2 * W_A].reshape(1, bs, t, N_HEADS, HEAD_DIM)
    new_v_s = z0s[:, 2 * W_A:3 * W_A].reshape(1, bs, t, N_HEADS, HEAD_DIM)
    new_pool_s = jnp.concatenate([state_pool[0][:, 1:], ps[:, None, :]], axis=1)[None]

    (z1,) = c_in(xp1, tm=IN_PROJ_TM)
    a1, v_last = _c_gate_prompt(z1, v_gain[0], w_spatial[0], b_spatial[0], chunks=C_GATE_CHUNKS)
    yp = _out_proj(a1, w_out1, xp1, tm=OUT_PROJ_TM, tn=OUT_PROJ_TN)

    (z1s,) = c_in(xs1, tm=bs)
    a1s, v_s = _c_gate_sample(z1s[:, :d], z1s[:, d:2 * d], z1s[:, 2 * d:], v_gain[0], w_spatial[0],
                              b_spatial[0])
    ys = _out_proj(a1s, w_out1, xs1, tm=bs, tn=OUT_PROJ_TN)

    return (yp.reshape(bp, s, d), ys.reshape(bs, t, d), new_k_p, new_v_p, new_k_s, new_v_s,
            new_pool_p, new_pool_s, v_last.reshape(1, bp, CHUNK_C, d), v_s.reshape(1, bs, t, d))
```

```python
import functools
import math

import jax
import jax.numpy as jnp
from jax import lax
from jax.experimental import pallas as pl
from jax.experimental.pallas import tpu as pltpu

F32 = jnp.float32
BF16 = jnp.bfloat16

LANES = 128
HEAD_DIM = 128
N_HEADS = 8
W_A = N_HEADS * HEAD_DIM
POOL_WINDOWS = (2, 4, 8, 16)
G_B = 256
W_B = G_B * len(POOL_WINDOWS)
W_AB = W_A + W_B
POOL_BUF = max(POOL_WINDOWS) - 1
HALO = 16
CHUNK_C = 128
H_C = 8
EPS = 1e-6
SM_SCALE = 1.0 / math.sqrt(HEAD_DIM)
LOG2E = math.log2(math.e)
GELU_C = math.sqrt(2.0 / math.pi)
VMEM_LIMIT = 56 * 1024 * 1024

IN_PROJ_TM = 1024
AB_IN_PROJ_TN, C_IN_PROJ_TN = 1024, 512
IN_PROJ_ROW_CHUNK = 256
OUT_PROJ_TM, OUT_PROJ_TN = 1024, 1024
ATTN_HALF_ROWS, ATTN_SUFFIX_BLOCK = 512, 256
DECODE_PAGES_PER_STEP = 8
AB_MIX_TM = 512
C_GATE_CHUNKS = 4


def _params(semantics):
    return pltpu.CompilerParams(dimension_semantics=semantics, vmem_limit_bytes=VMEM_LIMIT)


def _softplus_2(z2):
    neg_abs = lax.bitcast_convert_type(
        lax.bitcast_convert_type(z2, jnp.uint32) | jnp.uint32(0x80000000), F32)
    return jnp.maximum(z2, 0.0) + jnp.log(1.0 + jnp.exp2(neg_abs)) * LOG2E


def _silu(g):
    return g * jax.nn.sigmoid(g)


def _gelu_tanh(x):
    return 0.5 * x * (1.0 + jnp.tanh(GELU_C * (x + 0.044715 * (x * x * x))))


def _rms_rows(xf, gain):
    ms = jnp.mean(xf * xf, axis=-1, keepdims=True)
    return xf * lax.rsqrt(ms + EPS) * gain


def _in_proj_kernel(x_ref, gain_ref, w_ref, *refs, norm_blocks, gelu_blocks, bf16_blocks):
    if norm_blocks:
        colgain_ref, refs = refs[0], refs[1:]
    if bf16_blocks:
        zf_ref, zb_ref, h_ref = refs
    else:
        zf_ref, h_ref = refs
    j = pl.program_id(1)

    @pl.when(j == 0)
    def _():
        h_ref[...] = _rms_rows(x_ref[...], gain_ref[...]).astype(BF16)

    tm = h_ref.shape[0]
    step = min(tm, IN_PROJ_ROW_CHUNK)
    for r0 in range(0, tm, step):
        rows = slice(r0, r0 + step)
        z = jnp.dot(h_ref[rows, :], w_ref[...], preferred_element_type=F32)
        if norm_blocks:
            is_norm = j < norm_blocks
            cols = []
            for c in range(z.shape[1] // HEAD_DIM):
                zc = z[:, c * HEAD_DIM:(c + 1) * HEAD_DIM]
                rs = lax.rsqrt(jnp.mean(zc * zc, axis=-1, keepdims=True) + EPS)
                cols.append(zc * jnp.where(is_norm, rs, 1.0))
            z = jnp.concatenate(cols, axis=1) * colgain_ref[...]
        if gelu_blocks:
            z = jnp.where(j < gelu_blocks, _gelu_tanh(z), z)
        zf_ref[rows, :] = z
        if bf16_blocks:
            zb_ref[rows, :] = z.astype(BF16)


def _in_proj(x, gain, w, *, tm, tn, colgain=None, norm_cols=0, gelu_cols=0, bf16_cols=0, name):
    m, d = x.shape
    n = w.shape[1]
    bf16_blocks = bf16_cols // tn
    in_specs = [
        pl.BlockSpec((tm, d), lambda i, j: (i, 0)),
        pl.BlockSpec((1, d), lambda i, j: (0, 0)),
        pl.BlockSpec((d, tn), lambda i, j: (0, j)),
    ]
    args = [x, gain.reshape(1, d), w]
    if norm_cols:
        in_specs.append(pl.BlockSpec((1, tn), lambda i, j: (0, j)))
        args.append(colgain.reshape(1, n))
    out_shape = [jax.ShapeDtypeStruct((m, n), F32)]
    out_specs = [pl.BlockSpec((tm, tn), lambda i, j: (i, j))]
    if bf16_blocks:
        out_shape.append(jax.ShapeDtypeStruct((m, bf16_cols + tn), BF16))
        out_specs.append(pl.BlockSpec((tm, tn), lambda i, j: (i, jnp.minimum(j, bf16_blocks))))
    return pl.pallas_call(
        functools.partial(_in_proj_kernel, norm_blocks=norm_cols // tn, gelu_blocks=gelu_cols // tn,
                          bf16_blocks=bf16_blocks),
        out_shape=tuple(out_shape),
        grid=(m // tm, n // tn),
        in_specs=in_specs,
        out_specs=tuple(out_specs),
        scratch_shapes=[pltpu.VMEM((tm, d), BF16)],
        compiler_params=_params(("parallel", "arbitrary")),
        name=name,
    )(*args)


def _strict_upper(n):
    r = lax.broadcasted_iota(jnp.int32, (n, n), 0)
    c = lax.broadcasted_iota(jnp.int32, (n, n), 1)
    return jnp.where(r > c, 1.0, 0.0).astype(BF16)


def _sb_attention_kernel(pt_ref, bias_ref, q_ref, k_ref, v_ref, qs_ref, bias_col_ref, ck_ref, cv_ref,
                         o_ref, os_ref, acc_ref, r_ref, kbuf_ref, vbuf_ref, ksem_ref, vsem_ref,
                         accs_ref, rs_ref, dls_ref, dsum_ref, *, th, tk, group):
    h = pl.program_id(0)
    i = pl.program_id(1)
    n_seq, n_pages = pt_ref.shape
    groups_per_seq = n_pages // group
    n_groups = n_seq * groups_per_seq
    groups_per_head = n_groups // pl.num_programs(0)
    first_step = (h == 0) & (i == 0)
    last_step = (h == pl.num_programs(0) - 1) & (i == pl.num_programs(1) - 1)

    def page_copies(cache_ref, buf_ref, sems, n, slot):
        seq = n // groups_per_seq
        pg = n % groups_per_seq
        return [pltpu.make_async_copy(
            cache_ref.at[pt_ref[seq, n_pages - 1 - (pg * group + g)]], buf_ref.at[slot, g],
            sems.at[slot]) for g in range(group)]

    def k_copies(n, slot):
        return page_copies(ck_ref, kbuf_ref, ksem_ref, n, slot)

    def v_copies(n, slot):
        return page_copies(cv_ref, vbuf_ref, vsem_ref, n, slot)

    def decode_second_half(m, v_slot):
        seq = m // groups_per_seq
        fresh = (m % groups_per_seq) == 0
        acc0 = jnp.where(fresh, 0.0, accs_ref[...])
        run0 = jnp.where(fresh, 0.0, rs_ref[...])
        acc, run = _decode_pv(dls_ref[...], dsum_ref[...],
                              [vbuf_ref[v_slot, g] for g in range(group)], acc0, run0)
        accs_ref[...] = acc
        rs_ref[...] = run
        os_ref[seq] = acc

    def decode_fetch(n):
        k_slot = n % 2
        for c in k_copies(n, k_slot) + v_copies(jnp.maximum(n - 1, 0), (n + 2) % 3):
            c.wait()
        nxt = jnp.minimum(n + 1, n_groups - 1)
        for c in k_copies(nxt, 1 - k_slot) + v_copies(nxt, (n + 1) % 3):
            c.start()

    def decode_first_half(n):
        ls2, stacked = _decode_qk(qs_ref[n // groups_per_seq], bias_col_ref[...] * LOG2E,
                                  [kbuf_ref[n % 2, g] for g in range(group)])
        dls_ref[...] = ls2
        dsum_ref[...] = _decode_suffix(stacked)

    def decode_previous(n):
        decode_second_half(jnp.maximum(n - 1, 0), (n + 2) % 3)

    @pl.when(first_step)
    def _():
        for ref in (accs_ref, rs_ref, dls_ref, dsum_ref, os_ref):
            ref[...] = jnp.zeros_like(ref)
        for c in k_copies(0, 0) + v_copies(0, 0) + v_copies(0, 2):
            c.start()

    bias2 = bias_ref[h] * LOG2E
    upper = _strict_upper(tk)
    nsub = th // tk
    half_rows = [slice(half * th, (half + 1) * th) for half in range(2)]
    even = 1 - i % 2
    group_base = h * groups_per_head + (i * (i - 1)) // 2 + (i + 1) // 2

    def qk_stage(half, kt, masked):
        ks = pl.multiple_of(kt * th, th)
        z2 = lax.dot_general(q_ref[half_rows[half], :], k_ref[pl.ds(ks, th), :],
                             (((1,), (1,)), ((), ())), preferred_element_type=F32) + bias2
        sp2 = _softplus_2(z2)
        ls2 = z2 - sp2
        valid = None
        if masked:
            t_pos = lax.broadcasted_iota(jnp.int32, (th, th), 0)
            s_pos = lax.broadcasted_iota(jnp.int32, (th, th), 1)
            valid = s_pos < t_pos
            sp2 = jnp.where(valid, sp2, 0.0)
        blocks = [sp2[:, c * tk:(c + 1) * tk] for c in range(nsub)]
        totals = [jnp.broadcast_to(jnp.sum(b, axis=-1, keepdims=True), (th, LANES)) for b in blocks]
        return ks, ls2, jnp.concatenate(blocks, axis=0).astype(BF16), totals, valid

    def suffix_stage(st):
        return jnp.dot(st[2], upper, preferred_element_type=F32)

    def pv_stage(st, later, run):
        ks, ls2, _, totals, valid = st
        ws = [None] * nsub
        for c in reversed(range(nsub)):
            offs = jnp.concatenate([run] * (tk // LANES), axis=1)
            ws[c] = jnp.exp2(ls2[:, c * tk:(c + 1) * tk] - later[c * th:(c + 1) * th] - offs)
            run = run + totals[c]
        w = jnp.concatenate(ws, axis=1)
        if valid is not None:
            w = jnp.where(valid, w, 0.0)
        return jnp.dot(w.astype(BF16), v_ref[pl.ds(ks, th), :], preferred_element_type=F32), run

    def diagonal(with_decode):
        if with_decode:
            decode_fetch(group_base)
        zero = jnp.zeros((th, LANES), F32)
        st = qk_stage(1, 2 * i + 1, True)
        pv_hi, run1 = pv_stage(st, suffix_stage(st), zero)
        if with_decode:
            decode_previous(group_base)
        st = qk_stage(0, 2 * i, True)
        pv0, run0 = pv_stage(st, suffix_stage(st), zero)
        if with_decode:
            decode_first_half(group_base)
        st = qk_stage(1, 2 * i, False)
        pv_lo, run1 = pv_stage(st, suffix_stage(st), run1)
        acc_ref[half_rows[0], :] = pv0
        acc_ref[half_rows[1], :] = pv_hi + pv_lo
        r_ref[half_rows[0], :] = run0
        r_ref[half_rows[1], :] = run1

    pl.when(even == 1)(functools.partial(diagonal, True))
    pl.when(even == 0)(functools.partial(diagonal, False))

    def body(it, carry):
        kt = 2 * i - 1 - 2 * it
        n = group_base + even + it
        decode_fetch(n)
        pvs = [None] * 4
        runs = [r_ref[rows, :] for rows in half_rows]
        for t, (half, ktile) in enumerate([(0, kt), (1, kt), (0, kt - 1), (1, kt - 1)]):
            st = qk_stage(half, ktile, False)
            pvs[t], runs[half] = pv_stage(st, suffix_stage(st), runs[half])
            if t == 0:
                decode_previous(n)
            if t == 2:
                decode_first_half(n)
        for half, rows in enumerate(half_rows):
            acc_ref[rows, :] += pvs[half] + pvs[half + 2]
            r_ref[rows, :] = runs[half]
        return carry

    lax.fori_loop(0, i, body, 0)
    o_ref[...] = acc_ref[...]

    @pl.when(last_step)
    def _():
        last = n_groups - 1
        for c in v_copies(last, last % 3):
            c.wait()
        decode_second_half(last, last % 3)
        for c in k_copies(last, n_groups % 2) + v_copies(last, n_groups % 3):
            c.wait()


def _sb_attention(qkv, q_s, cache_k, cache_v, page_table, bias, *, th, tk, group):
    s = qkv.shape[0]
    b = q_s.shape[0]
    page = cache_k.shape[1]
    n_pages = page_table.shape[1]
    tq = 2 * th
    steps = s // tq
    keys = group * page
    assert b * (n_pages // group) == N_HEADS * (steps * (steps - 1) // 2 + (steps + 1) // 2)
    grid_spec = pltpu.PrefetchScalarGridSpec(
        num_scalar_prefetch=1,
        grid=(N_HEADS, steps),
        in_specs=[
            pl.BlockSpec(memory_space=pltpu.SMEM),
            pl.BlockSpec((tq, HEAD_DIM), lambda h, i, pt: (i, h)),
            pl.BlockSpec((s, HEAD_DIM), lambda h, i, pt: (0, N_HEADS + h)),
            pl.BlockSpec((s, HEAD_DIM), lambda h, i, pt: (0, 2 * N_HEADS + h)),
            pl.BlockSpec((b, N_HEADS, HEAD_DIM), lambda h, i, pt: (0, 0, 0)),
            pl.BlockSpec((N_HEADS, 1), lambda h, i, pt: (0, 0)),
            pl.BlockSpec(memory_space=pl.ANY),
            pl.BlockSpec(memory_space=pl.ANY),
        ],
        out_specs=(pl.BlockSpec((tq, HEAD_DIM), lambda h, i, pt: (i, h)),
                   pl.BlockSpec((b, N_HEADS, HEAD_DIM), lambda h, i, pt: (0, 0, 0))),
        scratch_shapes=[
            pltpu.VMEM((tq, HEAD_DIM), F32),
            pltpu.VMEM((tq, LANES), F32),
            pltpu.VMEM((2, group, page, N_HEADS, HEAD_DIM), F32),
            pltpu.VMEM((3, group, page, N_HEADS, HEAD_DIM), F32),
            pltpu.SemaphoreType.DMA((2,)),
            pltpu.SemaphoreType.DMA((3,)),
            pltpu.VMEM((N_HEADS, HEAD_DIM), F32),
            pltpu.VMEM((N_HEADS, LANES), F32),
            pltpu.VMEM((N_HEADS, keys * N_HEADS), F32),
            pltpu.VMEM((keys * N_HEADS // LANES * N_HEADS, 2 * LANES), F32),
        ],
    )
    o_a, o_s = pl.pallas_call(
        functools.partial(_sb_attention_kernel, th=th, tk=tk, group=group),
        out_shape=(jax.ShapeDtypeStruct((s, W_A), F32),
                   jax.ShapeDtypeStruct((b, N_HEADS, HEAD_DIM), F32)),
        grid_spec=grid_spec,
        compiler_params=_params(("arbitrary", "arbitrary")),
        name="sb_attention",
    )(page_table, bias, qkv, qkv, qkv, q_s.reshape(b, N_HEADS, HEAD_DIM).astype(F32),
      bias.reshape(N_HEADS, 1), cache_k, cache_v)
    return o_a, o_s.reshape(b, W_A)


def _suffix_and_total(n):
    r = lax.broadcasted_iota(jnp.int32, (n, 2 * n), 0)
    c = lax.broadcasted_iota(jnp.int32, (n, 2 * n), 1)
    return jnp.where((r > c) | (c >= n), 1.0, 0.0).astype(BF16)


def _own_lanes():
    lane = lax.broadcasted_iota(jnp.int32, (N_HEADS, LANES), 1)
    row = lax.broadcasted_iota(jnp.int32, (N_HEADS, LANES), 0)
    return (lane % N_HEADS) == row


def _newest_first(n_tiles, tiles_per_page):
    return [g * tiles_per_page + j for g in range(n_tiles // tiles_per_page)
            for j in reversed(range(tiles_per_page))]


def _decode_qk(q, bias2, k_pages):
    page = k_pages[0].shape[0]
    flat = page * N_HEADS
    keys_per_tile = LANES // N_HEADS
    lane = lax.broadcasted_iota(jnp.int32, (N_HEADS, LANES), 1)
    row = lax.broadcasted_iota(jnp.int32, (N_HEADS, LANES), 0)
    at_key = [lane == c * N_HEADS + row for c in range(keys_per_tile)]
    tiles = []
    for kp in k_pages:
        cols = jnp.sum(kp * q[None], axis=-1, keepdims=True)
        for j in range(page // keys_per_tile):
            zt = jnp.zeros((N_HEADS, LANES), F32)
            for c in range(keys_per_tile):
                col = jnp.broadcast_to(cols[j * keys_per_tile + c], (N_HEADS, LANES))
                zt = jnp.where(at_key[c], col, zt)
            tiles.append(zt)
    z2 = jnp.concatenate(tiles, axis=1) + bias2
    sp2 = _softplus_2(z2)
    ls2 = z2 - sp2
    own_t = _own_lanes()
    order = _newest_first(z2.shape[1] // LANES, flat // LANES)
    stacked = jnp.concatenate(
        [jnp.where(own_t, sp2[:, t * LANES:(t + 1) * LANES], 0.0) for t in order], axis=0)
    hi = stacked.astype(BF16)
    lo = (stacked - hi.astype(F32)).astype(BF16)
    return ls2, jnp.concatenate([hi, lo], axis=0)


def _decode_suffix(stacked):
    s = jnp.dot(stacked, _suffix_and_total(LANES), preferred_element_type=F32)
    half = s.shape[0] // 2
    return s[:half] + s[half:]


def _decode_pv(ls2, sums, v_pages, acc, run):
    flat = v_pages[0].shape[0] * N_HEADS
    own_t = _own_lanes()
    n_tiles = ls2.shape[1] // LANES
    ws = [None] * n_tiles
    for n, t in enumerate(_newest_first(n_tiles, flat // LANES)):
        rows = slice(n * N_HEADS, (n + 1) * N_HEADS)
        later = sums[rows, :LANES] + run
        ws[t] = jnp.where(own_t, jnp.exp2(ls2[:, t * LANES:(t + 1) * LANES] - later), 0.0)
        run = run + sums[rows, LANES:]
    w = jnp.concatenate(ws, axis=1).astype(BF16)
    vm = jnp.concatenate([vp.reshape(flat, HEAD_DIM).astype(BF16) for vp in v_pages], axis=0)
    return acc + jnp.dot(w, vm, preferred_element_type=F32), run


def _pool_mix_gate(window_sums, p, count_inv, o_a, g, wp_ref, scale_ref, a_ref):
    a_ref[:, :W_A] = (o_a * _silu(g[:, :W_A])).astype(BF16)
    for gi in range(len(POOL_WINDOWS)):
        cols = slice(gi * G_B, (gi + 1) * G_B)
        pooled = window_sums[gi] * count_inv[gi] - p[:, cols]
        mixed = jnp.dot(pooled.astype(BF16), wp_ref[gi], preferred_element_type=F32)
        o_b = mixed * scale_ref[:, cols]
        a_ref[:, W_A + gi * G_B:W_A + (gi + 1) * G_B] = (
            o_b * _silu(g[:, W_A + gi * G_B:W_A + (gi + 1) * G_B])).astype(BF16)


def _ab_mix_prompt_kernel(p_ref, halo_ref, oa_ref, g_ref, wp_ref, scale_ref, a_ref, buf_ref, *, tm):
    i = pl.program_id(0)
    buf_ref[:HALO, :] = jnp.where(i > 0, halo_ref[...], 0.0)
    buf_ref[HALO:, :] = p_ref[...]
    pos = i * tm + lax.broadcasted_iota(jnp.int32, (tm, 1), 0)
    sums, invs = [], []
    for gi, wnd in enumerate(POOL_WINDOWS):
        cols = slice(gi * G_B, (gi + 1) * G_B)
        acc = buf_ref[HALO:, cols]
        for back in range(1, wnd):
            acc = acc + buf_ref[HALO - back:HALO - back + tm, cols]
        sums.append(acc)
        invs.append(1.0 / jnp.minimum(wnd, pos + 1).astype(F32))
    _pool_mix_gate(sums, p_ref[...], invs, oa_ref[...], g_ref[...], wp_ref, scale_ref, a_ref)


def _ab_mix_prompt(z, o_a, w_pool, scale, *, tm):
    m = z.shape[0]
    halo_blocks = tm // HALO
    p_col = 3 * W_A // W_B
    g_col = (3 * W_A + W_B) // W_AB
    return pl.pallas_call(
        functools.partial(_ab_mix_prompt_kernel, tm=tm),
        out_shape=jax.ShapeDtypeStruct((m, W_AB), BF16),
        grid=(m // tm,),
        in_specs=[
            pl.BlockSpec((tm, W_B), lambda i: (i, p_col)),
            pl.BlockSpec((HALO, W_B), lambda i: (jnp.maximum(i * halo_blocks - 1, 0), p_col)),
            pl.BlockSpec((tm, W_A), lambda i: (i, 0)),
            pl.BlockSpec((tm, W_AB), lambda i: (i, g_col)),
            pl.BlockSpec((len(POOL_WINDOWS), G_B, G_B), lambda i: (0, 0, 0)),
            pl.BlockSpec((1, W_B), lambda i: (0, 0)),
        ],
        out_specs=pl.BlockSpec((tm, W_AB), lambda i: (i, 0)),
        scratch_shapes=[pltpu.VMEM((tm + HALO, W_B), F32)],
        compiler_params=_params(("parallel",)),
        name="ab_mix_prompt",
    )(z, z, o_a, z, w_pool, scale.reshape(1, W_B))


def _ab_mix_sample_kernel(p_ref, prev_ref, oa_ref, g_ref, wp_ref, scale_ref, a_ref):
    p = p_ref[...]
    sums, invs = [], []
    for gi, wnd in enumerate(POOL_WINDOWS):
        cols = slice(gi * G_B, (gi + 1) * G_B)
        acc = p[:, cols]
        for back in range(1, wnd):
            acc = acc + prev_ref[POOL_BUF - back, :, cols]
        sums.append(acc)
        invs.append(1.0 / wnd)
    _pool_mix_gate(sums, p, invs, oa_ref[...], g_ref[...], wp_ref, scale_ref, a_ref)


def _ab_mix_sample(p, prev_t, o_a, g, w_pool, scale):
    m = p.shape[0]
    return pl.pallas_call(
        _ab_mix_sample_kernel,
        out_shape=jax.ShapeDtypeStruct((m, W_AB), BF16),
        name="ab_mix_sample",
        compiler_params=pltpu.CompilerParams(vmem_limit_bytes=VMEM_LIMIT),
    )(p, prev_t, o_a, g, w_pool, scale.reshape(1, W_B))


def _out_proj_kernel(a_ref, w_ref, x_ref, y_ref):
    y_ref[...] = x_ref[...] + jnp.dot(a_ref[...], w_ref[...], preferred_element_type=F32)


def _out_proj(a, w, x, *, tm, tn):
    m, kdim = a.shape
    n = w.shape[1]
    return pl.pallas_call(
        _out_proj_kernel,
        out_shape=jax.ShapeDtypeStruct((m, n), F32),
        grid=(m // tm, n // tn),
        in_specs=[
            pl.BlockSpec((tm, kdim), lambda i, j: (i, 0)),
            pl.BlockSpec((kdim, tn), lambda i, j: (0, j)),
            pl.BlockSpec((tm, tn), lambda i, j: (i, j)),
        ],
        out_specs=pl.BlockSpec((tm, tn), lambda i, j: (i, j)),
        compiler_params=_params(("parallel", "parallel")),
        name="out_proj",
    )(a, w, x)


def _c_gate_prompt_kernel(u_ref, vg_ref, g_ref, vgain_ref, ws_ref, bst_ref, a_ref, vrow_ref, *,
                          chunks):
    gc = u_ref.shape[1] // H_C
    r = lax.broadcasted_iota(jnp.int32, (CHUNK_C, CHUNK_C), 0)
    c = lax.broadcasted_iota(jnp.int32, (CHUNK_C, CHUNK_C), 1)
    lower = r >= c
    for ch in range(chunks):
        rows = slice(ch * CHUNK_C, (ch + 1) * CHUNK_C)
        v = _rms_rows(vg_ref[rows, :], vgain_ref[...])
        if ch == chunks - 1:
            vrow_ref[...] = v
        for hd in range(H_C):
            cols = slice(hd * gc, (hd + 1) * gc)
            ws = jnp.where(lower, ws_ref[hd], 0.0).astype(BF16)
            mixed = jnp.dot(ws, v[:, cols].astype(BF16), preferred_element_type=F32)
            mixed = mixed + bst_ref[:, hd:hd + 1]
            a_ref[rows, cols] = (u_ref[rows, cols] * mixed * _silu(g_ref[rows, cols])).astype(BF16)


def _c_gate_prompt(z, v_gain, w_s, b_s, *, chunks):
    m = z.shape[0]
    wc = z.shape[1] // 3
    tm = chunks * CHUNK_C
    return pl.pallas_call(
        functools.partial(_c_gate_prompt_kernel, chunks=chunks),
        out_shape=(jax.ShapeDtypeStruct((m, wc), BF16),
                   jax.ShapeDtypeStruct((CHUNK_C, wc), F32)),
        grid=(m // tm,),
        in_specs=[
            pl.BlockSpec((tm, wc), lambda i: (i, 0)),
            pl.BlockSpec((tm, wc), lambda i: (i, 1)),
            pl.BlockSpec((tm, wc), lambda i: (i, 2)),
            pl.BlockSpec((1, wc), lambda i: (0, 0)),
            pl.BlockSpec((H_C, CHUNK_C, CHUNK_C), lambda i: (0, 0, 0)),
            pl.BlockSpec((CHUNK_C, H_C), lambda i: (0, 0)),
        ],
        out_specs=(pl.BlockSpec((tm, wc), lambda i: (i, 0)),
                   pl.BlockSpec((CHUNK_C, wc), lambda i: (0, 0))),
        compiler_params=_params(("arbitrary",)),
        name="c_gate_prompt",
    )(z, z, z, v_gain.reshape(1, wc), w_s, b_s.T)


def _c_gate_sample_kernel(u_ref, vg_ref, g_ref, vgain_ref, w00_ref, b0_ref, a_ref, vrow_ref):
    v = _rms_rows(vg_ref[...], vgain_ref[...])
    vrow_ref[...] = v
    mixed = w00_ref[...] * v + b0_ref[...]
    a_ref[...] = (u_ref[...] * mixed * _silu(g_ref[...])).astype(BF16)


def _c_gate_sample(u, vg, g, v_gain, w_s, b_s):
    m, wc = u.shape
    gc = wc // H_C
    w00 = jnp.repeat(w_s[:, 0, 0], gc).reshape(1, wc)
    b0 = jnp.repeat(b_s[:, 0], gc).reshape(1, wc)
    return pl.pallas_call(
        _c_gate_sample_kernel,
        out_shape=(jax.ShapeDtypeStruct((m, wc), BF16), jax.ShapeDtypeStruct((m, wc), F32)),
        name="c_gate_sample",
    )(u, vg, g, v_gain.reshape(1, wc), w00, b0)


def kernel(x_prompt, x_sample, cache_k, cache_v, state_pool, page_table, norm_gain, w_in_ab, q_gain,
           k_gain, sb_bias, w_pool, pool_scale, w_out_ab, w_in_c, v_gain, w_spatial, b_spatial,
           w_out_c):
    bp, s, d = x_prompt.shape
    bs, t, _ = x_sample.shape
    assert bp == 1 and t == 1
    xp = x_prompt.reshape(s, d)
    xs = x_sample.reshape(bs, d)

    w_in0 = w_in_ab[0].astype(BF16)
    w_pool0 = w_pool[0].astype(BF16)
    w_out0 = w_out_ab[0].astype(BF16)
    w_in1 = w_in_c[0].astype(BF16)
    w_out1 = w_out_c[0].astype(BF16)
    colgain = jnp.concatenate([jnp.tile(q_gain[0] * (SM_SCALE * LOG2E), N_HEADS),
                               jnp.tile(k_gain[0], N_HEADS),
                               jnp.ones((W_A + W_B + W_AB,), F32)])
    ab_in = functools.partial(_in_proj, gain=norm_gain[0], w=w_in0, tn=AB_IN_PROJ_TN, colgain=colgain,
                              norm_cols=2 * W_A, bf16_cols=3 * W_A, name="ab_in_proj")
    c_in = functools.partial(_in_proj, gain=norm_gain[1], w=w_in1, tn=C_IN_PROJ_TN,
                             gelu_cols=2 * d, name="c_in_proj")

    z0, qkv = ab_in(xp, tm=IN_PROJ_TM)
    z0s, qkv_s = ab_in(xs, tm=bs)
    o_a, oa_s = _sb_attention(qkv, qkv_s[:, :W_A], cache_k[0], cache_v[0], page_table, sb_bias[0],
                              th=ATTN_HALF_ROWS, tk=ATTN_SUFFIX_BLOCK, group=DECODE_PAGES_PER_STEP)

    a = _ab_mix_prompt(z0, o_a, w_pool0, pool_scale[0], tm=AB_MIX_TM)
    xp1 = _out_proj(a, w_out0, xp, tm=OUT_PROJ_TM, tn=OUT_PROJ_TN)
    new_k_p = z0[:, W_A:2 * W_A].reshape(1, bp, s, N_HEADS, HEAD_DIM)
    new_v_p = z0[:, 2 * W_A:3 * W_A].reshape(1, bp, s, N_HEADS, HEAD_DIM)
    new_pool_p = z0[s - POOL_BUF:, 3 * W_A:3 * W_A + W_B].reshape(1, bp, POOL_BUF, W_B)

    ps = z0s[:, 3 * W_A:3 * W_A + W_B]
    prev_t = jnp.swapaxes(state_pool[0], 0, 1)
    a_s = _ab_mix_sample(ps, prev_t, oa_s, z0s[:, 3 * W_A + W_B:], w_pool0, pool_scale[0])
    xs1 = _out_proj(a_s, w_out0, xs, tm=bs, tn=OUT_PROJ_TN)
    new_k_s = z0s[:, W_A:2 * W_A].reshape(1, bs, t, N_HEADS, HEAD_DIM)
    new_v_s = z0s[:, 2 * W_A:3 * W_A].reshape(1, bs, t, N_HEADS, HEAD_DIM)
    new_pool_s = jnp.concatenate([state_pool[0][:, 1:], ps[:, None, :]], axis=1)[None]

    (z1,) = c_in(xp1, tm=IN_PROJ_TM)
    a1, v_last = _c_gate_prompt(z1, v_gain[0], w_spatial[0], b_spatial[0], chunks=C_GATE_CHUNKS)
    yp = _out_proj(a1, w_out1, xp1, tm=OUT_PROJ_TM, tn=OUT_PROJ_TN)

    (z1s,) = c_in(xs1, tm=bs)
    a1s, v_s = _c_gate_sample(z1s[:, :d], z1s[:, d:2 * d], z1s[:, 2 * d:], v_gain[0], w_spatial[0],
                              b_spatial[0])
    ys = _out_proj(a1s, w_out1, xs1, tm=bs, tn=OUT_PROJ_TN)

    return (yp.reshape(bp, s, d), ys.reshape(bs, t, d), new_k_p, new_v_p, new_k_s, new_v_s,
            new_pool_p, new_pool_s, v_last.reshape(1, bp, CHUNK_C, d), v_s.reshape(1, bs, t, d))
```

```python
import functools
import math

import jax
import jax.numpy as jnp
from jax import lax
from jax.experimental import pallas as pl
from jax.experimental.pallas import tpu as pltpu

F32 = jnp.float32
BF16 = jnp.bfloat16

LANES = 128
HEAD_DIM = 128
N_HEADS = 8
W_A = N_HEADS * HEAD_DIM
POOL_WINDOWS = (2, 4, 8, 16)
G_B = 256
W_B = G_B * len(POOL_WINDOWS)
W_AB = W_A + W_B
POOL_BUF = max(POOL_WINDOWS) - 1
HALO = 16
CHUNK_C = 128
H_C = 8
EPS = 1e-6
SM_SCALE = 1.0 / math.sqrt(HEAD_DIM)
LOG2E = math.log2(math.e)
GELU_C = math.sqrt(2.0 / math.pi)
VMEM_LIMIT = 56 * 1024 * 1024

IN_PROJ_TM = 1024
AB_IN_PROJ_TN, C_IN_PROJ_TN = 1024, 512
IN_PROJ_ROW_CHUNK = 256
OUT_PROJ_TM, OUT_PROJ_TN = 1024, 1024
ATTN_HALF_ROWS, ATTN_SUFFIX_BLOCK = 512, 256
DECODE_PAGES_PER_STEP = 8
AB_MIX_TM = 512
C_GATE_CHUNKS = 4


def _params(semantics):
    return pltpu.CompilerParams(dimension_semantics=semantics, vmem_limit_bytes=VMEM_LIMIT)


def _softplus_2(z2):
    neg_abs = lax.bitcast_convert_type(
        lax.bitcast_convert_type(z2, jnp.uint32) | jnp.uint32(0x80000000), F32)
    return jnp.maximum(z2, 0.0) + jnp.log(1.0 + jnp.exp2(neg_abs)) * LOG2E


def _silu(g):
    return g * jax.nn.sigmoid(g)


def _gelu_tanh(x):
    return 0.5 * x * (1.0 + jnp.tanh(GELU_C * (x + 0.044715 * (x * x * x))))


def _rms_rows(xf, gain):
    ms = jnp.mean(xf * xf, axis=-1, keepdims=True)
    return xf * lax.rsqrt(ms + EPS) * gain


def _maskable(dtype):
    return jnp.dtype(dtype).itemsize == 4


def _in_proj_kernel(x_ref, gain_ref, w_ref, *refs, norm_blocks, gelu_blocks, groups, cast_w):
    if norm_blocks:
        colgain_ref, refs = refs[0], refs[1:]
    out_refs, refs = refs[:len(groups)], refs[len(groups):]
    if cast_w:
        wb_ref, refs = refs[0], refs[1:]
    (h_ref,) = refs
    j = pl.program_id(1)

    @pl.when(j == 0)
    def _():
        h_ref[...] = _rms_rows(x_ref[...], gain_ref[...]).astype(BF16)

    if cast_w:
        w = w_ref[...].astype(BF16)
        wb_ref[...] = w
    else:
        w = w_ref[...]
    tm = h_ref.shape[0]
    step = min(tm, IN_PROJ_ROW_CHUNK)
    n_blocks = pl.num_programs(1)
    for r0 in range(0, tm, step):
        rows = slice(r0, r0 + step)
        z = jnp.dot(h_ref[rows, :], w, preferred_element_type=F32)
        if norm_blocks:
            is_norm = j < norm_blocks
            cols = []
            for c in range(z.shape[1] // HEAD_DIM):
                zc = z[:, c * HEAD_DIM:(c + 1) * HEAD_DIM]
                rs = lax.rsqrt(jnp.mean(zc * zc, axis=-1, keepdims=True) + EPS)
                cols.append(zc * jnp.where(is_norm, rs, 1.0))
            z = jnp.concatenate(cols, axis=1) * colgain_ref[...]
        if gelu_blocks:
            z = jnp.where(j < gelu_blocks, _gelu_tanh(z), z)
        for out_ref, (first, count, dtype) in zip(out_refs, groups):
            val = z.astype(dtype)
            if count == n_blocks or not _maskable(dtype):
                out_ref[rows, :] = val
            else:
                inside = (j >= first) & (j < first + count)
                pltpu.store(out_ref.at[rows, :], val, mask=jnp.broadcast_to(inside, val.shape))


def _in_proj(x, gain, w, *, tm, tn, groups, colgain=None, norm_cols=0, gelu_cols=0, cast_w=False,
             name):
    m, d = x.shape
    n = w.shape[1]
    assert not cast_w or m == tm
    in_specs = [
        pl.BlockSpec((tm, d), lambda i, j: (i, 0)),
        pl.BlockSpec((1, d), lambda i, j: (0, 0)),
        pl.BlockSpec((d, tn), lambda i, j: (0, j)),
    ]
    args = [x, gain.reshape(1, d), w]
    if norm_cols:
        in_specs.append(pl.BlockSpec((1, tn), lambda i, j: (0, j)))
        args.append(colgain.reshape(1, n))
    block_groups = tuple((first // tn, cols // tn, dtype) for first, cols, dtype in groups)
    out_shape, out_specs = [], []
    for first, count, dtype in block_groups:
        if _maskable(dtype) or count == n // tn:
            cols, last = count * tn, count - 1
        else:
            assert first == 0
            cols, last = (count + 1) * tn, count
        out_shape.append(jax.ShapeDtypeStruct((m, cols), dtype))
        out_specs.append(pl.BlockSpec(
            (tm, tn), lambda i, j, first=first, last=last: (i, jnp.clip(j - first, 0, last))))
    if cast_w:
        out_shape.append(jax.ShapeDtypeStruct((d, n), BF16))
        out_specs.append(pl.BlockSpec((d, tn), lambda i, j: (0, j)))
    return pl.pallas_call(
        functools.partial(_in_proj_kernel, norm_blocks=norm_cols // tn, gelu_blocks=gelu_cols // tn,
                          groups=block_groups, cast_w=cast_w),
        out_shape=tuple(out_shape),
        grid=(m // tm, n // tn),
        in_specs=in_specs,
        out_specs=tuple(out_specs),
        scratch_shapes=[pltpu.VMEM((tm, d), BF16)],
        compiler_params=_params(("parallel", "arbitrary")),
        name=name,
    )(*args)


def _strict_upper(n):
    r = lax.broadcasted_iota(jnp.int32, (n, n), 0)
    c = lax.broadcasted_iota(jnp.int32, (n, n), 1)
    return jnp.where(r > c, 1.0, 0.0).astype(BF16)


def _sb_attention_kernel(pt_ref, bias_ref, q_ref, k_ref, v_ref, qs_ref, bias_col_ref, ck_ref, cv_ref,
                         o_ref, os_ref, acc_ref, r_ref, kbuf_ref, vbuf_ref, ksem_ref, vsem_ref,
                         accs_ref, rs_ref, dls_ref, dsum_ref, *, th, tk, group):
    h = pl.program_id(0)
    i = pl.program_id(1)
    n_seq, n_pages = pt_ref.shape
    groups_per_seq = n_pages // group
    n_groups = n_seq * groups_per_seq
    groups_per_head = n_groups // pl.num_programs(0)
    first_step = (h == 0) & (i == 0)
    last_step = (h == pl.num_programs(0) - 1) & (i == pl.num_programs(1) - 1)

    def page_copies(cache_ref, buf_ref, sems, n, slot):
        seq = n // groups_per_seq
        pg = n % groups_per_seq
        return [pltpu.make_async_copy(
            cache_ref.at[pt_ref[seq, n_pages - 1 - (pg * group + g)]], buf_ref.at[slot, g],
            sems.at[slot]) for g in range(group)]

    def k_copies(n, slot):
        return page_copies(ck_ref, kbuf_ref, ksem_ref, n, slot)

    def v_copies(n, slot):
        return page_copies(cv_ref, vbuf_ref, vsem_ref, n, slot)

    def decode_second_half(m, v_slot):
        seq = m // groups_per_seq
        fresh = (m % groups_per_seq) == 0
        acc0 = jnp.where(fresh, 0.0, accs_ref[...])
        run0 = jnp.where(fresh, 0.0, rs_ref[...])
        acc, run = _decode_pv(dls_ref[...], dsum_ref[...],
                              [vbuf_ref[v_slot, g] for g in range(group)], acc0, run0)
        accs_ref[...] = acc
        rs_ref[...] = run
        os_ref[seq] = acc

    def decode_fetch(n):
        k_slot = n % 2
        for c in k_copies(n, k_slot) + v_copies(jnp.maximum(n - 1, 0), (n + 2) % 3):
            c.wait()
        nxt = jnp.minimum(n + 1, n_groups - 1)
        for c in k_copies(nxt, 1 - k_slot) + v_copies(nxt, (n + 1) % 3):
            c.start()

    def decode_first_half(n):
        ls2, stacked = _decode_qk(qs_ref[n // groups_per_seq], bias_col_ref[...] * LOG2E,
                                  [kbuf_ref[n % 2, g] for g in range(group)])
        dls_ref[...] = ls2
        dsum_ref[...] = _decode_suffix(stacked)

    def decode_previous(n):
        decode_second_half(jnp.maximum(n - 1, 0), (n + 2) % 3)

    @pl.when(first_step)
    def _():
        for ref in (accs_ref, rs_ref, dls_ref, dsum_ref, os_ref):
            ref[...] = jnp.zeros_like(ref)
        for c in k_copies(0, 0) + v_copies(0, 0) + v_copies(0, 2):
            c.start()

    bias2 = bias_ref[h] * LOG2E
    upper = _strict_upper(tk)
    nsub = th // tk
    half_rows = [slice(half * th, (half + 1) * th) for half in range(2)]
    even = 1 - i % 2
    group_base = h * groups_per_head + (i * (i - 1)) // 2 + (i + 1) // 2

    def qk_stage(half, kt, masked):
        ks = pl.multiple_of(kt * th, th)
        z2 = lax.dot_general(q_ref[half_rows[half], :], k_ref[pl.ds(ks, th), :],
                             (((1,), (1,)), ((), ())), preferred_element_type=F32) + bias2
        sp2 = _softplus_2(z2)
        ls2 = z2 - sp2
        valid = None
        if masked:
            t_pos = lax.broadcasted_iota(jnp.int32, (th, th), 0)
            s_pos = lax.broadcasted_iota(jnp.int32, (th, th), 1)
            valid = s_pos < t_pos
            sp2 = jnp.where(valid, sp2, 0.0)
        blocks = [sp2[:, c * tk:(c + 1) * tk] for c in range(nsub)]
        totals = [jnp.broadcast_to(jnp.sum(b, axis=-1, keepdims=True), (th, LANES)) for b in blocks]
        return ks, ls2, jnp.concatenate(blocks, axis=0).astype(BF16), totals, valid

    def suffix_stage(st):
        return jnp.dot(st[2], upper, preferred_element_type=F32)

    def pv_stage(st, later, run):
        ks, ls2, _, totals, valid = st
        ws = [None] * nsub
        for c in reversed(range(nsub)):
            offs = jnp.concatenate([run] * (tk // LANES), axis=1)
            ws[c] = jnp.exp2(ls2[:, c * tk:(c + 1) * tk] - later[c * th:(c + 1) * th] - offs)
            run = run + totals[c]
        w = jnp.concatenate(ws, axis=1)
        if valid is not None:
            w = jnp.where(valid, w, 0.0)
        return jnp.dot(w.astype(BF16), v_ref[pl.ds(ks, th), :], preferred_element_type=F32), run

    def diagonal(with_decode):
        if with_decode:
            decode_fetch(group_base)
        zero = jnp.zeros((th, LANES), F32)
        st = qk_stage(1, 2 * i + 1, True)
        pv_hi, run1 = pv_stage(st, suffix_stage(st), zero)
        if with_decode:
            decode_previous(group_base)
        st = qk_stage(0, 2 * i, True)
        pv0, run0 = pv_stage(st, suffix_stage(st), zero)
        if with_decode:
            decode_first_half(group_base)
        st = qk_stage(1, 2 * i, False)
        pv_lo, run1 = pv_stage(st, suffix_stage(st), run1)
        acc_ref[half_rows[0], :] = pv0
        acc_ref[half_rows[1], :] = pv_hi + pv_lo
        r_ref[half_rows[0], :] = run0
        r_ref[half_rows[1], :] = run1

    pl.when(even == 1)(functools.partial(diagonal, True))
    pl.when(even == 0)(functools.partial(diagonal, False))

    def body(it, carry):
        kt = 2 * i - 1 - 2 * it
        n = group_base + even + it
        decode_fetch(n)
        pvs = [None] * 4
        runs = [r_ref[rows, :] for rows in half_rows]
        for t, (half, ktile) in enumerate([(0, kt), (1, kt), (0, kt - 1), (1, kt - 1)]):
            st = qk_stage(half, ktile, False)
            pvs[t], runs[half] = pv_stage(st, suffix_stage(st), runs[half])
            if t == 0:
                decode_previous(n)
            if t == 2:
                decode_first_half(n)
        for half, rows in enumerate(half_rows):
            acc_ref[rows, :] += pvs[half] + pvs[half + 2]
            r_ref[rows, :] = runs[half]
        return carry

    lax.fori_loop(0, i, body, 0)
    o_ref[...] = acc_ref[...]

    @pl.when(last_step)
    def _():
        last = n_groups - 1
        for c in v_copies(last, last % 3):
            c.wait()
        decode_second_half(last, last % 3)
        for c in k_copies(last, n_groups % 2) + v_copies(last, n_groups % 3):
            c.wait()


def _sb_attention(qkv, q_s, cache_k, cache_v, page_table, bias, *, th, tk, group):
    s = qkv.shape[0]
    b = q_s.shape[0]
    page = cache_k.shape[1]
    n_pages = page_table.shape[1]
    tq = 2 * th
    steps = s // tq
    keys = group * page
    assert b * (n_pages // group) == N_HEADS * (steps * (steps - 1) // 2 + (steps + 1) // 2)
    grid_spec = pltpu.PrefetchScalarGridSpec(
        num_scalar_prefetch=1,
        grid=(N_HEADS, steps),
        in_specs=[
            pl.BlockSpec(memory_space=pltpu.SMEM),
            pl.BlockSpec((tq, HEAD_DIM), lambda h, i, pt: (i, h)),
            pl.BlockSpec((s, HEAD_DIM), lambda h, i, pt: (0, N_HEADS + h)),
            pl.BlockSpec((s, HEAD_DIM), lambda h, i, pt: (0, 2 * N_HEADS + h)),
            pl.BlockSpec((b, N_HEADS, HEAD_DIM), lambda h, i, pt: (0, 0, 0)),
            pl.BlockSpec((N_HEADS, 1), lambda h, i, pt: (0, 0)),
            pl.BlockSpec(memory_space=pl.ANY),
            pl.BlockSpec(memory_space=pl.ANY),
        ],
        out_specs=(pl.BlockSpec((tq, HEAD_DIM), lambda h, i, pt: (i, h)),
                   pl.BlockSpec((b, N_HEADS, HEAD_DIM), lambda h, i, pt: (0, 0, 0))),
        scratch_shapes=[
            pltpu.VMEM((tq, HEAD_DIM), F32),
            pltpu.VMEM((tq, LANES), F32),
            pltpu.VMEM((2, group, page, N_HEADS, HEAD_DIM), F32),
            pltpu.VMEM((3, group, page, N_HEADS, HEAD_DIM), F32),
            pltpu.SemaphoreType.DMA((2,)),
            pltpu.SemaphoreType.DMA((3,)),
            pltpu.VMEM((N_HEADS, HEAD_DIM), F32),
            pltpu.VMEM((N_HEADS, LANES), F32),
            pltpu.VMEM((N_HEADS, keys * N_HEADS), F32),
            pltpu.VMEM((keys * N_HEADS // LANES * N_HEADS, 2 * LANES), F32),
        ],
    )
    o_a, o_s = pl.pallas_call(
        functools.partial(_sb_attention_kernel, th=th, tk=tk, group=group),
        out_shape=(jax.ShapeDtypeStruct((s, W_A), F32),
                   jax.ShapeDtypeStruct((b, N_HEADS, HEAD_DIM), F32)),
        grid_spec=grid_spec,
        compiler_params=_params(("arbitrary", "arbitrary")),
        name="sb_attention",
    )(page_table, bias, qkv, qkv, qkv, q_s.reshape(b, N_HEADS, HEAD_DIM).astype(F32),
      bias.reshape(N_HEADS, 1), cache_k, cache_v)
    return o_a, o_s.reshape(b, W_A)


def _suffix_and_total(n):
    r = lax.broadcasted_iota(jnp.int32, (n, 2 * n), 0)
    c = lax.broadcasted_iota(jnp.int32, (n, 2 * n), 1)
    return jnp.where((r > c) | (c >= n), 1.0, 0.0).astype(BF16)


def _own_lanes():
    lane = lax.broadcasted_iota(jnp.int32, (N_HEADS, LANES), 1)
    row = lax.broadcasted_iota(jnp.int32, (N_HEADS, LANES), 0)
    return (lane % N_HEADS) == row


def _newest_first(n_tiles, tiles_per_page):
    return [g * tiles_per_page + j for g in range(n_tiles // tiles_per_page)
            for j in reversed(range(tiles_per_page))]


def _decode_qk(q, bias2, k_pages):
    page = k_pages[0].shape[0]
    flat = page * N_HEADS
    keys_per_tile = LANES // N_HEADS
    lane = lax.broadcasted_iota(jnp.int32, (N_HEADS, LANES), 1)
    row = lax.broadcasted_iota(jnp.int32, (N_HEADS, LANES), 0)
    at_key = [lane == c * N_HEADS + row for c in range(keys_per_tile)]
    tiles = []
    for kp in k_pages:
        cols = jnp.sum(kp * q[None], axis=-1, keepdims=True)
        for j in range(page // keys_per_tile):
            zt = jnp.zeros((N_HEADS, LANES), F32)
            for c in range(keys_per_tile):
                col = jnp.broadcast_to(cols[j * keys_per_tile + c], (N_HEADS, LANES))
                zt = jnp.where(at_key[c], col, zt)
            tiles.append(zt)
    z2 = jnp.concatenate(tiles, axis=1) + bias2
    sp2 = _softplus_2(z2)
    ls2 = z2 - sp2
    own_t = _own_lanes()
    order = _newest_first(z2.shape[1] // LANES, flat // LANES)
    stacked = jnp.concatenate(
        [jnp.where(own_t, sp2[:, t * LANES:(t + 1) * LANES], 0.0) for t in order], axis=0)
    hi = stacked.astype(BF16)
    lo = (stacked - hi.astype(F32)).astype(BF16)
    return ls2, jnp.concatenate([hi, lo], axis=0)


def _decode_suffix(stacked):
    s = jnp.dot(stacked, _suffix_and_total(LANES), preferred_element_type=F32)
    half = s.shape[0] // 2
    return s[:half] + s[half:]


def _decode_pv(ls2, sums, v_pages, acc, run):
    flat = v_pages[0].shape[0] * N_HEADS
    own_t = _own_lanes()
    n_tiles = ls2.shape[1] // LANES
    ws = [None] * n_tiles
    for n, t in enumerate(_newest_first(n_tiles, flat // LANES)):
        rows = slice(n * N_HEADS, (n + 1) * N_HEADS)
        later = sums[rows, :LANES] + run
        ws[t] = jnp.where(own_t, jnp.exp2(ls2[:, t * LANES:(t + 1) * LANES] - later), 0.0)
        run = run + sums[rows, LANES:]
    w = jnp.concatenate(ws, axis=1).astype(BF16)
    vm = jnp.concatenate([vp.reshape(flat, HEAD_DIM).astype(BF16) for vp in v_pages], axis=0)
    return acc + jnp.dot(w, vm, preferred_element_type=F32), run


def _pool_mix_gate(window_sums, p, count_inv, o_a, g, wp_ref, scale_ref, a_ref):
    a_ref[:, :W_A] = (o_a * _silu(g[:, :W_A])).astype(BF16)
    for gi in range(len(POOL_WINDOWS)):
        cols = slice(gi * G_B, (gi + 1) * G_B)
        pooled = window_sums[gi] * count_inv[gi] - p[:, cols]
        mixed = jnp.dot(pooled.astype(BF16), wp_ref[gi], preferred_element_type=F32)
        o_b = mixed * scale_ref[:, cols]
        a_ref[:, W_A + gi * G_B:W_A + (gi + 1) * G_B] = (
            o_b * _silu(g[:, W_A + gi * G_B:W_A + (gi + 1) * G_B])).astype(BF16)


def _ab_mix_prompt_kernel(p_ref, halo_ref, oa_ref, g_ref, wp_ref, scale_ref, a_ref, buf_ref, *, tm):
    i = pl.program_id(0)
    buf_ref[:HALO, :] = jnp.where(i > 0, halo_ref[...], 0.0)
    buf_ref[HALO:, :] = p_ref[...]
    pos = i * tm + lax.broadcasted_iota(jnp.int32, (tm, 1), 0)
    sums, invs = [], []
    for gi, wnd in enumerate(POOL_WINDOWS):
        cols = slice(gi * G_B, (gi + 1) * G_B)
        acc = buf_ref[HALO:, cols]
        for back in range(1, wnd):
            acc = acc + buf_ref[HALO - back:HALO - back + tm, cols]
        sums.append(acc)
        invs.append(1.0 / jnp.minimum(wnd, pos + 1).astype(F32))
    _pool_mix_gate(sums, p_ref[...], invs, oa_ref[...], g_ref[...], wp_ref, scale_ref, a_ref)


def _ab_mix_prompt(z, o_a, w_pool, scale, *, tm):
    m = z.shape[0]
    halo_blocks = tm // HALO
    p_col = 3 * W_A // W_B
    g_col = (3 * W_A + W_B) // W_AB
    return pl.pallas_call(
        functools.partial(_ab_mix_prompt_kernel, tm=tm),
        out_shape=jax.ShapeDtypeStruct((m, W_AB), BF16),
        grid=(m // tm,),
        in_specs=[
            pl.BlockSpec((tm, W_B), lambda i: (i, p_col)),
            pl.BlockSpec((HALO, W_B), lambda i: (jnp.maximum(i * halo_blocks - 1, 0), p_col)),
            pl.BlockSpec((tm, W_A), lambda i: (i, 0)),
            pl.BlockSpec((tm, W_AB), lambda i: (i, g_col)),
            pl.BlockSpec((len(POOL_WINDOWS), G_B, G_B), lambda i: (0, 0, 0)),
            pl.BlockSpec((1, W_B), lambda i: (0, 0)),
        ],
        out_specs=pl.BlockSpec((tm, W_AB), lambda i: (i, 0)),
        scratch_shapes=[pltpu.VMEM((tm + HALO, W_B), F32)],
        compiler_params=_params(("parallel",)),
        name="ab_mix_prompt",
    )(z, z, o_a, z, w_pool, scale.reshape(1, W_B))


def _ab_mix_sample_kernel(p_ref, prev_ref, oa_ref, g_ref, wp_ref, scale_ref, a_ref):
    p = p_ref[...]
    sums, invs = [], []
    for gi, wnd in enumerate(POOL_WINDOWS):
        cols = slice(gi * G_B, (gi + 1) * G_B)
        acc = p[:, cols]
        for back in range(1, wnd):
            acc = acc + prev_ref[POOL_BUF - back, :, cols]
        sums.append(acc)
        invs.append(1.0 / wnd)
    _pool_mix_gate(sums, p, invs, oa_ref[...], g_ref[...], wp_ref, scale_ref, a_ref)


def _ab_mix_sample(p, prev_t, o_a, g, w_pool, scale):
    m = p.shape[0]
    return pl.pallas_call(
        _ab_mix_sample_kernel,
        out_shape=jax.ShapeDtypeStruct((m, W_AB), BF16),
        name="ab_mix_sample",
        compiler_params=pltpu.CompilerParams(vmem_limit_bytes=VMEM_LIMIT),
    )(p, prev_t, o_a, g, w_pool, scale.reshape(1, W_B))


def _out_proj_kernel(a_ref, w_ref, x_ref, y_ref, *wb_ref):
    w = w_ref[...]
    if wb_ref:
        w = w.astype(BF16)
        wb_ref[0][...] = w
    y_ref[...] = x_ref[...] + jnp.dot(a_ref[...], w, preferred_element_type=F32)


def _out_proj(a, w, x, *, tm, tn, cast_w=False):
    m, kdim = a.shape
    n = w.shape[1]
    assert not cast_w or m == tm
    out_shape = [jax.ShapeDtypeStruct((m, n), F32)]
    out_specs = [pl.BlockSpec((tm, tn), lambda i, j: (i, j))]
    if cast_w:
        out_shape.append(jax.ShapeDtypeStruct((kdim, n), BF16))
        out_specs.append(pl.BlockSpec((kdim, tn), lambda i, j: (0, j)))
    return pl.pallas_call(
        _out_proj_kernel,
        out_shape=tuple(out_shape),
        grid=(m // tm, n // tn),
        in_specs=[
            pl.BlockSpec((tm, kdim), lambda i, j: (i, 0)),
            pl.BlockSpec((kdim, tn), lambda i, j: (0, j)),
            pl.BlockSpec((tm, tn), lambda i, j: (i, j)),
        ],
        out_specs=tuple(out_specs),
        compiler_params=_params(("parallel", "parallel")),
        name="out_proj",
    )(a, w, x)


def _c_gate_prompt_kernel(u_ref, vg_ref, g_ref, vgain_ref, ws_ref, bst_ref, a_ref, vrow_ref, *,
                          chunks):
    gc = u_ref.shape[1] // H_C
    r = lax.broadcasted_iota(jnp.int32, (CHUNK_C, CHUNK_C), 0)
    c = lax.broadcasted_iota(jnp.int32, (CHUNK_C, CHUNK_C), 1)
    lower = r >= c
    for ch in range(chunks):
        rows = slice(ch * CHUNK_C, (ch + 1) * CHUNK_C)
        v = _rms_rows(vg_ref[rows, :], vgain_ref[...])
        if ch == chunks - 1:
            vrow_ref[...] = v
        for hd in range(H_C):
            cols = slice(hd * gc, (hd + 1) * gc)
            ws = jnp.where(lower, ws_ref[hd], 0.0).astype(BF16)
            mixed = jnp.dot(ws, v[:, cols].astype(BF16), preferred_element_type=F32)
            mixed = mixed + bst_ref[:, hd:hd + 1]
            a_ref[rows, cols] = (u_ref[rows, cols] * mixed * _silu(g_ref[rows, cols])).astype(BF16)


def _c_gate_prompt(z, v_gain, w_s, b_s, *, chunks):
    m = z.shape[0]
    wc = z.shape[1] // 3
    tm = chunks * CHUNK_C
    return pl.pallas_call(
        functools.partial(_c_gate_prompt_kernel, chunks=chunks),
        out_shape=(jax.ShapeDtypeStruct((m, wc), BF16),
                   jax.ShapeDtypeStruct((CHUNK_C, wc), F32)),
        grid=(m // tm,),
        in_specs=[
            pl.BlockSpec((tm, wc), lambda i: (i, 0)),
            pl.BlockSpec((tm, wc), lambda i: (i, 1)),
            pl.BlockSpec((tm, wc), lambda i: (i, 2)),
            pl.BlockSpec((1, wc), lambda i: (0, 0)),
            pl.BlockSpec((H_C, CHUNK_C, CHUNK_C), lambda i: (0, 0, 0)),
            pl.BlockSpec((CHUNK_C, H_C), lambda i: (0, 0)),
        ],
        out_specs=(pl.BlockSpec((tm, wc), lambda i: (i, 0)),
                   pl.BlockSpec((CHUNK_C, wc), lambda i: (0, 0))),
        compiler_params=_params(("arbitrary",)),
        name="c_gate_prompt",
    )(z, z, z, v_gain.reshape(1, wc), w_s, b_s.T)


def _c_gate_sample_kernel(u_ref, vg_ref, g_ref, vgain_ref, w00_ref, b0_ref, a_ref, vrow_ref):
    v = _rms_rows(vg_ref[...], vgain_ref[...])
    vrow_ref[...] = v
    mixed = w00_ref[...] * v + b0_ref[...]
    a_ref[...] = (u_ref[...] * mixed * _silu(g_ref[...])).astype(BF16)


def _c_gate_sample(u, vg, g, v_gain, w_s, b_s):
    m, wc = u.shape
    gc = wc // H_C
    w00 = jnp.repeat(w_s[:, 0, 0], gc).reshape(1, wc)
    b0 = jnp.repeat(b_s[:, 0], gc).reshape(1, wc)
    return pl.pallas_call(
        _c_gate_sample_kernel,
        out_shape=(jax.ShapeDtypeStruct((m, wc), BF16), jax.ShapeDtypeStruct((m, wc), F32)),
        name="c_gate_sample",
    )(u, vg, g, v_gain.reshape(1, wc), w00, b0)


def kernel(x_prompt, x_sample, cache_k, cache_v, state_pool, page_table, norm_gain, w_in_ab, q_gain,
           k_gain, sb_bias, w_pool, pool_scale, w_out_ab, w_in_c, v_gain, w_spatial, b_spatial,
           w_out_c):
    bp, s, d = x_prompt.shape
    bs, t, _ = x_sample.shape
    assert bp == 1 and t == 1
    xp = x_prompt.reshape(s, d)
    xs = x_sample.reshape(bs, d)

    w_pool0 = w_pool[0].astype(BF16)
    colgain = jnp.concatenate([jnp.tile(q_gain[0] * (SM_SCALE * LOG2E), N_HEADS),
                               jnp.tile(k_gain[0], N_HEADS),
                               jnp.ones((W_A + W_B + W_AB,), F32)])
    n0 = 3 * W_A + W_B + W_AB
    ab_groups = ((0, n0, F32), (0, 3 * W_A, BF16))
    ab_in = functools.partial(_in_proj, gain=norm_gain[0], tn=AB_IN_PROJ_TN, groups=ab_groups,
                              colgain=colgain, norm_cols=2 * W_A, name="ab_in_proj")
    c_in = functools.partial(_in_proj, gain=norm_gain[1], tn=C_IN_PROJ_TN,
                             groups=((0, 3 * d, F32),), gelu_cols=2 * d, name="c_in_proj")

    z0s, qkv_s, w_in0 = ab_in(xs, w=w_in_ab[0], tm=bs, cast_w=True)
    z0, qkv = ab_in(xp, w=w_in0, tm=IN_PROJ_TM)
    o_a, oa_s = _sb_attention(qkv, qkv_s[:, :W_A], cache_k[0], cache_v[0], page_table, sb_bias[0],
                              th=ATTN_HALF_ROWS, tk=ATTN_SUFFIX_BLOCK, group=DECODE_PAGES_PER_STEP)

    ps = z0s[:, 3 * W_A:3 * W_A + W_B]
    prev_t = jnp.swapaxes(state_pool[0], 0, 1)
    a_s = _ab_mix_sample(ps, prev_t, oa_s, z0s[:, 3 * W_A + W_B:], w_pool0, pool_scale[0])
    xs1, w_out0 = _out_proj(a_s, w_out_ab[0], xs, tm=bs, tn=OUT_PROJ_TN, cast_w=True)
    a = _ab_mix_prompt(z0, o_a, w_pool0, pool_scale[0], tm=AB_MIX_TM)
    (xp1,) = _out_proj(a, w_out0, xp, tm=OUT_PROJ_TM, tn=OUT_PROJ_TN)
    new_k_p = z0[:, W_A:2 * W_A].reshape(1, bp, s, N_HEADS, HEAD_DIM)
    new_v_p = z0[:, 2 * W_A:3 * W_A].reshape(1, bp, s, N_HEADS, HEAD_DIM)
    new_pool_p = z0[s - POOL_BUF:, 3 * W_A:3 * W_A + W_B].reshape(1, bp, POOL_BUF, W_B)
    new_k_s = z0s[:, W_A:2 * W_A].reshape(1, bs, t, N_HEADS, HEAD_DIM)
    new_v_s = z0s[:, 2 * W_A:3 * W_A].reshape(1, bs, t, N_HEADS, HEAD_DIM)
    new_pool_s = jnp.concatenate([state_pool[0][:, 1:], ps[:, None, :]], axis=1)[None]

    z1s, w_in1 = c_in(xs1, w=w_in_c[0], tm=bs, cast_w=True)
    (z1,) = c_in(xp1, w=w_in1, tm=IN_PROJ_TM)
    a1s, v_s = _c_gate_sample(z1s[:, :d], z1s[:, d:2 * d], z1s[:, 2 * d:], v_gain[0], w_spatial[0],
                              b_spatial[0])
    ys, w_out1 = _out_proj(a1s, w_out_c[0], xs1, tm=bs, tn=OUT_PROJ_TN, cast_w=True)
    a1, v_last = _c_gate_prompt(z1, v_gain[0], w_spatial[0], b_spatial[0], chunks=C_GATE_CHUNKS)
    (yp,) = _out_proj(a1, w_out1, xp1, tm=OUT_PROJ_TM, tn=OUT_PROJ_TN)

    return (yp.reshape(bp, s, d), ys.reshape(bs, t, d), new_k_p, new_v_p, new_k_s, new_v_s,
            new_pool_p, new_pool_s, v_last.reshape(1, bp, CHUNK_C, d), v_s.reshape(1, bs, t, d))
```

```python
import functools
import math

import jax
import jax.numpy as jnp
from jax import lax
from jax.experimental import pallas as pl
from jax.experimental.pallas import tpu as pltpu

F32 = jnp.float32
BF16 = jnp.bfloat16

LANES = 128
HEAD_DIM = 128
N_HEADS = 8
W_A = N_HEADS * HEAD_DIM
POOL_WINDOWS = (2, 4, 8, 16)
G_B = 256
W_B = G_B * len(POOL_WINDOWS)
W_AB = W_A + W_B
POOL_BUF = max(POOL_WINDOWS) - 1
HALO = 16
CHUNK_C = 128
H_C = 8
EPS = 1e-6
SM_SCALE = 1.0 / math.sqrt(HEAD_DIM)
LOG2E = math.log2(math.e)
GELU_C = math.sqrt(2.0 / math.pi)
VMEM_LIMIT = 56 * 1024 * 1024

IN_PROJ_TM = 1024
AB_IN_PROJ_TN, C_IN_PROJ_TN = 1024, 512
IN_PROJ_ROW_CHUNK = 256
OUT_PROJ_TM, OUT_PROJ_TN = 512, 2048
ATTN_HALF_ROWS, ATTN_SUFFIX_BLOCK = 512, 256
DECODE_PAGES_PER_STEP = 8
AB_MIX_TM = 512
C_GATE_CHUNKS = 4


def _params(semantics):
    return pltpu.CompilerParams(dimension_semantics=semantics, vmem_limit_bytes=VMEM_LIMIT)


def _softplus_2(z2):
    neg_abs = lax.bitcast_convert_type(
        lax.bitcast_convert_type(z2, jnp.uint32) | jnp.uint32(0x80000000), F32)
    return jnp.maximum(z2, 0.0) + jnp.log(1.0 + jnp.exp2(neg_abs)) * LOG2E


def _silu(g):
    return g * jax.nn.sigmoid(g)


def _gelu_tanh(x):
    return 0.5 * x * (1.0 + jnp.tanh(GELU_C * (x + 0.044715 * (x * x * x))))


def _rms_rows(xf, gain):
    ms = jnp.mean(xf * xf, axis=-1, keepdims=True)
    return xf * lax.rsqrt(ms + EPS) * gain


def _maskable(dtype):
    return jnp.dtype(dtype).itemsize == 4


def _in_proj_kernel(x_ref, gain_ref, w_ref, *refs, norm_blocks, gelu_blocks, groups, cast_w):
    if norm_blocks:
        colgain_ref, refs = refs[0], refs[1:]
    out_refs, refs = refs[:len(groups)], refs[len(groups):]
    if cast_w:
        wb_ref, refs = refs[0], refs[1:]
    (h_ref,) = refs
    j = pl.program_id(1)

    @pl.when(j == 0)
    def _():
        h_ref[...] = _rms_rows(x_ref[...], gain_ref[...]).astype(BF16)

    if cast_w:
        w = w_ref[...].astype(BF16)
        wb_ref[...] = w
    else:
        w = w_ref[...]
    tm = h_ref.shape[0]
    step = min(tm, IN_PROJ_ROW_CHUNK)
    n_blocks = pl.num_programs(1)
    for r0 in range(0, tm, step):
        rows = slice(r0, r0 + step)
        z = jnp.dot(h_ref[rows, :], w, preferred_element_type=F32)
        if norm_blocks:
            is_norm = j < norm_blocks
            cols = []
            for c in range(z.shape[1] // HEAD_DIM):
                zc = z[:, c * HEAD_DIM:(c + 1) * HEAD_DIM]
                rs = lax.rsqrt(jnp.mean(zc * zc, axis=-1, keepdims=True) + EPS)
                cols.append(zc * jnp.where(is_norm, rs, 1.0))
            z = jnp.concatenate(cols, axis=1) * colgain_ref[...]
        if gelu_blocks:
            z = jnp.where(j < gelu_blocks, _gelu_tanh(z), z)
        for out_ref, (first, count, dtype) in zip(out_refs, groups):
            val = z.astype(dtype)
            if count == n_blocks or not _maskable(dtype):
                out_ref[rows, :] = val
            else:
                inside = (j >= first) & (j < first + count)
                pltpu.store(out_ref.at[rows, :], val, mask=jnp.broadcast_to(inside, val.shape))


def _in_proj(x, gain, w, *, tm, tn, groups, colgain=None, norm_cols=0, gelu_cols=0, cast_w=False,
             name):
    m, d = x.shape
    n = w.shape[1]
    assert not cast_w or m == tm
    in_specs = [
        pl.BlockSpec((tm, d), lambda i, j: (i, 0)),
        pl.BlockSpec((1, d), lambda i, j: (0, 0)),
        pl.BlockSpec((d, tn), lambda i, j: (0, j)),
    ]
    args = [x, gain.reshape(1, d), w]
    if norm_cols:
        in_specs.append(pl.BlockSpec((1, tn), lambda i, j: (0, j)))
        args.append(colgain.reshape(1, n))
    block_groups = tuple((first // tn, cols // tn, dtype) for first, cols, dtype in groups)
    out_shape, out_specs = [], []
    for first, count, dtype in block_groups:
        if _maskable(dtype) or count == n // tn:
            cols, last = count * tn, count - 1
        else:
            assert first == 0
            cols, last = (count + 1) * tn, count
        out_shape.append(jax.ShapeDtypeStruct((m, cols), dtype))
        out_specs.append(pl.BlockSpec(
            (tm, tn), lambda i, j, first=first, last=last: (i, jnp.clip(j - first, 0, last))))
    if cast_w:
        out_shape.append(jax.ShapeDtypeStruct((d, n), BF16))
        out_specs.append(pl.BlockSpec((d, tn), lambda i, j: (0, j)))
    return pl.pallas_call(
        functools.partial(_in_proj_kernel, norm_blocks=norm_cols // tn, gelu_blocks=gelu_cols // tn,
                          groups=block_groups, cast_w=cast_w),
        out_shape=tuple(out_shape),
        grid=(m // tm, n // tn),
        in_specs=in_specs,
        out_specs=tuple(out_specs),
        scratch_shapes=[pltpu.VMEM((tm, d), BF16)],
        compiler_params=_params(("parallel", "arbitrary")),
        name=name,
    )(*args)


def _from_key_on(n):
    r = lax.broadcasted_iota(jnp.int32, (n, n), 0)
    c = lax.broadcasted_iota(jnp.int32, (n, n), 1)
    return jnp.where(r >= c, 1.0, 0.0).astype(BF16)


def _sb_attention_kernel(pt_ref, bias_ref, q_ref, k_ref, v_ref, qs_ref, bias_col_ref, ck_ref, cv_ref,
                         o_ref, os_ref, acc_ref, r_ref, kbuf_ref, vbuf_ref, ksem_ref, vsem_ref,
                         accs_ref, rs_ref, dls_ref, dsum_ref, *, th, tk, group):
    h = pl.program_id(0)
    i = pl.program_id(1)
    n_seq, n_pages = pt_ref.shape
    groups_per_seq = n_pages // group
    n_groups = n_seq * groups_per_seq
    groups_per_head = n_groups // pl.num_programs(0)
    first_step = (h == 0) & (i == 0)
    last_step = (h == pl.num_programs(0) - 1) & (i == pl.num_programs(1) - 1)

    def page_copies(cache_ref, buf_ref, sems, n, slot):
        seq = n // groups_per_seq
        pg = n % groups_per_seq
        return [pltpu.make_async_copy(
            cache_ref.at[pt_ref[seq, n_pages - 1 - (pg * group + g)]], buf_ref.at[slot, g],
            sems.at[slot]) for g in range(group)]

    def k_copies(n, slot):
        return page_copies(ck_ref, kbuf_ref, ksem_ref, n, slot)

    def v_copies(n, slot):
        return page_copies(cv_ref, vbuf_ref, vsem_ref, n, slot)

    def decode_second_half(m, v_slot):
        seq = m // groups_per_seq
        fresh = (m % groups_per_seq) == 0
        acc0 = jnp.where(fresh, 0.0, accs_ref[...])
        run0 = jnp.where(fresh, 0.0, rs_ref[...])
        acc, run = _decode_pv(dls_ref[...], dsum_ref[...],
                              [vbuf_ref[v_slot, g] for g in range(group)], acc0, run0)
        accs_ref[...] = acc
        rs_ref[...] = run
        os_ref[seq] = acc

    def decode_fetch(n):
        k_slot = n % 2
        for c in k_copies(n, k_slot) + v_copies(jnp.maximum(n - 1, 0), (n + 2) % 3):
            c.wait()
        nxt = jnp.minimum(n + 1, n_groups - 1)
        for c in k_copies(nxt, 1 - k_slot) + v_copies(nxt, (n + 1) % 3):
            c.start()

    def decode_first_half(n):
        ls2, stacked = _decode_qk(qs_ref[n // groups_per_seq], bias_col_ref[...] * LOG2E,
                                  [kbuf_ref[n % 2, g] for g in range(group)])
        dls_ref[...] = ls2
        dsum_ref[...] = _decode_suffix(stacked)

    def decode_previous(n):
        decode_second_half(jnp.maximum(n - 1, 0), (n + 2) % 3)

    @pl.when(first_step)
    def _():
        for ref in (accs_ref, rs_ref, dls_ref, dsum_ref, os_ref):
            ref[...] = jnp.zeros_like(ref)
        for c in k_copies(0, 0) + v_copies(0, 0) + v_copies(0, 2):
            c.start()

    bias2 = bias_ref[h] * LOG2E
    from_key_on = _from_key_on(tk)
    nsub = th // tk
    half_rows = [slice(half * th, (half + 1) * th) for half in range(2)]
    even = 1 - i % 2
    group_base = h * groups_per_head + (i * (i - 1)) // 2 + (i + 1) // 2

    def qk_stage(half, kt, masked):
        ks = pl.multiple_of(kt * th, th)
        z2 = lax.dot_general(q_ref[half_rows[half], :], k_ref[pl.ds(ks, th), :],
                             (((1,), (1,)), ((), ())), preferred_element_type=F32) + bias2
        sp2 = _softplus_2(z2)
        valid = None
        if masked:
            t_pos = lax.broadcasted_iota(jnp.int32, (th, th), 0)
            s_pos = lax.broadcasted_iota(jnp.int32, (th, th), 1)
            valid = s_pos < t_pos
            sp2 = jnp.where(valid, sp2, 0.0)
        blocks = [sp2[:, c * tk:(c + 1) * tk] for c in range(nsub)]
        totals = [jnp.broadcast_to(jnp.sum(b, axis=-1, keepdims=True), (th, LANES)) for b in blocks]
        return ks, z2, jnp.concatenate(blocks, axis=0).astype(BF16), totals, valid

    def suffix_stage(st):
        return jnp.dot(st[2], from_key_on, preferred_element_type=F32)

    def pv_stage(st, from_key, run):
        ks, z2, _, totals, valid = st
        ws = [None] * nsub
        for c in reversed(range(nsub)):
            offs = jnp.concatenate([run] * (tk // LANES), axis=1)
            ws[c] = jnp.exp2(z2[:, c * tk:(c + 1) * tk] - from_key[c * th:(c + 1) * th] - offs)
            run = run + totals[c]
        w = jnp.concatenate(ws, axis=1)
        if valid is not None:
            w = jnp.where(valid, w, 0.0)
        return jnp.dot(w.astype(BF16), v_ref[pl.ds(ks, th), :], preferred_element_type=F32), run

    def diagonal(with_decode):
        if with_decode:
            decode_fetch(group_base)
        zero = jnp.zeros((th, LANES), F32)
        st = qk_stage(1, 2 * i + 1, True)
        pv_hi, run1 = pv_stage(st, suffix_stage(st), zero)
        if with_decode:
            decode_previous(group_base)
        st = qk_stage(0, 2 * i, True)
        pv0, run0 = pv_stage(st, suffix_stage(st), zero)
        if with_decode:
            decode_first_half(group_base)
        st = qk_stage(1, 2 * i, False)
        pv_lo, run1 = pv_stage(st, suffix_stage(st), run1)
        acc_ref[half_rows[0], :] = pv0
        acc_ref[half_rows[1], :] = pv_hi + pv_lo
        r_ref[half_rows[0], :] = run0
        r_ref[half_rows[1], :] = run1

    pl.when(even == 1)(functools.partial(diagonal, True))
    pl.when(even == 0)(functools.partial(diagonal, False))

    def body(it, carry):
        kt = 2 * i - 1 - 2 * it
        n = group_base + even + it
        decode_fetch(n)
        pvs = [None] * 4
        runs = [r_ref[rows, :] for rows in half_rows]
        for t, (half, ktile) in enumerate([(0, kt), (1, kt), (0, kt - 1), (1, kt - 1)]):
            st = qk_stage(half, ktile, False)
            pvs[t], runs[half] = pv_stage(st, suffix_stage(st), runs[half])
            if t == 0:
                decode_previous(n)
            if t == 2:
                decode_first_half(n)
        for half, rows in enumerate(half_rows):
            acc_ref[rows, :] += pvs[half] + pvs[half + 2]
            r_ref[rows, :] = runs[half]
        return carry

    lax.fori_loop(0, i, body, 0)
    o_ref[...] = acc_ref[...]

    @pl.when(last_step)
    def _():
        last = n_groups - 1
        for c in v_copies(last, last % 3):
            c.wait()
        decode_second_half(last, last % 3)
        for c in k_copies(last, n_groups % 2) + v_copies(last, n_groups % 3):
            c.wait()


def _sb_attention(qkv, q_s, cache_k, cache_v, page_table, bias, *, th, tk, group):
    s = qkv.shape[0]
    b = q_s.shape[0]
    page = cache_k.shape[1]
    n_pages = page_table.shape[1]
    tq = 2 * th
    steps = s // tq
    keys = group * page
    assert b * (n_pages // group) == N_HEADS * (steps * (steps - 1) // 2 + (steps + 1) // 2)
    grid_spec = pltpu.PrefetchScalarGridSpec(
        num_scalar_prefetch=1,
        grid=(N_HEADS, steps),
        in_specs=[
            pl.BlockSpec(memory_space=pltpu.SMEM),
            pl.BlockSpec((tq, HEAD_DIM), lambda h, i, pt: (i, h)),
            pl.BlockSpec((s, HEAD_DIM), lambda h, i, pt: (0, N_HEADS + h)),
            pl.BlockSpec((s, HEAD_DIM), lambda h, i, pt: (0, 2 * N_HEADS + h)),
            pl.BlockSpec((b, N_HEADS, HEAD_DIM), lambda h, i, pt: (0, 0, 0)),
            pl.BlockSpec((N_HEADS, 1), lambda h, i, pt: (0, 0)),
            pl.BlockSpec(memory_space=pl.ANY),
            pl.BlockSpec(memory_space=pl.ANY),
        ],
        out_specs=(pl.BlockSpec((tq, HEAD_DIM), lambda h, i, pt: (i, h)),
                   pl.BlockSpec((b, N_HEADS, HEAD_DIM), lambda h, i, pt: (0, 0, 0))),
        scratch_shapes=[
            pltpu.VMEM((tq, HEAD_DIM), F32),
            pltpu.VMEM((tq, LANES), F32),
            pltpu.VMEM((2, group, page, N_HEADS, HEAD_DIM), F32),
            pltpu.VMEM((3, group, page, N_HEADS, HEAD_DIM), F32),
            pltpu.SemaphoreType.DMA((2,)),
            pltpu.SemaphoreType.DMA((3,)),
            pltpu.VMEM((N_HEADS, HEAD_DIM), F32),
            pltpu.VMEM((N_HEADS, LANES), F32),
            pltpu.VMEM((N_HEADS, keys * N_HEADS), F32),
            pltpu.VMEM((keys * N_HEADS // LANES * N_HEADS, 2 * LANES), F32),
        ],
    )
    o_a, o_s = pl.pallas_call(
        functools.partial(_sb_attention_kernel, th=th, tk=tk, group=group),
        out_shape=(jax.ShapeDtypeStruct((s, W_A), F32),
                   jax.ShapeDtypeStruct((b, N_HEADS, HEAD_DIM), F32)),
        grid_spec=grid_spec,
        compiler_params=_params(("arbitrary", "arbitrary")),
        name="sb_attention",
    )(page_table, bias, qkv, qkv, qkv, q_s.reshape(b, N_HEADS, HEAD_DIM).astype(F32),
      bias.reshape(N_HEADS, 1), cache_k, cache_v)
    return o_a, o_s.reshape(b, W_A)


def _suffix_and_total(n):
    r = lax.broadcasted_iota(jnp.int32, (n, 2 * n), 0)
    c = lax.broadcasted_iota(jnp.int32, (n, 2 * n), 1)
    return jnp.where((r > c) | (c >= n), 1.0, 0.0).astype(BF16)


def _own_lanes():
    lane = lax.broadcasted_iota(jnp.int32, (N_HEADS, LANES), 1)
    row = lax.broadcasted_iota(jnp.int32, (N_HEADS, LANES), 0)
    return (lane % N_HEADS) == row


def _newest_first(n_tiles, tiles_per_page):
    return [g * tiles_per_page + j for g in range(n_tiles // tiles_per_page)
            for j in reversed(range(tiles_per_page))]


def _decode_qk(q, bias2, k_pages):
    page = k_pages[0].shape[0]
    flat = page * N_HEADS
    keys_per_tile = LANES // N_HEADS
    lane = lax.broadcasted_iota(jnp.int32, (N_HEADS, LANES), 1)
    row = lax.broadcasted_iota(jnp.int32, (N_HEADS, LANES), 0)
    at_key = [lane == c * N_HEADS + row for c in range(keys_per_tile)]
    tiles = []
    for kp in k_pages:
        cols = jnp.sum(kp * q[None], axis=-1, keepdims=True)
        for j in range(page // keys_per_tile):
            zt = jnp.zeros((N_HEADS, LANES), F32)
            for c in range(keys_per_tile):
                col = jnp.broadcast_to(cols[j * keys_per_tile + c], (N_HEADS, LANES))
                zt = jnp.where(at_key[c], col, zt)
            tiles.append(zt)
    z2 = jnp.concatenate(tiles, axis=1) + bias2
    sp2 = _softplus_2(z2)
    ls2 = z2 - sp2
    own_t = _own_lanes()
    order = _newest_first(z2.shape[1] // LANES, flat // LANES)
    stacked = jnp.concatenate(
        [jnp.where(own_t, sp2[:, t * LANES:(t + 1) * LANES], 0.0) for t in order], axis=0)
    hi = stacked.astype(BF16)
    lo = (stacked - hi.astype(F32)).astype(BF16)
    return ls2, jnp.concatenate([hi, lo], axis=0)


def _decode_suffix(stacked):
    s = jnp.dot(stacked, _suffix_and_total(LANES), preferred_element_type=F32)
    half = s.shape[0] // 2
    return s[:half] + s[half:]


def _decode_pv(ls2, sums, v_pages, acc, run):
    flat = v_pages[0].shape[0] * N_HEADS
    own_t = _own_lanes()
    n_tiles = ls2.shape[1] // LANES
    ws = [None] * n_tiles
    for n, t in enumerate(_newest_first(n_tiles, flat // LANES)):
        rows = slice(n * N_HEADS, (n + 1) * N_HEADS)
        later = sums[rows, :LANES] + run
        ws[t] = jnp.where(own_t, jnp.exp2(ls2[:, t * LANES:(t + 1) * LANES] - later), 0.0)
        run = run + sums[rows, LANES:]
    w = jnp.concatenate(ws, axis=1).astype(BF16)
    vm = jnp.concatenate([vp.reshape(flat, HEAD_DIM).astype(BF16) for vp in v_pages], axis=0)
    return acc + jnp.dot(w, vm, preferred_element_type=F32), run


def _pool_mix_gate(window_sums, p, count_inv, o_a, g, wp_ref, scale_ref, a_ref):
    a_ref[:, :W_A] = (o_a * _silu(g[:, :W_A])).astype(BF16)
    for gi in range(len(POOL_WINDOWS)):
        cols = slice(gi * G_B, (gi + 1) * G_B)
        pooled = window_sums[gi] * count_inv[gi] - p[:, cols]
        mixed = jnp.dot(pooled.astype(BF16), wp_ref[gi], preferred_element_type=F32)
        o_b = mixed * scale_ref[:, cols]
        a_ref[:, W_A + gi * G_B:W_A + (gi + 1) * G_B] = (
            o_b * _silu(g[:, W_A + gi * G_B:W_A + (gi + 1) * G_B])).astype(BF16)


def _ab_mix_prompt_kernel(p_ref, halo_ref, oa_ref, g_ref, wp_ref, scale_ref, a_ref, buf_ref, *, tm):
    i = pl.program_id(0)
    buf_ref[:HALO, :] = jnp.where(i > 0, halo_ref[...], 0.0)
    buf_ref[HALO:, :] = p_ref[...]
    pos = i * tm + lax.broadcasted_iota(jnp.int32, (tm, 1), 0)
    sums, invs = [], []
    for gi, wnd in enumerate(POOL_WINDOWS):
        cols = slice(gi * G_B, (gi + 1) * G_B)
        acc = buf_ref[HALO:, cols]
        for back in range(1, wnd):
            acc = acc + buf_ref[HALO - back:HALO - back + tm, cols]
        sums.append(acc)
        invs.append(1.0 / jnp.minimum(wnd, pos + 1).astype(F32))
    _pool_mix_gate(sums, p_ref[...], invs, oa_ref[...], g_ref[...], wp_ref, scale_ref, a_ref)


def _ab_mix_prompt(z, o_a, w_pool, scale, *, tm):
    m = z.shape[0]
    halo_blocks = tm // HALO
    p_col = 3 * W_A // W_B
    g_col = (3 * W_A + W_B) // W_AB
    return pl.pallas_call(
        functools.partial(_ab_mix_prompt_kernel, tm=tm),
        out_shape=jax.ShapeDtypeStruct((m, W_AB), BF16),
        grid=(m // tm,),
        in_specs=[
            pl.BlockSpec((tm, W_B), lambda i: (i, p_col)),
            pl.BlockSpec((HALO, W_B), lambda i: (jnp.maximum(i * halo_blocks - 1, 0), p_col)),
            pl.BlockSpec((tm, W_A), lambda i: (i, 0)),
            pl.BlockSpec((tm, W_AB), lambda i: (i, g_col)),
            pl.BlockSpec((len(POOL_WINDOWS), G_B, G_B), lambda i: (0, 0, 0)),
            pl.BlockSpec((1, W_B), lambda i: (0, 0)),
        ],
        out_specs=pl.BlockSpec((tm, W_AB), lambda i: (i, 0)),
        scratch_shapes=[pltpu.VMEM((tm + HALO, W_B), F32)],
        compiler_params=_params(("parallel",)),
        name="ab_mix_prompt",
    )(z, z, o_a, z, w_pool, scale.reshape(1, W_B))


def _ab_mix_sample_kernel(p_ref, prev_ref, oa_ref, g_ref, wp_ref, scale_ref, a_ref):
    p = p_ref[...]
    sums, invs = [], []
    for gi, wnd in enumerate(POOL_WINDOWS):
        cols = slice(gi * G_B, (gi + 1) * G_B)
        acc = p[:, cols]
        for back in range(1, wnd):
            acc = acc + prev_ref[POOL_BUF - back, :, cols]
        sums.append(acc)
        invs.append(1.0 / wnd)
    _pool_mix_gate(sums, p, invs, oa_ref[...], g_ref[...], wp_ref, scale_ref, a_ref)


def _ab_mix_sample(p, prev_t, o_a, g, w_pool, scale):
    m = p.shape[0]
    return pl.pallas_call(
        _ab_mix_sample_kernel,
        out_shape=jax.ShapeDtypeStruct((m, W_AB), BF16),
        name="ab_mix_sample",
        compiler_params=pltpu.CompilerParams(vmem_limit_bytes=VMEM_LIMIT),
    )(p, prev_t, o_a, g, w_pool, scale.reshape(1, W_B))


def _out_proj_kernel(a_ref, w_ref, x_ref, y_ref, *wb_ref):
    w = w_ref[...]
    if wb_ref:
        w = w.astype(BF16)
        wb_ref[0][...] = w
    y_ref[...] = x_ref[...] + jnp.dot(a_ref[...], w, preferred_element_type=F32)


def _out_proj(a, w, x, *, tm, tn, cast_w=False):
    m, kdim = a.shape
    n = w.shape[1]
    assert not cast_w or m == tm
    out_shape = [jax.ShapeDtypeStruct((m, n), F32)]
    out_specs = [pl.BlockSpec((tm, tn), lambda i, j: (i, j))]
    if cast_w:
        out_shape.append(jax.ShapeDtypeStruct((kdim, n), BF16))
        out_specs.append(pl.BlockSpec((kdim, tn), lambda i, j: (0, j)))
    return pl.pallas_call(
        _out_proj_kernel,
        out_shape=tuple(out_shape),
        grid=(m // tm, n // tn),
        in_specs=[
            pl.BlockSpec((tm, kdim), lambda i, j: (i, 0)),
            pl.BlockSpec((kdim, tn), lambda i, j: (0, j)),
            pl.BlockSpec((tm, tn), lambda i, j: (i, j)),
        ],
        out_specs=tuple(out_specs),
        compiler_params=_params(("parallel", "parallel")),
        name="out_proj",
    )(a, w, x)


def _c_gate_prompt_kernel(u_ref, vg_ref, g_ref, vgain_ref, ws_ref, bst_ref, a_ref, vrow_ref, *,
                          chunks):
    gc = u_ref.shape[1] // H_C
    r = lax.broadcasted_iota(jnp.int32, (CHUNK_C, CHUNK_C), 0)
    c = lax.broadcasted_iota(jnp.int32, (CHUNK_C, CHUNK_C), 1)
    lower = r >= c
    for ch in range(chunks):
        rows = slice(ch * CHUNK_C, (ch + 1) * CHUNK_C)
        v = _rms_rows(vg_ref[rows, :], vgain_ref[...])
        if ch == chunks - 1:
            vrow_ref[...] = v
        for hd in range(H_C):
            cols = slice(hd * gc, (hd + 1) * gc)
            ws = jnp.where(lower, ws_ref[hd], 0.0).astype(BF16)
            mixed = jnp.dot(ws, v[:, cols].astype(BF16), preferred_element_type=F32)
            mixed = mixed + bst_ref[:, hd:hd + 1]
            a_ref[rows, cols] = (u_ref[rows, cols] * mixed * _silu(g_ref[rows, cols])).astype(BF16)


def _c_gate_prompt(z, v_gain, w_s, b_s, *, chunks):
    m = z.shape[0]
    wc = z.shape[1] // 3
    tm = chunks * CHUNK_C
    return pl.pallas_call(
        functools.partial(_c_gate_prompt_kernel, chunks=chunks),
        out_shape=(jax.ShapeDtypeStruct((m, wc), BF16),
                   jax.ShapeDtypeStruct((CHUNK_C, wc), F32)),
        grid=(m // tm,),
        in_specs=[
            pl.BlockSpec((tm, wc), lambda i: (i, 0)),
            pl.BlockSpec((tm, wc), lambda i: (i, 1)),
            pl.BlockSpec((tm, wc), lambda i: (i, 2)),
            pl.BlockSpec((1, wc), lambda i: (0, 0)),
            pl.BlockSpec((H_C, CHUNK_C, CHUNK_C), lambda i: (0, 0, 0)),
            pl.BlockSpec((CHUNK_C, H_C), lambda i: (0, 0)),
        ],
        out_specs=(pl.BlockSpec((tm, wc), lambda i: (i, 0)),
                   pl.BlockSpec((CHUNK_C, wc), lambda i: (0, 0))),
        compiler_params=_params(("arbitrary",)),
        name="c_gate_prompt",
    )(z, z, z, v_gain.reshape(1, wc), w_s, b_s.T)


def _c_gate_sample_kernel(u_ref, vg_ref, g_ref, vgain_ref, w00_ref, b0_ref, a_ref, vrow_ref):
    v = _rms_rows(vg_ref[...], vgain_ref[...])
    vrow_ref[...] = v
    mixed = w00_ref[...] * v + b0_ref[...]
    a_ref[...] = (u_ref[...] * mixed * _silu(g_ref[...])).astype(BF16)


def _c_gate_sample(u, vg, g, v_gain, w_s, b_s):
    m, wc = u.shape
    gc = wc // H_C
    w00 = jnp.repeat(w_s[:, 0, 0], gc).reshape(1, wc)
    b0 = jnp.repeat(b_s[:, 0], gc).reshape(1, wc)
    return pl.pallas_call(
        _c_gate_sample_kernel,
        out_shape=(jax.ShapeDtypeStruct((m, wc), BF16), jax.ShapeDtypeStruct((m, wc), F32)),
        name="c_gate_sample",
    )(u, vg, g, v_gain.reshape(1, wc), w00, b0)


def kernel(x_prompt, x_sample, cache_k, cache_v, state_pool, page_table, norm_gain, w_in_ab, q_gain,
           k_gain, sb_bias, w_pool, pool_scale, w_out_ab, w_in_c, v_gain, w_spatial, b_spatial,
           w_out_c):
    bp, s, d = x_prompt.shape
    bs, t, _ = x_sample.shape
    assert bp == 1 and t == 1
    xp = x_prompt.reshape(s, d)
    xs = x_sample.reshape(bs, d)

    w_pool0 = w_pool[0].astype(BF16)
    colgain = jnp.concatenate([jnp.tile(q_gain[0] * (SM_SCALE * LOG2E), N_HEADS),
                               jnp.tile(k_gain[0], N_HEADS),
                               jnp.ones((W_A + W_B + W_AB,), F32)])
    n0 = 3 * W_A + W_B + W_AB
    ab_groups = ((0, n0, F32), (0, 3 * W_A, BF16))
    ab_in = functools.partial(_in_proj, gain=norm_gain[0], tn=AB_IN_PROJ_TN, groups=ab_groups,
                              colgain=colgain, norm_cols=2 * W_A, name="ab_in_proj")
    c_in = functools.partial(_in_proj, gain=norm_gain[1], tn=C_IN_PROJ_TN,
                             groups=((0, 3 * d, F32),), gelu_cols=2 * d, name="c_in_proj")

    z0s, qkv_s, w_in0 = ab_in(xs, w=w_in_ab[0], tm=bs, cast_w=True)
    z0, qkv = ab_in(xp, w=w_in0, tm=IN_PROJ_TM)
    o_a, oa_s = _sb_attention(qkv, qkv_s[:, :W_A], cache_k[0], cache_v[0], page_table, sb_bias[0],
                              th=ATTN_HALF_ROWS, tk=ATTN_SUFFIX_BLOCK, group=DECODE_PAGES_PER_STEP)

    ps = z0s[:, 3 * W_A:3 * W_A + W_B]
    prev_t = jnp.swapaxes(state_pool[0], 0, 1)
    a_s = _ab_mix_sample(ps, prev_t, oa_s, z0s[:, 3 * W_A + W_B:], w_pool0, pool_scale[0])
    xs1, w_out0 = _out_proj(a_s, w_out_ab[0], xs, tm=bs, tn=OUT_PROJ_TN, cast_w=True)
    a = _ab_mix_prompt(z0, o_a, w_pool0, pool_scale[0], tm=AB_MIX_TM)
    (xp1,) = _out_proj(a, w_out0, xp, tm=OUT_PROJ_TM, tn=OUT_PROJ_TN)
    new_k_p = z0[:, W_A:2 * W_A].reshape(1, bp, s, N_HEADS, HEAD_DIM)
    new_v_p = z0[:, 2 * W_A:3 * W_A].reshape(1, bp, s, N_HEADS, HEAD_DIM)
    new_pool_p = z0[s - POOL_BUF:, 3 * W_A:3 * W_A + W_B].reshape(1, bp, POOL_BUF, W_B)
    new_k_s = z0s[:, W_A:2 * W_A].reshape(1, bs, t, N_HEADS, HEAD_DIM)
    new_v_s = z0s[:, 2 * W_A:3 * W_A].reshape(1, bs, t, N_HEADS, HEAD_DIM)
    new_pool_s = jnp.concatenate([state_pool[0][:, 1:], ps[:, None, :]], axis=1)[None]

    z1s, w_in1 = c_in(xs1, w=w_in_c[0], tm=bs, cast_w=True)
    (z1,) = c_in(xp1, w=w_in1, tm=IN_PROJ_TM)
    a1s, v_s = _c_gate_sample(z1s[:, :d], z1s[:, d:2 * d], z1s[:, 2 * d:], v_gain[0], w_spatial[0],
                              b_spatial[0])
    ys, w_out1 = _out_proj(a1s, w_out_c[0], xs1, tm=bs, tn=OUT_PROJ_TN, cast_w=True)
    a1, v_last = _c_gate_prompt(z1, v_gain[0], w_spatial[0], b_spatial[0], chunks=C_GATE_CHUNKS)
    (yp,) = _out_proj(a1, w_out1, xp1, tm=OUT_PROJ_TM, tn=OUT_PROJ_TN)

    return (yp.reshape(bp, s, d), ys.reshape(bs, t, d), new_k_p, new_v_p, new_k_s, new_v_s,
            new_pool_p, new_pool_s, v_last.reshape(1, bp, CHUNK_C, d), v_s.reshape(1, bs, t, d))
```

```python
import functools
import math

import jax
import jax.numpy as jnp
from jax import lax
from jax.experimental import pallas as pl
from jax.experimental.pallas import tpu as pltpu

F32 = jnp.float32
BF16 = jnp.bfloat16

LANES = 128
HEAD_DIM = 128
N_HEADS = 8
W_A = N_HEADS * HEAD_DIM
POOL_WINDOWS = (2, 4, 8, 16)
G_B = 256
W_B = G_B * len(POOL_WINDOWS)
W_AB = W_A + W_B
POOL_BUF = max(POOL_WINDOWS) - 1
HALO = 16
CHUNK_C = 128
H_C = 8
EPS = 1e-6
SM_SCALE = 1.0 / math.sqrt(HEAD_DIM)
LOG2E = math.log2(math.e)
GELU_C = math.sqrt(2.0 / math.pi)
VMEM_LIMIT = 56 * 1024 * 1024

IN_PROJ_TM = 1024
AB_IN_PROJ_TN, C_IN_PROJ_TN = 1024, 1024
IN_PROJ_ROW_CHUNK = 256
OUT_PROJ_TM, OUT_PROJ_TN = 512, 2048
ATTN_HALF_ROWS, ATTN_SUFFIX_BLOCK = 512, 256
DECODE_PAGES_PER_STEP = 8
AB_MIX_TM = 512
C_GATE_CHUNKS = 4


def _params(semantics):
    return pltpu.CompilerParams(dimension_semantics=semantics, vmem_limit_bytes=VMEM_LIMIT)


def _softplus_2(z2):
    neg_abs = lax.bitcast_convert_type(
        lax.bitcast_convert_type(z2, jnp.uint32) | jnp.uint32(0x80000000), F32)
    return jnp.maximum(z2, 0.0) + jnp.log(1.0 + jnp.exp2(neg_abs)) * LOG2E


def _silu(g):
    return g * jax.nn.sigmoid(g)


def _gelu_tanh(x):
    return 0.5 * x * (1.0 + jnp.tanh(GELU_C * (x + 0.044715 * (x * x * x))))


def _rms_rows(xf, gain):
    ms = jnp.mean(xf * xf, axis=-1, keepdims=True)
    return xf * lax.rsqrt(ms + EPS) * gain


def _maskable(dtype):
    return jnp.dtype(dtype).itemsize == 4


def _in_proj_kernel(x_ref, gain_ref, w_ref, *refs, norm_blocks, gelu_blocks, groups, cast_w):
    if norm_blocks:
        colgain_ref, refs = refs[0], refs[1:]
    out_refs, refs = refs[:len(groups)], refs[len(groups):]
    if cast_w:
        wb_ref, refs = refs[0], refs[1:]
    (h_ref,) = refs
    j = pl.program_id(1)

    @pl.when(j == 0)
    def _():
        h_ref[...] = _rms_rows(x_ref[...], gain_ref[...]).astype(BF16)

    if cast_w:
        w = w_ref[...].astype(BF16)
        wb_ref[...] = w
    else:
        w = w_ref[...]
    tm = h_ref.shape[0]
    step = min(tm, IN_PROJ_ROW_CHUNK)
    n_blocks = pl.num_programs(1)
    for r0 in range(0, tm, step):
        rows = slice(r0, r0 + step)
        z = jnp.dot(h_ref[rows, :], w, preferred_element_type=F32)
        if norm_blocks:
            is_norm = j < norm_blocks
            cols = []
            for c in range(z.shape[1] // HEAD_DIM):
                zc = z[:, c * HEAD_DIM:(c + 1) * HEAD_DIM]
                rs = lax.rsqrt(jnp.mean(zc * zc, axis=-1, keepdims=True) + EPS)
                cols.append(zc * jnp.where(is_norm, rs, 1.0))
            z = jnp.concatenate(cols, axis=1) * colgain_ref[...]
        if gelu_blocks:
            z = jnp.where(j < gelu_blocks, _gelu_tanh(z), z)
        for out_ref, (first, count, dtype) in zip(out_refs, groups):
            val = z.astype(dtype)
            if count == n_blocks or not _maskable(dtype):
                out_ref[rows, :] = val
            else:
                inside = (j >= first) & (j < first + count)
                pltpu.store(out_ref.at[rows, :], val, mask=jnp.broadcast_to(inside, val.shape))


def _in_proj(x, gain, w, *, tm, tn, groups, colgain=None, norm_cols=0, gelu_cols=0, cast_w=False,
             name):
    m, d = x.shape
    n = w.shape[1]
    assert not cast_w or m == tm
    in_specs = [
        pl.BlockSpec((tm, d), lambda i, j: (i, 0)),
        pl.BlockSpec((1, d), lambda i, j: (0, 0)),
        pl.BlockSpec((d, tn), lambda i, j: (0, j)),
    ]
    args = [x, gain.reshape(1, d), w]
    if norm_cols:
        in_specs.append(pl.BlockSpec((1, tn), lambda i, j: (0, j)))
        args.append(colgain.reshape(1, n))
    block_groups = tuple((first // tn, cols // tn, dtype) for first, cols, dtype in groups)
    out_shape, out_specs = [], []
    for first, count, dtype in block_groups:
        if _maskable(dtype) or count == n // tn:
            cols, last = count * tn, count - 1
        else:
            assert first == 0
            cols, last = (count + 1) * tn, count
        out_shape.append(jax.ShapeDtypeStruct((m, cols), dtype))
        out_specs.append(pl.BlockSpec(
            (tm, tn), lambda i, j, first=first, last=last: (i, jnp.clip(j - first, 0, last))))
    if cast_w:
        out_shape.append(jax.ShapeDtypeStruct((d, n), BF16))
        out_specs.append(pl.BlockSpec((d, tn), lambda i, j: (0, j)))
    return pl.pallas_call(
        functools.partial(_in_proj_kernel, norm_blocks=norm_cols // tn, gelu_blocks=gelu_cols // tn,
                          groups=block_groups, cast_w=cast_w),
        out_shape=tuple(out_shape),
        grid=(m // tm, n // tn),
        in_specs=in_specs,
        out_specs=tuple(out_specs),
        scratch_shapes=[pltpu.VMEM((tm, d), BF16)],
        compiler_params=_params(("parallel", "arbitrary")),
        name=name,
    )(*args)


def _from_key_on(n):
    r = lax.broadcasted_iota(jnp.int32, (n, n), 0)
    c = lax.broadcasted_iota(jnp.int32, (n, n), 1)
    return jnp.where(r >= c, 1.0, 0.0).astype(BF16)


def _sb_attention_kernel(pt_ref, bias_ref, q_ref, k_ref, v_ref, qs_ref, bias_col_ref, ck_ref, cv_ref,
                         o_ref, os_ref, acc_ref, r_ref, kbuf_ref, vbuf_ref, ksem_ref, vsem_ref,
                         accs_ref, rs_ref, dls_ref, dsum_ref, *, th, tk, group):
    h = pl.program_id(0)
    i = pl.program_id(1)
    n_seq, n_pages = pt_ref.shape
    groups_per_seq = n_pages // group
    n_groups = n_seq * groups_per_seq
    groups_per_head = n_groups // pl.num_programs(0)
    first_step = (h == 0) & (i == 0)
    last_step = (h == pl.num_programs(0) - 1) & (i == pl.num_programs(1) - 1)

    def page_copies(cache_ref, buf_ref, sems, n, slot):
        seq = n // groups_per_seq
        pg = n % groups_per_seq
        return [pltpu.make_async_copy(
            cache_ref.at[pt_ref[seq, n_pages - 1 - (pg * group + g)]], buf_ref.at[slot, g],
            sems.at[slot]) for g in range(group)]

    def k_copies(n, slot):
        return page_copies(ck_ref, kbuf_ref, ksem_ref, n, slot)

    def v_copies(n, slot):
        return page_copies(cv_ref, vbuf_ref, vsem_ref, n, slot)

    def decode_second_half(m, v_slot):
        seq = m // groups_per_seq
        fresh = (m % groups_per_seq) == 0
        acc0 = jnp.where(fresh, 0.0, accs_ref[...])
        run0 = jnp.where(fresh, 0.0, rs_ref[...])
        acc, run = _decode_pv(dls_ref[...], dsum_ref[...],
                              [vbuf_ref[v_slot, g] for g in range(group)], acc0, run0)
        accs_ref[...] = acc
        rs_ref[...] = run
        os_ref[seq] = acc

    def decode_fetch(n):
        k_slot = n % 2
        for c in k_copies(n, k_slot) + v_copies(jnp.maximum(n - 1, 0), (n + 2) % 3):
            c.wait()
        nxt = jnp.minimum(n + 1, n_groups - 1)
        for c in k_copies(nxt, 1 - k_slot) + v_copies(nxt, (n + 1) % 3):
            c.start()

    def decode_first_half(n):
        ls2, stacked = _decode_qk(qs_ref[n // groups_per_seq], bias_col_ref[...] * LOG2E,
                                  [kbuf_ref[n % 2, g] for g in range(group)])
        dls_ref[...] = ls2
        dsum_ref[...] = _decode_suffix(stacked)

    def decode_previous(n):
        decode_second_half(jnp.maximum(n - 1, 0), (n + 2) % 3)

    @pl.when(first_step)
    def _():
        for ref in (accs_ref, rs_ref, dls_ref, dsum_ref, os_ref):
            ref[...] = jnp.zeros_like(ref)
        for c in k_copies(0, 0) + v_copies(0, 0) + v_copies(0, 2):
            c.start()

    bias2 = bias_ref[h] * LOG2E
    from_key_on = _from_key_on(tk)
    nsub = th // tk
    half_rows = [slice(half * th, (half + 1) * th) for half in range(2)]
    even = 1 - i % 2
    group_base = h * groups_per_head + (i * (i - 1)) // 2 + (i + 1) // 2

    def qk_stage(half, kt, masked):
        ks = pl.multiple_of(kt * th, th)
        z2 = lax.dot_general(q_ref[half_rows[half], :], k_ref[pl.ds(ks, th), :],
                             (((1,), (1,)), ((), ())), preferred_element_type=F32) + bias2
        sp2 = _softplus_2(z2)
        valid = None
        if masked:
            t_pos = lax.broadcasted_iota(jnp.int32, (th, th), 0)
            s_pos = lax.broadcasted_iota(jnp.int32, (th, th), 1)
            valid = s_pos < t_pos
            sp2 = jnp.where(valid, sp2, 0.0)
        blocks = [sp2[:, c * tk:(c + 1) * tk] for c in range(nsub)]
        totals = [jnp.broadcast_to(jnp.sum(b, axis=-1, keepdims=True), (th, LANES)) for b in blocks]
        return ks, z2, jnp.concatenate(blocks, axis=0).astype(BF16), totals, valid

    def suffix_stage(st):
        return jnp.dot(st[2], from_key_on, preferred_element_type=F32)

    def pv_stage(st, from_key, run):
        ks, z2, _, totals, valid = st
        ws = [None] * nsub
        for c in reversed(range(nsub)):
            offs = jnp.concatenate([run] * (tk // LANES), axis=1)
            ws[c] = jnp.exp2(z2[:, c * tk:(c + 1) * tk] - from_key[c * th:(c + 1) * th] - offs)
            run = run + totals[c]
        w = jnp.concatenate(ws, axis=1)
        if valid is not None:
            w = jnp.where(valid, w, 0.0)
        return jnp.dot(w.astype(BF16), v_ref[pl.ds(ks, th), :], preferred_element_type=F32), run

    def diagonal(with_decode):
        if with_decode:
            decode_fetch(group_base)
        zero = jnp.zeros((th, LANES), F32)
        st = qk_stage(1, 2 * i + 1, True)
        pv_hi, run1 = pv_stage(st, suffix_stage(st), zero)
        if with_decode:
            decode_previous(group_base)
        st = qk_stage(0, 2 * i, True)
        pv0, run0 = pv_stage(st, suffix_stage(st), zero)
        if with_decode:
            decode_first_half(group_base)
        st = qk_stage(1, 2 * i, False)
        pv_lo, run1 = pv_stage(st, suffix_stage(st), run1)
        acc_ref[half_rows[0], :] = pv0
        acc_ref[half_rows[1], :] = pv_hi + pv_lo
        r_ref[half_rows[0], :] = run0
        r_ref[half_rows[1], :] = run1

    pl.when(even == 1)(functools.partial(diagonal, True))
    pl.when(even == 0)(functools.partial(diagonal, False))

    def body(it, carry):
        kt = 2 * i - 1 - 2 * it
        n = group_base + even + it
        decode_fetch(n)
        pvs = [None] * 4
        runs = [r_ref[rows, :] for rows in half_rows]
        for t, (half, ktile) in enumerate([(0, kt), (1, kt), (0, kt - 1), (1, kt - 1)]):
            st = qk_stage(half, ktile, False)
            pvs[t], runs[half] = pv_stage(st, suffix_stage(st), runs[half])
            if t == 0:
                decode_previous(n)
            if t == 2:
                decode_first_half(n)
        for half, rows in enumerate(half_rows):
            acc_ref[rows, :] += pvs[half] + pvs[half + 2]
            r_ref[rows, :] = runs[half]
        return carry

    lax.fori_loop(0, i, body, 0)
    o_ref[...] = acc_ref[...]

    @pl.when(last_step)
    def _():
        last = n_groups - 1
        for c in v_copies(last, last % 3):
            c.wait()
        decode_second_half(last, last % 3)
        for c in k_copies(last, n_groups % 2) + v_copies(last, n_groups % 3):
            c.wait()


def _sb_attention(qkv, q_s, cache_k, cache_v, page_table, bias, *, th, tk, group):
    s = qkv.shape[0]
    b = q_s.shape[0]
    page = cache_k.shape[1]
    n_pages = page_table.shape[1]
    tq = 2 * th
    steps = s // tq
    keys = group * page
    assert b * (n_pages // group) == N_HEADS * (steps * (steps - 1) // 2 + (steps + 1) // 2)
    grid_spec = pltpu.PrefetchScalarGridSpec(
        num_scalar_prefetch=1,
        grid=(N_HEADS, steps),
        in_specs=[
            pl.BlockSpec(memory_space=pltpu.SMEM),
            pl.BlockSpec((tq, HEAD_DIM), lambda h, i, pt: (i, h)),
            pl.BlockSpec((s, HEAD_DIM), lambda h, i, pt: (0, N_HEADS + h)),
            pl.BlockSpec((s, HEAD_DIM), lambda h, i, pt: (0, 2 * N_HEADS + h)),
            pl.BlockSpec((b, N_HEADS, HEAD_DIM), lambda h, i, pt: (0, 0, 0)),
            pl.BlockSpec((N_HEADS, 1), lambda h, i, pt: (0, 0)),
            pl.BlockSpec(memory_space=pl.ANY),
            pl.BlockSpec(memory_space=pl.ANY),
        ],
        out_specs=(pl.BlockSpec((tq, HEAD_DIM), lambda h, i, pt: (i, h)),
                   pl.BlockSpec((b, N_HEADS, HEAD_DIM), lambda h, i, pt: (0, 0, 0))),
        scratch_shapes=[
            pltpu.VMEM((tq, HEAD_DIM), F32),
            pltpu.VMEM((tq, LANES), F32),
            pltpu.VMEM((2, group, page, N_HEADS, HEAD_DIM), F32),
            pltpu.VMEM((3, group, page, N_HEADS, HEAD_DIM), F32),
            pltpu.SemaphoreType.DMA((2,)),
            pltpu.SemaphoreType.DMA((3,)),
            pltpu.VMEM((N_HEADS, HEAD_DIM), F32),
            pltpu.VMEM((N_HEADS, LANES), F32),
            pltpu.VMEM((N_HEADS, keys * N_HEADS), F32),
            pltpu.VMEM((keys * N_HEADS // LANES * N_HEADS, 2 * LANES), F32),
        ],
    )
    o_a, o_s = pl.pallas_call(
        functools.partial(_sb_attention_kernel, th=th, tk=tk, group=group),
        out_shape=(jax.ShapeDtypeStruct((s, W_A), F32),
                   jax.ShapeDtypeStruct((b, N_HEADS, HEAD_DIM), F32)),
        grid_spec=grid_spec,
        compiler_params=_params(("arbitrary", "arbitrary")),
        name="sb_attention",
    )(page_table, bias, qkv, qkv, qkv, q_s.reshape(b, N_HEADS, HEAD_DIM).astype(F32),
      bias.reshape(N_HEADS, 1), cache_k, cache_v)
    return o_a, o_s.reshape(b, W_A)


def _suffix_and_total(n):
    r = lax.broadcasted_iota(jnp.int32, (n, 2 * n), 0)
    c = lax.broadcasted_iota(jnp.int32, (n, 2 * n), 1)
    return jnp.where((r > c) | (c >= n), 1.0, 0.0).astype(BF16)


def _own_lanes():
    lane = lax.broadcasted_iota(jnp.int32, (N_HEADS, LANES), 1)
    row = lax.broadcasted_iota(jnp.int32, (N_HEADS, LANES), 0)
    return (lane % N_HEADS) == row


def _newest_first(n_tiles, tiles_per_page):
    return [g * tiles_per_page + j for g in range(n_tiles // tiles_per_page)
            for j in reversed(range(tiles_per_page))]


def _decode_qk(q, bias2, k_pages):
    page = k_pages[0].shape[0]
    flat = page * N_HEADS
    keys_per_tile = LANES // N_HEADS
    lane = lax.broadcasted_iota(jnp.int32, (N_HEADS, LANES), 1)
    row = lax.broadcasted_iota(jnp.int32, (N_HEADS, LANES), 0)
    at_key = [lane == c * N_HEADS + row for c in range(keys_per_tile)]
    tiles = []
    for kp in k_pages:
        cols = jnp.sum(kp * q[None], axis=-1, keepdims=True)
        for j in range(page // keys_per_tile):
            zt = jnp.zeros((N_HEADS, LANES), F32)
            for c in range(keys_per_tile):
                col = jnp.broadcast_to(cols[j * keys_per_tile + c], (N_HEADS, LANES))
                zt = jnp.where(at_key[c], col, zt)
            tiles.append(zt)
    z2 = jnp.concatenate(tiles, axis=1) + bias2
    sp2 = _softplus_2(z2)
    ls2 = z2 - sp2
    own_t = _own_lanes()
    order = _newest_first(z2.shape[1] // LANES, flat // LANES)
    stacked = jnp.concatenate(
        [jnp.where(own_t, sp2[:, t * LANES:(t + 1) * LANES], 0.0) for t in order], axis=0)
    hi = stacked.astype(BF16)
    lo = (stacked - hi.astype(F32)).astype(BF16)
    return ls2, jnp.concatenate([hi, lo], axis=0)


def _decode_suffix(stacked):
    s = jnp.dot(stacked, _suffix_and_total(LANES), preferred_element_type=F32)
    half = s.shape[0] // 2
    return s[:half] + s[half:]


def _decode_pv(ls2, sums, v_pages, acc, run):
    flat = v_pages[0].shape[0] * N_HEADS
    own_t = _own_lanes()
    n_tiles = ls2.shape[1] // LANES
    ws = [None] * n_tiles
    for n, t in enumerate(_newest_first(n_tiles, flat // LANES)):
        rows = slice(n * N_HEADS, (n + 1) * N_HEADS)
        later = sums[rows, :LANES] + run
        ws[t] = jnp.where(own_t, jnp.exp2(ls2[:, t * LANES:(t + 1) * LANES] - later), 0.0)
        run = run + sums[rows, LANES:]
    w = jnp.concatenate(ws, axis=1).astype(BF16)
    vm = jnp.concatenate([vp.reshape(flat, HEAD_DIM).astype(BF16) for vp in v_pages], axis=0)
    return acc + jnp.dot(w, vm, preferred_element_type=F32), run


def _pool_mix_gate(window_sums, p, count_inv, o_a, g, wp_ref, scale_ref, a_ref):
    a_ref[:, :W_A] = (o_a * _silu(g[:, :W_A])).astype(BF16)
    for gi in range(len(POOL_WINDOWS)):
        cols = slice(gi * G_B, (gi + 1) * G_B)
        pooled = window_sums[gi] * count_inv[gi] - p[:, cols]
        mixed = jnp.dot(pooled.astype(BF16), wp_ref[gi], preferred_element_type=F32)
        o_b = mixed * scale_ref[:, cols]
        a_ref[:, W_A + gi * G_B:W_A + (gi + 1) * G_B] = (
            o_b * _silu(g[:, W_A + gi * G_B:W_A + (gi + 1) * G_B])).astype(BF16)


def _ab_mix_prompt_kernel(p_ref, halo_ref, oa_ref, g_ref, wp_ref, scale_ref, a_ref, buf_ref, *, tm):
    i = pl.program_id(0)
    buf_ref[:HALO, :] = jnp.where(i > 0, halo_ref[...], 0.0)
    buf_ref[HALO:, :] = p_ref[...]
    pos = i * tm + lax.broadcasted_iota(jnp.int32, (tm, 1), 0)
    sums, invs = [], []
    for gi, wnd in enumerate(POOL_WINDOWS):
        cols = slice(gi * G_B, (gi + 1) * G_B)
        acc = buf_ref[:, cols]
        shift = 1
        while shift < wnd:
            acc = acc + pltpu.roll(acc, shift, 0)
            shift *= 2
        sums.append(acc[HALO:])
        invs.append(1.0 / jnp.minimum(wnd, pos + 1).astype(F32))
    _pool_mix_gate(sums, p_ref[...], invs, oa_ref[...], g_ref[...], wp_ref, scale_ref, a_ref)


def _ab_mix_prompt(z, o_a, w_pool, scale, *, tm):
    m = z.shape[0]
    halo_blocks = tm // HALO
    p_col = 3 * W_A // W_B
    g_col = (3 * W_A + W_B) // W_AB
    return pl.pallas_call(
        functools.partial(_ab_mix_prompt_kernel, tm=tm),
        out_shape=jax.ShapeDtypeStruct((m, W_AB), BF16),
        grid=(m // tm,),
        in_specs=[
            pl.BlockSpec((tm, W_B), lambda i: (i, p_col)),
            pl.BlockSpec((HALO, W_B), lambda i: (jnp.maximum(i * halo_blocks - 1, 0), p_col)),
            pl.BlockSpec((tm, W_A), lambda i: (i, 0)),
            pl.BlockSpec((tm, W_AB), lambda i: (i, g_col)),
            pl.BlockSpec((len(POOL_WINDOWS), G_B, G_B), lambda i: (0, 0, 0)),
            pl.BlockSpec((1, W_B), lambda i: (0, 0)),
        ],
        out_specs=pl.BlockSpec((tm, W_AB), lambda i: (i, 0)),
        scratch_shapes=[pltpu.VMEM((tm + HALO, W_B), F32)],
        compiler_params=_params(("parallel",)),
        name="ab_mix_prompt",
    )(z, z, o_a, z, w_pool, scale.reshape(1, W_B))


def _ab_mix_sample_kernel(p_ref, prev_ref, oa_ref, g_ref, wp_ref, scale_ref, a_ref):
    p = p_ref[...]
    sums, invs = [], []
    for gi, wnd in enumerate(POOL_WINDOWS):
        cols = slice(gi * G_B, (gi + 1) * G_B)
        acc = p[:, cols]
        for back in range(1, wnd):
            acc = acc + prev_ref[POOL_BUF - back, :, cols]
        sums.append(acc)
        invs.append(1.0 / wnd)
    _pool_mix_gate(sums, p, invs, oa_ref[...], g_ref[...], wp_ref, scale_ref, a_ref)


def _ab_mix_sample(p, prev_t, o_a, g, w_pool, scale):
    m = p.shape[0]
    return pl.pallas_call(
        _ab_mix_sample_kernel,
        out_shape=jax.ShapeDtypeStruct((m, W_AB), BF16),
        name="ab_mix_sample",
        compiler_params=pltpu.CompilerParams(vmem_limit_bytes=VMEM_LIMIT),
    )(p, prev_t, o_a, g, w_pool, scale.reshape(1, W_B))


def _out_proj_kernel(a_ref, w_ref, x_ref, y_ref, *wb_ref):
    w = w_ref[...]
    if wb_ref:
        w = w.astype(BF16)
        wb_ref[0][...] = w
    y_ref[...] = x_ref[...] + jnp.dot(a_ref[...], w, preferred_element_type=F32)


def _out_proj(a, w, x, *, tm, tn, cast_w=False):
    m, kdim = a.shape
    n = w.shape[1]
    assert not cast_w or m == tm
    out_shape = [jax.ShapeDtypeStruct((m, n), F32)]
    out_specs = [pl.BlockSpec((tm, tn), lambda i, j: (i, j))]
    if cast_w:
        out_shape.append(jax.ShapeDtypeStruct((kdim, n), BF16))
        out_specs.append(pl.BlockSpec((kdim, tn), lambda i, j: (0, j)))
    return pl.pallas_call(
        _out_proj_kernel,
        out_shape=tuple(out_shape),
        grid=(m // tm, n // tn),
        in_specs=[
            pl.BlockSpec((tm, kdim), lambda i, j: (i, 0)),
            pl.BlockSpec((kdim, tn), lambda i, j: (0, j)),
            pl.BlockSpec((tm, tn), lambda i, j: (i, j)),
        ],
        out_specs=tuple(out_specs),
        compiler_params=_params(("parallel", "parallel")),
        name="out_proj",
    )(a, w, x)


def _c_gate_prompt_kernel(u_ref, vg_ref, g_ref, vgain_ref, ws_ref, bst_ref, a_ref, vrow_ref, *,
                          chunks):
    gc = u_ref.shape[1] // H_C
    r = lax.broadcasted_iota(jnp.int32, (CHUNK_C, CHUNK_C), 0)
    c = lax.broadcasted_iota(jnp.int32, (CHUNK_C, CHUNK_C), 1)
    lower = r >= c
    for ch in range(chunks):
        rows = slice(ch * CHUNK_C, (ch + 1) * CHUNK_C)
        v = _rms_rows(vg_ref[rows, :], vgain_ref[...])
        if ch == chunks - 1:
            vrow_ref[...] = v
        for hd in range(H_C):
            cols = slice(hd * gc, (hd + 1) * gc)
            ws = jnp.where(lower, ws_ref[hd], 0.0).astype(BF16)
            mixed = jnp.dot(ws, v[:, cols].astype(BF16), preferred_element_type=F32)
            mixed = mixed + bst_ref[:, hd:hd + 1]
            a_ref[rows, cols] = (u_ref[rows, cols] * mixed * _silu(g_ref[rows, cols])).astype(BF16)


def _c_gate_prompt(z, v_gain, w_s, b_s, *, chunks):
    m = z.shape[0]
    wc = z.shape[1] // 3
    tm = chunks * CHUNK_C
    return pl.pallas_call(
        functools.partial(_c_gate_prompt_kernel, chunks=chunks),
        out_shape=(jax.ShapeDtypeStruct((m, wc), BF16),
                   jax.ShapeDtypeStruct((CHUNK_C, wc), F32)),
        grid=(m // tm,),
        in_specs=[
            pl.BlockSpec((tm, wc), lambda i: (i, 0)),
            pl.BlockSpec((tm, wc), lambda i: (i, 1)),
            pl.BlockSpec((tm, wc), lambda i: (i, 2)),
            pl.BlockSpec((1, wc), lambda i: (0, 0)),
            pl.BlockSpec((H_C, CHUNK_C, CHUNK_C), lambda i: (0, 0, 0)),
            pl.BlockSpec((CHUNK_C, H_C), lambda i: (0, 0)),
        ],
        out_specs=(pl.BlockSpec((tm, wc), lambda i: (i, 0)),
                   pl.BlockSpec((CHUNK_C, wc), lambda i: (0, 0))),
        compiler_params=_params(("arbitrary",)),
        name="c_gate_prompt",
    )(z, z, z, v_gain.reshape(1, wc), w_s, b_s.T)


def _c_gate_sample_kernel(u_ref, vg_ref, g_ref, vgain_ref, w00_ref, b0_ref, a_ref, vrow_ref):
    v = _rms_rows(vg_ref[...], vgain_ref[...])
    vrow_ref[...] = v
    mixed = w00_ref[...] * v + b0_ref[...]
    a_ref[...] = (u_ref[...] * mixed * _silu(g_ref[...])).astype(BF16)


def _c_gate_sample(u, vg, g, v_gain, w_s, b_s):
    m, wc = u.shape
    gc = wc // H_C
    w00 = jnp.repeat(w_s[:, 0, 0], gc).reshape(1, wc)
    b0 = jnp.repeat(b_s[:, 0], gc).reshape(1, wc)
    return pl.pallas_call(
        _c_gate_sample_kernel,
        out_shape=(jax.ShapeDtypeStruct((m, wc), BF16), jax.ShapeDtypeStruct((m, wc), F32)),
        name="c_gate_sample",
    )(u, vg, g, v_gain.reshape(1, wc), w00, b0)


def kernel(x_prompt, x_sample, cache_k, cache_v, state_pool, page_table, norm_gain, w_in_ab, q_gain,
           k_gain, sb_bias, w_pool, pool_scale, w_out_ab, w_in_c, v_gain, w_spatial, b_spatial,
           w_out_c):
    bp, s, d = x_prompt.shape
    bs, t, _ = x_sample.shape
    assert bp == 1 and t == 1
    xp = x_prompt.reshape(s, d)
    xs = x_sample.reshape(bs, d)

    w_pool0 = w_pool[0].astype(BF16)
    colgain = jnp.concatenate([jnp.tile(q_gain[0] * (SM_SCALE * LOG2E), N_HEADS),
                               jnp.tile(k_gain[0], N_HEADS),
                               jnp.ones((W_A + W_B + W_AB,), F32)])
    n0 = 3 * W_A + W_B + W_AB
    ab_groups = ((0, n0, F32), (0, 3 * W_A, BF16))
    ab_in = functools.partial(_in_proj, gain=norm_gain[0], tn=AB_IN_PROJ_TN, groups=ab_groups,
                              colgain=colgain, norm_cols=2 * W_A, name="ab_in_proj")
    c_in = functools.partial(_in_proj, gain=norm_gain[1], tn=C_IN_PROJ_TN,
                             groups=((0, 3 * d, F32),), gelu_cols=2 * d, name="c_in_proj")

    z0s, qkv_s, w_in0 = ab_in(xs, w=w_in_ab[0], tm=bs, cast_w=True)
    z0, qkv = ab_in(xp, w=w_in0, tm=IN_PROJ_TM)
    o_a, oa_s = _sb_attention(qkv, qkv_s[:, :W_A], cache_k[0], cache_v[0], page_table, sb_bias[0],
                              th=ATTN_HALF_ROWS, tk=ATTN_SUFFIX_BLOCK, group=DECODE_PAGES_PER_STEP)

    ps = z0s[:, 3 * W_A:3 * W_A + W_B]
    prev_t = jnp.swapaxes(state_pool[0], 0, 1)
    a_s = _ab_mix_sample(ps, prev_t, oa_s, z0s[:, 3 * W_A + W_B:], w_pool0, pool_scale[0])
    xs1, w_out0 = _out_proj(a_s, w_out_ab[0], xs, tm=bs, tn=OUT_PROJ_TN, cast_w=True)
    a = _ab_mix_prompt(z0, o_a, w_pool0, pool_scale[0], tm=AB_MIX_TM)
    (xp1,) = _out_proj(a, w_out0, xp, tm=OUT_PROJ_TM, tn=OUT_PROJ_TN)
    new_k_p = z0[:, W_A:2 * W_A].reshape(1, bp, s, N_HEADS, HEAD_DIM)
    new_v_p = z0[:, 2 * W_A:3 * W_A].reshape(1, bp, s, N_HEADS, HEAD_DIM)
    new_pool_p = z0[s - POOL_BUF:, 3 * W_A:3 * W_A + W_B].reshape(1, bp, POOL_BUF, W_B)
    new_k_s = z0s[:, W_A:2 * W_A].reshape(1, bs, t, N_HEADS, HEAD_DIM)
    new_v_s = z0s[:, 2 * W_A:3 * W_A].reshape(1, bs, t, N_HEADS, HEAD_DIM)
    new_pool_s = jnp.concatenate([state_pool[0][:, 1:], ps[:, None, :]], axis=1)[None]

    z1s, w_in1 = c_in(xs1, w=w_in_c[0], tm=bs, cast_w=True)
    (z1,) = c_in(xp1, w=w_in1, tm=IN_PROJ_TM)
    a1s, v_s = _c_gate_sample(z1s[:, :d], z1s[:, d:2 * d], z1s[:, 2 * d:], v_gain[0], w_spatial[0],
                              b_spatial[0])
    ys, w_out1 = _out_proj(a1s, w_out_c[0], xs1, tm=bs, tn=OUT_PROJ_TN, cast_w=True)
    a1, v_last = _c_gate_prompt(z1, v_gain[0], w_spatial[0], b_spatial[0], chunks=C_GATE_CHUNKS)
    (yp,) = _out_proj(a1, w_out1, xp1, tm=OUT_PROJ_TM, tn=OUT_PROJ_TN)

    return (yp.reshape(bp, s, d), ys.reshape(bs, t, d), new_k_p, new_v_p, new_k_s, new_v_s,
            new_pool_p, new_pool_s, v_last.reshape(1, bp, CHUNK_C, d), v_s.reshape(1, bs, t, d))
```

```python
import functools
import math

import jax
import jax.numpy as jnp
from jax import lax
from jax.experimental import pallas as pl
from jax.experimental.pallas import tpu as pltpu

F32 = jnp.float32
BF16 = jnp.bfloat16

LANES = 128
HEAD_DIM = 128
N_HEADS = 8
W_A = N_HEADS * HEAD_DIM
POOL_WINDOWS = (2, 4, 8, 16)
G_B = 256
W_B = G_B * len(POOL_WINDOWS)
W_AB = W_A + W_B
POOL_BUF = max(POOL_WINDOWS) - 1
HALO = 16
CHUNK_C = 128
H_C = 8
EPS = 1e-6
SM_SCALE = 1.0 / math.sqrt(HEAD_DIM)
LOG2E = math.log2(math.e)
GELU_C = math.sqrt(2.0 / math.pi)
VMEM_LIMIT = 56 * 1024 * 1024

IN_PROJ_TM = 1024
IN_PROJ_TN = 1024
IN_PROJ_ROW_CHUNK = 128
OUT_PROJ_TM, OUT_PROJ_TN = 512, 2048
ATTN_HALF_ROWS, ATTN_SUFFIX_BLOCK = 512, 256
DECODE_PAGES_PER_STEP = 8
AB_MIX_TM = 512
C_GATE_CHUNKS = 4


def _params(semantics):
    return pltpu.CompilerParams(dimension_semantics=semantics, vmem_limit_bytes=VMEM_LIMIT)


def _softplus_2(z2):
    neg_abs = lax.bitcast_convert_type(
        lax.bitcast_convert_type(z2, jnp.uint32) | jnp.uint32(0x80000000), F32)
    return jnp.maximum(z2, 0.0) + jnp.log(1.0 + jnp.exp2(neg_abs)) * LOG2E


def _silu(g):
    return g * jax.nn.sigmoid(g)


def _gelu_tanh(x):
    return 0.5 * x * (1.0 + jnp.tanh(GELU_C * (x + 0.044715 * (x * x * x))))


def _rms_rows(xf, gain):
    ms = jnp.mean(xf * xf, axis=-1, keepdims=True)
    return xf * lax.rsqrt(ms + EPS) * gain


def _in_proj_kernel(x_ref, gain_ref, w_ref, *refs, norm_blocks, gelu_blocks, n_out, cast_w):
    if norm_blocks:
        colgain_ref, refs = refs[0], refs[1:]
    out_refs, refs = refs[:n_out], refs[n_out:]
    if cast_w:
        wb_ref, refs = refs[0], refs[1:]
    (h_ref,) = refs
    j = pl.program_id(1)

    @pl.when(j == 0)
    def _():
        h_ref[...] = _rms_rows(x_ref[...], gain_ref[...]).astype(BF16)

    if cast_w:
        w = w_ref[...].astype(BF16)
        wb_ref[...] = w
    else:
        w = w_ref[...]
    tm = h_ref.shape[0]
    step = min(tm, IN_PROJ_ROW_CHUNK)
    for r0 in range(0, tm, step):
        rows = slice(r0, r0 + step)
        z = jnp.dot(h_ref[rows, :], w, preferred_element_type=F32)
        if norm_blocks:
            is_norm = j < norm_blocks
            cols = []
            for c in range(z.shape[1] // HEAD_DIM):
                zc = z[:, c * HEAD_DIM:(c + 1) * HEAD_DIM]
                rs = lax.rsqrt(jnp.mean(zc * zc, axis=-1, keepdims=True) + EPS)
                cols.append(zc * jnp.where(is_norm, rs, 1.0))
            z = jnp.concatenate(cols, axis=1) * colgain_ref[...]
        if gelu_blocks:
            z = jnp.where(j < gelu_blocks, _gelu_tanh(z), z)
        for out_ref in out_refs:
            out_ref[rows, :] = z.astype(out_ref.dtype)


def _in_proj(x, gain, w, *, tm, tn, outs, colgain=None, norm_cols=0, gelu_cols=0, cast_w=False,
             name):
    m, d = x.shape
    n = w.shape[1]
    assert not cast_w or m == tm
    in_specs = [
        pl.BlockSpec((tm, d), lambda i, j: (i, 0)),
        pl.BlockSpec((1, d), lambda i, j: (0, 0)),
        pl.BlockSpec((d, tn), lambda i, j: (0, j)),
    ]
    args = [x, gain.reshape(1, d), w]
    if norm_cols:
        in_specs.append(pl.BlockSpec((1, tn), lambda i, j: (0, j)))
        args.append(colgain.reshape(1, n))
    out_shape, out_specs = [], []
    for cols, dtype in outs:
        count = cols // tn
        spare = 0 if cols == n else 1
        out_shape.append(jax.ShapeDtypeStruct((m, (count + spare) * tn), dtype))
        out_specs.append(pl.BlockSpec(
            (tm, tn), lambda i, j, last=count + spare - 1: (i, jnp.minimum(j, last))))
    if cast_w:
        out_shape.append(jax.ShapeDtypeStruct((d, n), BF16))
        out_specs.append(pl.BlockSpec((d, tn), lambda i, j: (0, j)))
    return pl.pallas_call(
        functools.partial(_in_proj_kernel, norm_blocks=norm_cols // tn,
                          gelu_blocks=gelu_cols // tn, n_out=len(outs), cast_w=cast_w),
        out_shape=tuple(out_shape),
        grid=(m // tm, n // tn),
        in_specs=in_specs,
        out_specs=tuple(out_specs),
        scratch_shapes=[pltpu.VMEM((tm, d), BF16)],
        compiler_params=_params(("parallel", "arbitrary")),
        name=name,
    )(*args)


def _from_key_on(n):
    r = lax.broadcasted_iota(jnp.int32, (n, n), 0)
    c = lax.broadcasted_iota(jnp.int32, (n, n), 1)
    return jnp.where(r >= c, 1.0, 0.0).astype(BF16)


def _sb_attention_kernel(pt_ref, bias_ref, q_ref, k_ref, v_ref, qs_ref, bias_col_ref, ck_ref, cv_ref,
                         o_ref, os_ref, acc_ref, r_ref, kbuf_ref, vbuf_ref, ksem_ref, vsem_ref,
                         accs_ref, rs_ref, dls_ref, dsum_ref, *, th, tk, group):
    h = pl.program_id(0)
    i = pl.program_id(1)
    n_seq, n_pages = pt_ref.shape
    groups_per_seq = n_pages // group
    n_groups = n_seq * groups_per_seq
    groups_per_head = n_groups // pl.num_programs(0)
    first_step = (h == 0) & (i == 0)
    last_step = (h == pl.num_programs(0) - 1) & (i == pl.num_programs(1) - 1)

    def page_copies(cache_ref, buf_ref, sems, n, slot):
        seq = n // groups_per_seq
        pg = n % groups_per_seq
        return [pltpu.make_async_copy(
            cache_ref.at[pt_ref[seq, n_pages - 1 - (pg * group + g)]], buf_ref.at[slot, g],
            sems.at[slot]) for g in range(group)]

    def k_copies(n, slot):
        return page_copies(ck_ref, kbuf_ref, ksem_ref, n, slot)

    def v_copies(n, slot):
        return page_copies(cv_ref, vbuf_ref, vsem_ref, n, slot)

    def decode_second_half(m, v_slot):
        seq = m // groups_per_seq
        fresh = (m % groups_per_seq) == 0
        acc0 = jnp.where(fresh, 0.0, accs_ref[...])
        run0 = jnp.where(fresh, 0.0, rs_ref[...])
        acc, run = _decode_pv(dls_ref[...], dsum_ref[...],
                              [vbuf_ref[v_slot, g] for g in range(group)], acc0, run0)
        accs_ref[...] = acc
        rs_ref[...] = run
        os_ref[seq] = acc

    def decode_fetch(n):
        k_slot = n % 2
        for c in k_copies(n, k_slot) + v_copies(jnp.maximum(n - 1, 0), (n + 2) % 3):
            c.wait()
        nxt = jnp.minimum(n + 1, n_groups - 1)
        for c in k_copies(nxt, 1 - k_slot) + v_copies(nxt, (n + 1) % 3):
            c.start()

    def decode_first_half(n):
        ls2, stacked = _decode_qk(qs_ref[n // groups_per_seq], bias_col_ref[...] * LOG2E,
                                  [kbuf_ref[n % 2, g] for g in range(group)])
        dls_ref[...] = ls2
        dsum_ref[...] = _decode_suffix(stacked)

    def decode_previous(n):
        decode_second_half(jnp.maximum(n - 1, 0), (n + 2) % 3)

    @pl.when(first_step)
    def _():
        for ref in (accs_ref, rs_ref, dls_ref, dsum_ref, os_ref):
            ref[...] = jnp.zeros_like(ref)
        for c in k_copies(0, 0) + v_copies(0, 0) + v_copies(0, 2):
            c.start()

    bias2 = bias_ref[h] * LOG2E
    from_key_on = _from_key_on(tk)
    nsub = th // tk
    half_rows = [slice(half * th, (half + 1) * th) for half in range(2)]
    even = 1 - i % 2
    group_base = h * groups_per_head + (i * (i - 1)) // 2 + (i + 1) // 2

    def qk_stage(half, kt, masked):
        ks = pl.multiple_of(kt * th, th)
        z2 = lax.dot_general(q_ref[half_rows[half], :], k_ref[pl.ds(ks, th), :],
                             (((1,), (1,)), ((), ())), preferred_element_type=F32) + bias2
        sp2 = _softplus_2(z2)
        valid = None
        if masked:
            t_pos = lax.broadcasted_iota(jnp.int32, (th, th), 0)
            s_pos = lax.broadcasted_iota(jnp.int32, (th, th), 1)
            valid = s_pos < t_pos
            sp2 = jnp.where(valid, sp2, 0.0)
        blocks = [sp2[:, c * tk:(c + 1) * tk] for c in range(nsub)]
        totals = [jnp.broadcast_to(jnp.sum(b, axis=-1, keepdims=True), (th, LANES)) for b in blocks]
        return ks, z2, jnp.concatenate(blocks, axis=0).astype(BF16), totals, valid

    def suffix_stage(st):
        return jnp.dot(st[2], from_key_on, preferred_element_type=F32)

    def pv_stage(st, from_key, run):
        ks, z2, _, totals, valid = st
        ws = [None] * nsub
        for c in reversed(range(nsub)):
            offs = jnp.concatenate([run] * (tk // LANES), axis=1)
            ws[c] = jnp.exp2(z2[:, c * tk:(c + 1) * tk] - from_key[c * th:(c + 1) * th] - offs)
            run = run + totals[c]
        w = jnp.concatenate(ws, axis=1)
        if valid is not None:
            w = jnp.where(valid, w, 0.0)
        return jnp.dot(w.astype(BF16), v_ref[pl.ds(ks, th), :], preferred_element_type=F32), run

    def diagonal(with_decode):
        if with_decode:
            decode_fetch(group_base)
        zero = jnp.zeros((th, LANES), F32)
        st = qk_stage(1, 2 * i + 1, True)
        pv_hi, run1 = pv_stage(st, suffix_stage(st), zero)
        if with_decode:
            decode_previous(group_base)
        st = qk_stage(0, 2 * i, True)
        pv0, run0 = pv_stage(st, suffix_stage(st), zero)
        if with_decode:
            decode_first_half(group_base)
        st = qk_stage(1, 2 * i, False)
        pv_lo, run1 = pv_stage(st, suffix_stage(st), run1)
        acc_ref[half_rows[0], :] = pv0
        acc_ref[half_rows[1], :] = pv_hi + pv_lo
        r_ref[half_rows[0], :] = run0
        r_ref[half_rows[1], :] = run1

    pl.when(even == 1)(functools.partial(diagonal, True))
    pl.when(even == 0)(functools.partial(diagonal, False))

    def body(it, carry):
        kt = 2 * i - 1 - 2 * it
        n = group_base + even + it
        decode_fetch(n)
        pvs = [None] * 4
        runs = [r_ref[rows, :] for rows in half_rows]
        for t, (half, ktile) in enumerate([(0, kt), (1, kt), (0, kt - 1), (1, kt - 1)]):
            st = qk_stage(half, ktile, False)
            pvs[t], runs[half] = pv_stage(st, suffix_stage(st), runs[half])
            if t == 0:
                decode_previous(n)
            if t == 2:
                decode_first_half(n)
        for half, rows in enumerate(half_rows):
            acc_ref[rows, :] += pvs[half] + pvs[half + 2]
            r_ref[rows, :] = runs[half]
        return carry

    lax.fori_loop(0, i, body, 0)
    o_ref[...] = acc_ref[...]

    @pl.when(last_step)
    def _():
        last = n_groups - 1
        for c in v_copies(last, last % 3):
            c.wait()
        decode_second_half(last, last % 3)
        for c in k_copies(last, n_groups % 2) + v_copies(last, n_groups % 3):
            c.wait()


def _sb_attention(qkv, q_s, cache_k, cache_v, page_table, bias, *, th, tk, group):
    s = qkv.shape[0]
    b = q_s.shape[0]
    page = cache_k.shape[1]
    n_pages = page_table.shape[1]
    tq = 2 * th
    steps = s // tq
    keys = group * page
    assert b * (n_pages // group) == N_HEADS * (steps * (steps - 1) // 2 + (steps + 1) // 2)
    grid_spec = pltpu.PrefetchScalarGridSpec(
        num_scalar_prefetch=1,
        grid=(N_HEADS, steps),
        in_specs=[
            pl.BlockSpec(memory_space=pltpu.SMEM),
            pl.BlockSpec((tq, HEAD_DIM), lambda h, i, pt: (i, h)),
            pl.BlockSpec((s, HEAD_DIM), lambda h, i, pt: (0, N_HEADS + h)),
            pl.BlockSpec((s, HEAD_DIM), lambda h, i, pt: (0, 2 * N_HEADS + h)),
            pl.BlockSpec((b, N_HEADS, HEAD_DIM), lambda h, i, pt: (0, 0, 0)),
            pl.BlockSpec((N_HEADS, 1), lambda h, i, pt: (0, 0)),
            pl.BlockSpec(memory_space=pl.ANY),
            pl.BlockSpec(memory_space=pl.ANY),
        ],
        out_specs=(pl.BlockSpec((tq, HEAD_DIM), lambda h, i, pt: (i, h)),
                   pl.BlockSpec((b, N_HEADS, HEAD_DIM), lambda h, i, pt: (0, 0, 0))),
        scratch_shapes=[
            pltpu.VMEM((tq, HEAD_DIM), F32),
            pltpu.VMEM((tq, LANES), F32),
            pltpu.VMEM((2, group, page, N_HEADS, HEAD_DIM), F32),
            pltpu.VMEM((3, group, page, N_HEADS, HEAD_DIM), F32),
            pltpu.SemaphoreType.DMA((2,)),
            pltpu.SemaphoreType.DMA((3,)),
            pltpu.VMEM((N_HEADS, HEAD_DIM), F32),
            pltpu.VMEM((N_HEADS, LANES), F32),
            pltpu.VMEM((N_HEADS, keys * N_HEADS), F32),
            pltpu.VMEM((keys * N_HEADS // LANES * N_HEADS, 2 * LANES), F32),
        ],
    )
    o_a, o_s = pl.pallas_call(
        functools.partial(_sb_attention_kernel, th=th, tk=tk, group=group),
        out_shape=(jax.ShapeDtypeStruct((s, W_A), F32),
                   jax.ShapeDtypeStruct((b, N_HEADS, HEAD_DIM), F32)),
        grid_spec=grid_spec,
        compiler_params=_params(("arbitrary", "arbitrary")),
        name="sb_attention",
    )(page_table, bias, qkv, qkv, qkv, q_s.reshape(b, N_HEADS, HEAD_DIM).astype(F32),
      bias.reshape(N_HEADS, 1), cache_k, cache_v)
    return o_a, o_s.reshape(b, W_A)


def _suffix_and_total(n):
    r = lax.broadcasted_iota(jnp.int32, (n, 2 * n), 0)
    c = lax.broadcasted_iota(jnp.int32, (n, 2 * n), 1)
    return jnp.where((r > c) | (c >= n), 1.0, 0.0).astype(BF16)


def _own_lanes():
    lane = lax.broadcasted_iota(jnp.int32, (N_HEADS, LANES), 1)
    row = lax.broadcasted_iota(jnp.int32, (N_HEADS, LANES), 0)
    return (lane % N_HEADS) == row


def _newest_first(n_tiles, tiles_per_page):
    return [g * tiles_per_page + j for g in range(n_tiles // tiles_per_page)
            for j in reversed(range(tiles_per_page))]


def _decode_qk(q, bias2, k_pages):
    page = k_pages[0].shape[0]
    flat = page * N_HEADS
    keys_per_tile = LANES // N_HEADS
    lane = lax.broadcasted_iota(jnp.int32, (N_HEADS, LANES), 1)
    row = lax.broadcasted_iota(jnp.int32, (N_HEADS, LANES), 0)
    at_key = [lane == c * N_HEADS + row for c in range(keys_per_tile)]
    tiles = []
    for kp in k_pages:
        cols = jnp.sum(kp * q[None], axis=-1, keepdims=True)
        for j in range(page // keys_per_tile):
            zt = jnp.zeros((N_HEADS, LANES), F32)
            for c in range(keys_per_tile):
                col = jnp.broadcast_to(cols[j * keys_per_tile + c], (N_HEADS, LANES))
                zt = jnp.where(at_key[c], col, zt)
            tiles.append(zt)
    z2 = jnp.concatenate(tiles, axis=1) + bias2
    sp2 = _softplus_2(z2)
    ls2 = z2 - sp2
    own_t = _own_lanes()
    order = _newest_first(z2.shape[1] // LANES, flat // LANES)
    stacked = jnp.concatenate(
        [jnp.where(own_t, sp2[:, t * LANES:(t + 1) * LANES], 0.0) for t in order], axis=0)
    hi = stacked.astype(BF16)
    lo = (stacked - hi.astype(F32)).astype(BF16)
    return ls2, jnp.concatenate([hi, lo], axis=0)


def _decode_suffix(stacked):
    s = jnp.dot(stacked, _suffix_and_total(LANES), preferred_element_type=F32)
    half = s.shape[0] // 2
    return s[:half] + s[half:]


def _decode_pv(ls2, sums, v_pages, acc, run):
    flat = v_pages[0].shape[0] * N_HEADS
    own_t = _own_lanes()
    n_tiles = ls2.shape[1] // LANES
    ws = [None] * n_tiles
    for n, t in enumerate(_newest_first(n_tiles, flat // LANES)):
        rows = slice(n * N_HEADS, (n + 1) * N_HEADS)
        later = sums[rows, :LANES] + run
        ws[t] = jnp.where(own_t, jnp.exp2(ls2[:, t * LANES:(t + 1) * LANES] - later), 0.0)
        run = run + sums[rows, LANES:]
    w = jnp.concatenate(ws, axis=1).astype(BF16)
    vm = jnp.concatenate([vp.reshape(flat, HEAD_DIM).astype(BF16) for vp in v_pages], axis=0)
    return acc + jnp.dot(w, vm, preferred_element_type=F32), run


def _pool_mix_gate(window_sums, p, count_inv, o_a, g, wp_ref, scale_ref, a_ref):
    a_ref[:, :W_A] = (o_a * _silu(g[:, :W_A])).astype(BF16)
    for gi in range(len(POOL_WINDOWS)):
        cols = slice(gi * G_B, (gi + 1) * G_B)
        pooled = window_sums[gi] * count_inv[gi] - p[:, cols]
        mixed = jnp.dot(pooled.astype(BF16), wp_ref[gi], preferred_element_type=F32)
        o_b = mixed * scale_ref[:, cols]
        a_ref[:, W_A + gi * G_B:W_A + (gi + 1) * G_B] = (
            o_b * _silu(g[:, W_A + gi * G_B:W_A + (gi + 1) * G_B])).astype(BF16)


def _ab_mix_prompt_kernel(p_ref, halo_ref, oa_ref, g_ref, wp_ref, scale_ref, a_ref, buf_ref, *, tm):
    i = pl.program_id(0)
    buf_ref[:HALO, :] = jnp.where(i > 0, halo_ref[...], 0.0)
    buf_ref[HALO:, :] = p_ref[...]
    pos = i * tm + lax.broadcasted_iota(jnp.int32, (tm, 1), 0)
    sums, invs = [], []
    for gi, wnd in enumerate(POOL_WINDOWS):
        cols = slice(gi * G_B, (gi + 1) * G_B)
        acc = buf_ref[:, cols]
        shift = 1
        while shift < wnd:
            acc = acc + pltpu.roll(acc, shift, 0)
            shift *= 2
        sums.append(acc[HALO:])
        invs.append(1.0 / jnp.minimum(wnd, pos + 1).astype(F32))
    _pool_mix_gate(sums, p_ref[...], invs, oa_ref[...], g_ref[...], wp_ref, scale_ref, a_ref)


def _ab_mix_prompt(z, o_a, w_pool, scale, *, tm):
    m = z.shape[0]
    halo_blocks = tm // HALO
    p_col = 3 * W_A // W_B
    g_col = (3 * W_A + W_B) // W_AB
    return pl.pallas_call(
        functools.partial(_ab_mix_prompt_kernel, tm=tm),
        out_shape=jax.ShapeDtypeStruct((m, W_AB), BF16),
        grid=(m // tm,),
        in_specs=[
            pl.BlockSpec((tm, W_B), lambda i: (i, p_col)),
            pl.BlockSpec((HALO, W_B), lambda i: (jnp.maximum(i * halo_blocks - 1, 0), p_col)),
            pl.BlockSpec((tm, W_A), lambda i: (i, 0)),
            pl.BlockSpec((tm, W_AB), lambda i: (i, g_col)),
            pl.BlockSpec((len(POOL_WINDOWS), G_B, G_B), lambda i: (0, 0, 0)),
            pl.BlockSpec((1, W_B), lambda i: (0, 0)),
        ],
        out_specs=pl.BlockSpec((tm, W_AB), lambda i: (i, 0)),
        scratch_shapes=[pltpu.VMEM((tm + HALO, W_B), F32)],
        compiler_params=_params(("parallel",)),
        name="ab_mix_prompt",
    )(z, z, o_a, z, w_pool, scale.reshape(1, W_B))


def _ab_mix_sample_kernel(p_ref, prev_ref, oa_ref, g_ref, wp_ref, scale_ref, a_ref):
    p = p_ref[...]
    sums, invs = [], []
    for gi, wnd in enumerate(POOL_WINDOWS):
        cols = slice(gi * G_B, (gi + 1) * G_B)
        acc = p[:, cols]
        for back in range(1, wnd):
            acc = acc + prev_ref[POOL_BUF - back, :, cols]
        sums.append(acc)
        invs.append(1.0 / wnd)
    _pool_mix_gate(sums, p, invs, oa_ref[...], g_ref[...], wp_ref, scale_ref, a_ref)


def _ab_mix_sample(p, prev_t, o_a, g, w_pool, scale):
    m = p.shape[0]
    return pl.pallas_call(
        _ab_mix_sample_kernel,
        out_shape=jax.ShapeDtypeStruct((m, W_AB), BF16),
        name="ab_mix_sample",
        compiler_params=pltpu.CompilerParams(vmem_limit_bytes=VMEM_LIMIT),
    )(p, prev_t, o_a, g, w_pool, scale.reshape(1, W_B))


def _out_proj_kernel(a_ref, w_ref, x_ref, y_ref, *wb_ref):
    w = w_ref[...]
    if wb_ref:
        w = w.astype(BF16)
        wb_ref[0][...] = w
    y_ref[...] = x_ref[...] + jnp.dot(a_ref[...], w, preferred_element_type=F32)


def _out_proj(a, w, x, *, tm, tn, cast_w=False):
    m, kdim = a.shape
    n = w.shape[1]
    assert not cast_w or m == tm
    out_shape = [jax.ShapeDtypeStruct((m, n), F32)]
    out_specs = [pl.BlockSpec((tm, tn), lambda i, j: (i, j))]
    if cast_w:
        out_shape.append(jax.ShapeDtypeStruct((kdim, n), BF16))
        out_specs.append(pl.BlockSpec((kdim, tn), lambda i, j: (0, j)))
    return pl.pallas_call(
        _out_proj_kernel,
        out_shape=tuple(out_shape),
        grid=(m // tm, n // tn),
        in_specs=[
            pl.BlockSpec((tm, kdim), lambda i, j: (i, 0)),
            pl.BlockSpec((kdim, tn), lambda i, j: (0, j)),
            pl.BlockSpec((tm, tn), lambda i, j: (i, j)),
        ],
        out_specs=tuple(out_specs),
        compiler_params=_params(("parallel", "parallel")),
        name="out_proj",
    )(a, w, x)


def _c_gate_prompt_kernel(u_ref, vg_ref, g_ref, vgain_ref, ws_ref, bst_ref, a_ref, vrow_ref, *,
                          chunks):
    gc = u_ref.shape[1] // H_C
    r = lax.broadcasted_iota(jnp.int32, (CHUNK_C, CHUNK_C), 0)
    c = lax.broadcasted_iota(jnp.int32, (CHUNK_C, CHUNK_C), 1)
    lower = r >= c
    for ch in range(chunks):
        rows = slice(ch * CHUNK_C, (ch + 1) * CHUNK_C)
        v = _rms_rows(vg_ref[rows, :], vgain_ref[...])
        if ch == chunks - 1:
            vrow_ref[...] = v
        for hd in range(H_C):
            cols = slice(hd * gc, (hd + 1) * gc)
            ws = jnp.where(lower, ws_ref[hd], 0.0).astype(BF16)
            mixed = jnp.dot(ws, v[:, cols].astype(BF16), preferred_element_type=F32)
            mixed = mixed + bst_ref[:, hd:hd + 1]
            a_ref[rows, cols] = (u_ref[rows, cols] * mixed * _silu(g_ref[rows, cols])).astype(BF16)


def _c_gate_prompt(z, v_gain, w_s, b_s, *, chunks):
    m = z.shape[0]
    wc = z.shape[1] // 3
    tm = chunks * CHUNK_C
    return pl.pallas_call(
        functools.partial(_c_gate_prompt_kernel, chunks=chunks),
        out_shape=(jax.ShapeDtypeStruct((m, wc), BF16),
                   jax.ShapeDtypeStruct((CHUNK_C, wc), F32)),
        grid=(m // tm,),
        in_specs=[
            pl.BlockSpec((tm, wc), lambda i: (i, 0)),
            pl.BlockSpec((tm, wc), lambda i: (i, 1)),
            pl.BlockSpec((tm, wc), lambda i: (i, 2)),
            pl.BlockSpec((1, wc), lambda i: (0, 0)),
            pl.BlockSpec((H_C, CHUNK_C, CHUNK_C), lambda i: (0, 0, 0)),
            pl.BlockSpec((CHUNK_C, H_C), lambda i: (0, 0)),
        ],
        out_specs=(pl.BlockSpec((tm, wc), lambda i: (i, 0)),
                   pl.BlockSpec((CHUNK_C, wc), lambda i: (0, 0))),
        compiler_params=_params(("arbitrary",)),
        name="c_gate_prompt",
    )(z, z, z, v_gain.reshape(1, wc), w_s, b_s.T)


def _c_gate_sample_kernel(u_ref, vg_ref, g_ref, vgain_ref, w00_ref, b0_ref, a_ref, vrow_ref):
    v = _rms_rows(vg_ref[...], vgain_ref[...])
    vrow_ref[...] = v
    mixed = w00_ref[...] * v + b0_ref[...]
    a_ref[...] = (u_ref[...] * mixed * _silu(g_ref[...])).astype(BF16)


def _c_gate_sample(u, vg, g, v_gain, w_s, b_s):
    m, wc = u.shape
    gc = wc // H_C
    w00 = jnp.repeat(w_s[:, 0, 0], gc).reshape(1, wc)
    b0 = jnp.repeat(b_s[:, 0], gc).reshape(1, wc)
    return pl.pallas_call(
        _c_gate_sample_kernel,
        out_shape=(jax.ShapeDtypeStruct((m, wc), BF16), jax.ShapeDtypeStruct((m, wc), F32)),
        name="c_gate_sample",
    )(u, vg, g, v_gain.reshape(1, wc), w00, b0)


def kernel(x_prompt, x_sample, cache_k, cache_v, state_pool, page_table, norm_gain, w_in_ab, q_gain,
           k_gain, sb_bias, w_pool, pool_scale, w_out_ab, w_in_c, v_gain, w_spatial, b_spatial,
           w_out_c):
    bp, s, d = x_prompt.shape
    bs, t, _ = x_sample.shape
    assert bp == 1 and t == 1
    xp = x_prompt.reshape(s, d)
    xs = x_sample.reshape(bs, d)

    w_pool0 = w_pool[0].astype(BF16)
    colgain = jnp.concatenate([jnp.tile(q_gain[0] * (SM_SCALE * LOG2E), N_HEADS),
                               jnp.tile(k_gain[0], N_HEADS),
                               jnp.ones((W_A + W_B + W_AB,), F32)])
    n0 = 3 * W_A + W_B + W_AB
    ab_in = functools.partial(_in_proj, gain=norm_gain[0], tn=IN_PROJ_TN,
                              outs=((n0, F32), (3 * W_A, BF16)), colgain=colgain,
                              norm_cols=2 * W_A, name="ab_in_proj")
    c_in = functools.partial(_in_proj, gain=norm_gain[1], tn=IN_PROJ_TN, outs=((3 * d, F32),),
                             gelu_cols=2 * d, name="c_in_proj")

    z0s, qkv_s, w_in0 = ab_in(xs, w=w_in_ab[0], tm=bs, cast_w=True)
    z0, qkv = ab_in(xp, w=w_in0, tm=IN_PROJ_TM)
    o_a, oa_s = _sb_attention(qkv, qkv_s[:, :W_A], cache_k[0], cache_v[0], page_table, sb_bias[0],
                              th=ATTN_HALF_ROWS, tk=ATTN_SUFFIX_BLOCK, group=DECODE_PAGES_PER_STEP)

    ps = z0s[:, 3 * W_A:3 * W_A + W_B]
    prev_t = jnp.swapaxes(state_pool[0], 0, 1)
    a_s = _ab_mix_sample(ps, prev_t, oa_s, z0s[:, 3 * W_A + W_B:], w_pool0, pool_scale[0])
    xs1, w_out0 = _out_proj(a_s, w_out_ab[0], xs, tm=bs, tn=OUT_PROJ_TN, cast_w=True)
    a = _ab_mix_prompt(z0, o_a, w_pool0, pool_scale[0], tm=AB_MIX_TM)
    (xp1,) = _out_proj(a, w_out0, xp, tm=OUT_PROJ_TM, tn=OUT_PROJ_TN)
    new_k_p = z0[:, W_A:2 * W_A].reshape(1, bp, s, N_HEADS, HEAD_DIM)
    new_v_p = z0[:, 2 * W_A:3 * W_A].reshape(1, bp, s, N_HEADS, HEAD_DIM)
    new_pool_p = z0[s - POOL_BUF:, 3 * W_A:3 * W_A + W_B].reshape(1, bp, POOL_BUF, W_B)
    new_k_s = z0s[:, W_A:2 * W_A].reshape(1, bs, t, N_HEADS, HEAD_DIM)
    new_v_s = z0s[:, 2 * W_A:3 * W_A].reshape(1, bs, t, N_HEADS, HEAD_DIM)
    new_pool_s = jnp.concatenate([state_pool[0][:, 1:], ps[:, None, :]], axis=1)[None]

    z1s, w_in1 = c_in(xs1, w=w_in_c[0], tm=bs, cast_w=True)
    (z1,) = c_in(xp1, w=w_in1, tm=IN_PROJ_TM)
    a1s, v_s = _c_gate_sample(z1s[:, :d], z1s[:, d:2 * d], z1s[:, 2 * d:], v_gain[0], w_spatial[0],
                              b_spatial[0])
    ys, w_out1 = _out_proj(a1s, w_out_c[0], xs1, tm=bs, tn=OUT_PROJ_TN, cast_w=True)
    a1, v_last = _c_gate_prompt(z1, v_gain[0], w_spatial[0], b_spatial[0], chunks=C_GATE_CHUNKS)
    (yp,) = _out_proj(a1, w_out1, xp1, tm=OUT_PROJ_TM, tn=OUT_PROJ_TN)

    return (yp.reshape(bp, s, d), ys.reshape(bs, t, d), new_k_p, new_v_p, new_k_s, new_v_s,
            new_pool_p, new_pool_s, v_last.reshape(1, bp, CHUNK_C, d), v_s.reshape(1, bs, t, d))
```

```python
import functools
import math

import jax
import jax.numpy as jnp
from jax import lax
from jax.experimental import pallas as pl
from jax.experimental.pallas import tpu as pltpu

F32 = jnp.float32
BF16 = jnp.bfloat16

LANES = 128
HEAD_DIM = 128
N_HEADS = 8
W_A = N_HEADS * HEAD_DIM
POOL_WINDOWS = (2, 4, 8, 16)
G_B = 256
W_B = G_B * len(POOL_WINDOWS)
W_AB = W_A + W_B
POOL_BUF = max(POOL_WINDOWS) - 1
HALO = 16
CHUNK_C = 128
H_C = 8
EPS = 1e-6
SM_SCALE = 1.0 / math.sqrt(HEAD_DIM)
LOG2E = math.log2(math.e)
GELU_C = math.sqrt(2.0 / math.pi)
VMEM_LIMIT = 56 * 1024 * 1024

IN_PROJ_TM = 1024
IN_PROJ_TN = 1024
IN_PROJ_ROW_CHUNK = 256
OUT_PROJ_TM, OUT_PROJ_TN = 512, 2048
ATTN_HALF_ROWS, ATTN_SUFFIX_BLOCK = 512, 256
DECODE_PAGES_PER_STEP = 8
AB_MIX_TM = 512
C_GATE_CHUNKS = 4


def _params(semantics):
    return pltpu.CompilerParams(dimension_semantics=semantics, vmem_limit_bytes=VMEM_LIMIT)


def _softplus_2(z2):
    neg_abs = lax.bitcast_convert_type(
        lax.bitcast_convert_type(z2, jnp.uint32) | jnp.uint32(0x80000000), F32)
    return jnp.maximum(z2, 0.0) + jnp.log(1.0 + jnp.exp2(neg_abs)) * LOG2E


def _silu(g):
    return g * jax.nn.sigmoid(g)


def _gelu_tanh(x):
    return 0.5 * x * (1.0 + jnp.tanh(GELU_C * (x + 0.044715 * (x * x * x))))


def _rms_rows(xf, gain):
    ms = jnp.mean(xf * xf, axis=-1, keepdims=True)
    return xf * lax.rsqrt(ms + EPS) * gain


def _in_proj_kernel(x_ref, gain_ref, w_ref, *refs, norm_blocks, gelu_blocks, n_out, cast_w):
    if norm_blocks:
        colgain_ref, refs = refs[0], refs[1:]
    out_refs, refs = refs[:n_out], refs[n_out:]
    if cast_w:
        wb_ref, refs = refs[0], refs[1:]
    (h_ref,) = refs
    j = pl.program_id(1)

    @pl.when(j == 0)
    def _():
        h_ref[...] = _rms_rows(x_ref[...], gain_ref[...]).astype(BF16)

    if cast_w:
        w = w_ref[...].astype(BF16)
        wb_ref[...] = w
    else:
        w = w_ref[...]
    tm = h_ref.shape[0]
    step = min(tm, IN_PROJ_ROW_CHUNK)
    for r0 in range(0, tm, step):
        rows = slice(r0, r0 + step)
        z = jnp.dot(h_ref[rows, :], w, preferred_element_type=F32)
        if norm_blocks:
            is_norm = j < norm_blocks
            cols = []
            for c in range(z.shape[1] // HEAD_DIM):
                zc = z[:, c * HEAD_DIM:(c + 1) * HEAD_DIM]
                rs = lax.rsqrt(jnp.mean(zc * zc, axis=-1, keepdims=True) + EPS)
                cols.append(zc * jnp.where(is_norm, rs, 1.0))
            z = jnp.concatenate(cols, axis=1) * colgain_ref[...]
        if gelu_blocks:
            z = jnp.where(j < gelu_blocks, _gelu_tanh(z), z)
        for out_ref in out_refs:
            out_ref[rows, :] = z.astype(out_ref.dtype)


def _in_proj(x, gain, w, *, tm, tn, outs, colgain=None, norm_cols=0, gelu_cols=0, cast_w=False,
             name):
    m, d = x.shape
    n = w.shape[1]
    assert not cast_w or m == tm
    in_specs = [
        pl.BlockSpec((tm, d), lambda i, j: (i, 0)),
        pl.BlockSpec((1, d), lambda i, j: (0, 0)),
        pl.BlockSpec((d, tn), lambda i, j: (0, j)),
    ]
    args = [x, gain.reshape(1, d), w]
    if norm_cols:
        in_specs.append(pl.BlockSpec((1, tn), lambda i, j: (0, j)))
        args.append(colgain.reshape(1, n))
    out_shape, out_specs = [], []
    for cols, dtype in outs:
        count = cols // tn
        spare = 0 if cols == n else 1
        out_shape.append(jax.ShapeDtypeStruct((m, (count + spare) * tn), dtype))
        out_specs.append(pl.BlockSpec(
            (tm, tn), lambda i, j, last=count + spare - 1: (i, jnp.minimum(j, last))))
    if cast_w:
        out_shape.append(jax.ShapeDtypeStruct((d, n), BF16))
        out_specs.append(pl.BlockSpec((d, tn), lambda i, j: (0, j)))
    return pl.pallas_call(
        functools.partial(_in_proj_kernel, norm_blocks=norm_cols // tn,
                          gelu_blocks=gelu_cols // tn, n_out=len(outs), cast_w=cast_w),
        out_shape=tuple(out_shape),
        grid=(m // tm, n // tn),
        in_specs=in_specs,
        out_specs=tuple(out_specs),
        scratch_shapes=[pltpu.VMEM((tm, d), BF16)],
        compiler_params=_params(("parallel", "arbitrary")),
        name=name,
    )(*args)


def _from_key_on(n):
    r = lax.broadcasted_iota(jnp.int32, (n, n), 0)
    c = lax.broadcasted_iota(jnp.int32, (n, n), 1)
    return jnp.where(r >= c, 1.0, 0.0).astype(BF16)


def _sb_attention_kernel(pt_ref, bias_ref, q_ref, k_ref, v_ref, qs_ref, bias_col_ref, ck_ref, cv_ref,
                         o_ref, os_ref, acc_ref, r_ref, kbuf_ref, vbuf_ref, ksem_ref, vsem_ref,
                         accs_ref, rs_ref, dls_ref, dsum_ref, *, th, tk, group):
    h = pl.program_id(0)
    i = pl.program_id(1)
    n_seq, n_pages = pt_ref.shape
    groups_per_seq = n_pages // group
    n_groups = n_seq * groups_per_seq
    groups_per_head = n_groups // pl.num_programs(0)
    first_step = (h == 0) & (i == 0)
    last_step = (h == pl.num_programs(0) - 1) & (i == pl.num_programs(1) - 1)

    def page_copies(cache_ref, buf_ref, sems, n, slot):
        seq = n // groups_per_seq
        pg = n % groups_per_seq
        return [pltpu.make_async_copy(
            cache_ref.at[pt_ref[seq, n_pages - 1 - (pg * group + g)]], buf_ref.at[slot, g],
            sems.at[slot]) for g in range(group)]

    def k_copies(n, slot):
        return page_copies(ck_ref, kbuf_ref, ksem_ref, n, slot)

    def v_copies(n, slot):
        return page_copies(cv_ref, vbuf_ref, vsem_ref, n, slot)

    def decode_second_half(m, v_slot):
        seq = m // groups_per_seq
        fresh = (m % groups_per_seq) == 0
        acc0 = jnp.where(fresh, 0.0, accs_ref[...])
        run0 = jnp.where(fresh, 0.0, rs_ref[...])
        acc, run = _decode_pv(dls_ref[...], dsum_ref[...],
                              [vbuf_ref[v_slot, g] for g in range(group)], acc0, run0)
        accs_ref[...] = acc
        rs_ref[...] = run
        os_ref[seq] = acc

    def decode_fetch(n):
        k_slot = n % 2
        for c in k_copies(n, k_slot) + v_copies(jnp.maximum(n - 1, 0), (n + 2) % 3):
            c.wait()
        nxt = jnp.minimum(n + 1, n_groups - 1)
        for c in k_copies(nxt, 1 - k_slot) + v_copies(nxt, (n + 1) % 3):
            c.start()

    def decode_first_half(n):
        ls2, stacked = _decode_qk(qs_ref[n // groups_per_seq], bias_col_ref[...] * LOG2E,
                                  [kbuf_ref[n % 2, g] for g in range(group)])
        dls_ref[...] = ls2
        dsum_ref[...] = _decode_suffix(stacked)

    def decode_previous(n):
        decode_second_half(jnp.maximum(n - 1, 0), (n + 2) % 3)

    @pl.when(first_step)
    def _():
        for ref in (accs_ref, rs_ref, dls_ref, dsum_ref, os_ref):
            ref[...] = jnp.zeros_like(ref)
        for c in k_copies(0, 0) + v_copies(0, 0) + v_copies(0, 2):
            c.start()

    bias2 = bias_ref[h] * LOG2E
    from_key_on = _from_key_on(tk)
    nsub = th // tk
    half_rows = [slice(half * th, (half + 1) * th) for half in range(2)]
    even = 1 - i % 2
    group_base = h * groups_per_head + (i * (i - 1)) // 2 + (i + 1) // 2

    def qk_stage(half, kt, masked):
        ks = pl.multiple_of(kt * th, th)
        z2 = lax.dot_general(q_ref[half_rows[half], :], k_ref[pl.ds(ks, th), :],
                             (((1,), (1,)), ((), ())), preferred_element_type=F32) + bias2
        sp2 = _softplus_2(z2)
        valid = None
        if masked:
            t_pos = lax.broadcasted_iota(jnp.int32, (th, th), 0)
            s_pos = lax.broadcasted_iota(jnp.int32, (th, th), 1)
            valid = s_pos < t_pos
            sp2 = jnp.where(valid, sp2, 0.0)
        blocks = [sp2[:, c * tk:(c + 1) * tk] for c in range(nsub)]
        totals = [jnp.broadcast_to(jnp.sum(b, axis=-1, keepdims=True), (th, LANES)) for b in blocks]
        return ks, z2, jnp.concatenate(blocks, axis=0).astype(BF16), totals, valid

    def suffix_stage(st):
        return jnp.dot(st[2], from_key_on, preferred_element_type=F32)

    def pv_stage(st, from_key, run):
        ks, z2, _, totals, valid = st
        ws = [None] * nsub
        for c in reversed(range(nsub)):
            offs = jnp.concatenate([run] * (tk // LANES), axis=1)
            ws[c] = jnp.exp2(z2[:, c * tk:(c + 1) * tk] - from_key[c * th:(c + 1) * th] - offs)
            run = run + totals[c]
        w = jnp.concatenate(ws, axis=1)
        if valid is not None:
            w = jnp.where(valid, w, 0.0)
        return jnp.dot(w.astype(BF16), v_ref[pl.ds(ks, th), :], preferred_element_type=F32), run

    def diagonal(with_decode):
        if with_decode:
            decode_fetch(group_base)
        zero = jnp.zeros((th, LANES), F32)
        st = qk_stage(1, 2 * i + 1, True)
        pv_hi, run1 = pv_stage(st, suffix_stage(st), zero)
        if with_decode:
            decode_previous(group_base)
        st = qk_stage(0, 2 * i, True)
        pv0, run0 = pv_stage(st, suffix_stage(st), zero)
        if with_decode:
            decode_first_half(group_base)
        st = qk_stage(1, 2 * i, False)
        pv_lo, run1 = pv_stage(st, suffix_stage(st), run1)
        acc_ref[half_rows[0], :] = pv0
        acc_ref[half_rows[1], :] = pv_hi + pv_lo
        r_ref[half_rows[0], :] = run0
        r_ref[half_rows[1], :] = run1

    pl.when(even == 1)(functools.partial(diagonal, True))
    pl.when(even == 0)(functools.partial(diagonal, False))

    def body(it, carry):
        kt = 2 * i - 1 - 2 * it
        n = group_base + even + it
        decode_fetch(n)
        pvs = [None] * 4
        runs = [r_ref[rows, :] for rows in half_rows]
        for t, (half, ktile) in enumerate([(0, kt), (1, kt), (0, kt - 1), (1, kt - 1)]):
            st = qk_stage(half, ktile, False)
            pvs[t], runs[half] = pv_stage(st, suffix_stage(st), runs[half])
            if t == 0:
                decode_previous(n)
            if t == 2:
                decode_first_half(n)
        for half, rows in enumerate(half_rows):
            acc_ref[rows, :] += pvs[half] + pvs[half + 2]
            r_ref[rows, :] = runs[half]
        return carry

    lax.fori_loop(0, i, body, 0)
    o_ref[...] = acc_ref[...]

    @pl.when(last_step)
    def _():
        last = n_groups - 1
        for c in v_copies(last, last % 3):
            c.wait()
        decode_second_half(last, last % 3)
        for c in k_copies(last, n_groups % 2) + v_copies(last, n_groups % 3):
            c.wait()


def _sb_attention(qkv, q_s, cache_k, cache_v, page_table, bias, *, th, tk, group):
    s = qkv.shape[0]
    b = q_s.shape[0]
    page = cache_k.shape[1]
    n_pages = page_table.shape[1]
    tq = 2 * th
    steps = s // tq
    keys = group * page
    assert b * (n_pages // group) == N_HEADS * (steps * (steps - 1) // 2 + (steps + 1) // 2)
    grid_spec = pltpu.PrefetchScalarGridSpec(
        num_scalar_prefetch=1,
        grid=(N_HEADS, steps),
        in_specs=[
            pl.BlockSpec(memory_space=pltpu.SMEM),
            pl.BlockSpec((tq, HEAD_DIM), lambda h, i, pt: (i, h)),
            pl.BlockSpec((s, HEAD_DIM), lambda h, i, pt: (0, N_HEADS + h)),
            pl.BlockSpec((s, HEAD_DIM), lambda h, i, pt: (0, 2 * N_HEADS + h)),
            pl.BlockSpec((b, N_HEADS, HEAD_DIM), lambda h, i, pt: (0, 0, 0)),
            pl.BlockSpec((N_HEADS, 1), lambda h, i, pt: (0, 0)),
            pl.BlockSpec(memory_space=pl.ANY),
            pl.BlockSpec(memory_space=pl.ANY),
        ],
        out_specs=(pl.BlockSpec((tq, HEAD_DIM), lambda h, i, pt: (i, h)),
                   pl.BlockSpec((b, N_HEADS, HEAD_DIM), lambda h, i, pt: (0, 0, 0))),
        scratch_shapes=[
            pltpu.VMEM((tq, HEAD_DIM), F32),
            pltpu.VMEM((tq, LANES), F32),
            pltpu.VMEM((2, group, page, N_HEADS, HEAD_DIM), F32),
            pltpu.VMEM((3, group, page, N_HEADS, HEAD_DIM), F32),
            pltpu.SemaphoreType.DMA((2,)),
            pltpu.SemaphoreType.DMA((3,)),
            pltpu.VMEM((N_HEADS, HEAD_DIM), F32),
            pltpu.VMEM((N_HEADS, LANES), F32),
            pltpu.VMEM((N_HEADS, keys * N_HEADS), F32),
            pltpu.VMEM((keys * N_HEADS // LANES * N_HEADS, 2 * LANES), F32),
        ],
    )
    o_a, o_s = pl.pallas_call(
        functools.partial(_sb_attention_kernel, th=th, tk=tk, group=group),
        out_shape=(jax.ShapeDtypeStruct((s, W_A), F32),
                   jax.ShapeDtypeStruct((b, N_HEADS, HEAD_DIM), F32)),
        grid_spec=grid_spec,
        compiler_params=_params(("arbitrary", "arbitrary")),
        name="sb_attention",
    )(page_table, bias, qkv, qkv, qkv, q_s.reshape(b, N_HEADS, HEAD_DIM).astype(F32),
      bias.reshape(N_HEADS, 1), cache_k, cache_v)
    return o_a, o_s.reshape(b, W_A)


def _suffix_and_total(n):
    r = lax.broadcasted_iota(jnp.int32, (n, 2 * n), 0)
    c = lax.broadcasted_iota(jnp.int32, (n, 2 * n), 1)
    return jnp.where((r > c) | (c >= n), 1.0, 0.0).astype(BF16)


def _own_lanes():
    lane = lax.broadcasted_iota(jnp.int32, (N_HEADS, LANES), 1)
    row = lax.broadcasted_iota(jnp.int32, (N_HEADS, LANES), 0)
    return (lane % N_HEADS) == row


def _newest_first(n_tiles, tiles_per_page):
    return [g * tiles_per_page + j for g in range(n_tiles // tiles_per_page)
            for j in reversed(range(tiles_per_page))]


def _decode_qk(q, bias2, k_pages):
    page = k_pages[0].shape[0]
    flat = page * N_HEADS
    keys_per_tile = LANES // N_HEADS
    lane = lax.broadcasted_iota(jnp.int32, (N_HEADS, LANES), 1)
    row = lax.broadcasted_iota(jnp.int32, (N_HEADS, LANES), 0)
    at_key = [lane == c * N_HEADS + row for c in range(keys_per_tile)]
    tiles = []
    for kp in k_pages:
        cols = jnp.sum(kp * q[None], axis=-1, keepdims=True)
        for j in range(page // keys_per_tile):
            zt = jnp.zeros((N_HEADS, LANES), F32)
            for c in range(keys_per_tile):
                col = jnp.broadcast_to(cols[j * keys_per_tile + c], (N_HEADS, LANES))
                zt = jnp.where(at_key[c], col, zt)
            tiles.append(zt)
    z2 = jnp.concatenate(tiles, axis=1) + bias2
    sp2 = _softplus_2(z2)
    ls2 = z2 - sp2
    own_t = _own_lanes()
    order = _newest_first(z2.shape[1] // LANES, flat // LANES)
    stacked = jnp.concatenate(
        [jnp.where(own_t, sp2[:, t * LANES:(t + 1) * LANES], 0.0) for t in order], axis=0)
    hi = stacked.astype(BF16)
    lo = (stacked - hi.astype(F32)).astype(BF16)
    return ls2, jnp.concatenate([hi, lo], axis=0)


def _decode_suffix(stacked):
    s = jnp.dot(stacked, _suffix_and_total(LANES), preferred_element_type=F32)
    half = s.shape[0] // 2
    return s[:half] + s[half:]


def _decode_pv(ls2, sums, v_pages, acc, run):
    flat = v_pages[0].shape[0] * N_HEADS
    own_t = _own_lanes()
    n_tiles = ls2.shape[1] // LANES
    ws = [None] * n_tiles
    for n, t in enumerate(_newest_first(n_tiles, flat // LANES)):
        rows = slice(n * N_HEADS, (n + 1) * N_HEADS)
        later = sums[rows, :LANES] + run
        ws[t] = jnp.where(own_t, jnp.exp2(ls2[:, t * LANES:(t + 1) * LANES] - later), 0.0)
        run = run + sums[rows, LANES:]
    w = jnp.concatenate(ws, axis=1).astype(BF16)
    vm = jnp.concatenate([vp.reshape(flat, HEAD_DIM).astype(BF16) for vp in v_pages], axis=0)
    return acc + jnp.dot(w, vm, preferred_element_type=F32), run


def _pool_mix_gate(window_sums, p, count_inv, o_a, g, wp_ref, scale_ref, a_ref):
    a_ref[:, :W_A] = (o_a * _silu(g[:, :W_A])).astype(BF16)
    for gi in range(len(POOL_WINDOWS)):
        cols = slice(gi * G_B, (gi + 1) * G_B)
        pooled = window_sums[gi] * count_inv[gi] - p[:, cols]
        mixed = jnp.dot(pooled.astype(BF16), wp_ref[gi], preferred_element_type=F32)
        o_b = mixed * scale_ref[:, cols]
        a_ref[:, W_A + gi * G_B:W_A + (gi + 1) * G_B] = (
            o_b * _silu(g[:, W_A + gi * G_B:W_A + (gi + 1) * G_B])).astype(BF16)


def _ab_mix_prompt_kernel(p_ref, halo_ref, oa_ref, g_ref, wp_ref, scale_ref, a_ref, buf_ref, *, tm):
    i = pl.program_id(0)
    buf_ref[:HALO, :] = jnp.where(i > 0, halo_ref[...], 0.0)
    buf_ref[HALO:, :] = p_ref[...]
    pos = i * tm + lax.broadcasted_iota(jnp.int32, (tm, 1), 0)
    sums, invs = [], []
    for gi, wnd in enumerate(POOL_WINDOWS):
        cols = slice(gi * G_B, (gi + 1) * G_B)
        acc = buf_ref[:, cols]
        shift = 1
        while shift < wnd:
            acc = acc + pltpu.roll(acc, shift, 0)
            shift *= 2
        sums.append(acc[HALO:])
        invs.append(1.0 / jnp.minimum(wnd, pos + 1).astype(F32))
    _pool_mix_gate(sums, p_ref[...], invs, oa_ref[...], g_ref[...], wp_ref, scale_ref, a_ref)


def _ab_mix_prompt(z, o_a, w_pool, scale, *, tm):
    m = z.shape[0]
    halo_blocks = tm // HALO
    p_col = 3 * W_A // W_B
    g_col = (3 * W_A + W_B) // W_AB
    return pl.pallas_call(
        functools.partial(_ab_mix_prompt_kernel, tm=tm),
        out_shape=jax.ShapeDtypeStruct((m, W_AB), BF16),
        grid=(m // tm,),
        in_specs=[
            pl.BlockSpec((tm, W_B), lambda i: (i, p_col)),
            pl.BlockSpec((HALO, W_B), lambda i: (jnp.maximum(i * halo_blocks - 1, 0), p_col)),
            pl.BlockSpec((tm, W_A), lambda i: (i, 0)),
            pl.BlockSpec((tm, W_AB), lambda i: (i, g_col)),
            pl.BlockSpec((len(POOL_WINDOWS), G_B, G_B), lambda i: (0, 0, 0)),
            pl.BlockSpec((1, W_B), lambda i: (0, 0)),
        ],
        out_specs=pl.BlockSpec((tm, W_AB), lambda i: (i, 0)),
        scratch_shapes=[pltpu.VMEM((tm + HALO, W_B), F32)],
        compiler_params=_params(("parallel",)),
        name="ab_mix_prompt",
    )(z, z, o_a, z, w_pool, scale.reshape(1, W_B))


def _ab_mix_sample_kernel(p_ref, prev_ref, oa_ref, g_ref, wp_ref, scale_ref, a_ref):
    p = p_ref[...]
    sums, invs = [], []
    for gi, wnd in enumerate(POOL_WINDOWS):
        cols = slice(gi * G_B, (gi + 1) * G_B)
        acc = p[:, cols]
        for back in range(1, wnd):
            acc = acc + prev_ref[POOL_BUF - back, :, cols]
        sums.append(acc)
        invs.append(1.0 / wnd)
    _pool_mix_gate(sums, p, invs, oa_ref[...], g_ref[...], wp_ref, scale_ref, a_ref)


def _ab_mix_sample(p, prev_t, o_a, g, w_pool, scale):
    m = p.shape[0]
    return pl.pallas_call(
        _ab_mix_sample_kernel,
        out_shape=jax.ShapeDtypeStruct((m, W_AB), BF16),
        name="ab_mix_sample",
        compiler_params=pltpu.CompilerParams(vmem_limit_bytes=VMEM_LIMIT),
    )(p, prev_t, o_a, g, w_pool, scale.reshape(1, W_B))


def _out_proj_kernel(a_ref, w_ref, x_ref, y_ref, *wb_ref):
    w = w_ref[...]
    if wb_ref:
        w = w.astype(BF16)
        wb_ref[0][...] = w
    y_ref[...] = x_ref[...] + jnp.dot(a_ref[...], w, preferred_element_type=F32)


def _out_proj(a, w, x, *, tm, tn, cast_w=False):
    m, kdim = a.shape
    n = w.shape[1]
    assert not cast_w or m == tm
    out_shape = [jax.ShapeDtypeStruct((m, n), F32)]
    out_specs = [pl.BlockSpec((tm, tn), lambda i, j: (i, j))]
    if cast_w:
        out_shape.append(jax.ShapeDtypeStruct((kdim, n), BF16))
        out_specs.append(pl.BlockSpec((kdim, tn), lambda i, j: (0, j)))
    return pl.pallas_call(
        _out_proj_kernel,
        out_shape=tuple(out_shape),
        grid=(m // tm, n // tn),
        in_specs=[
            pl.BlockSpec((tm, kdim), lambda i, j: (i, 0)),
            pl.BlockSpec((kdim, tn), lambda i, j: (0, j)),
            pl.BlockSpec((tm, tn), lambda i, j: (i, j)),
        ],
        out_specs=tuple(out_specs),
        compiler_params=_params(("parallel", "parallel")),
        name="out_proj",
    )(a, w, x)


def _c_gate_prompt_kernel(u_ref, vg_ref, g_ref, vgain_ref, ws_ref, bst_ref, a_ref, vrow_ref, *,
                          chunks):
    gc = u_ref.shape[1] // H_C
    r = lax.broadcasted_iota(jnp.int32, (CHUNK_C, CHUNK_C), 0)
    c = lax.broadcasted_iota(jnp.int32, (CHUNK_C, CHUNK_C), 1)
    lower = r >= c
    for ch in range(chunks):
        rows = slice(ch * CHUNK_C, (ch + 1) * CHUNK_C)
        v = _rms_rows(vg_ref[rows, :], vgain_ref[...])
        if ch == chunks - 1:
            vrow_ref[...] = v
        for hd in range(H_C):
            cols = slice(hd * gc, (hd + 1) * gc)
            ws = jnp.where(lower, ws_ref[hd], 0.0).astype(BF16)
            mixed = jnp.dot(ws, v[:, cols].astype(BF16), preferred_element_type=F32)
            mixed = mixed + bst_ref[:, hd:hd + 1]
            a_ref[rows, cols] = (u_ref[rows, cols] * mixed * _silu(g_ref[rows, cols])).astype(BF16)


def _c_gate_prompt(z, v_gain, w_s, b_s, *, chunks):
    m = z.shape[0]
    wc = z.shape[1] // 3
    tm = chunks * CHUNK_C
    return pl.pallas_call(
        functools.partial(_c_gate_prompt_kernel, chunks=chunks),
        out_shape=(jax.ShapeDtypeStruct((m, wc), BF16),
                   jax.ShapeDtypeStruct((CHUNK_C, wc), F32)),
        grid=(m // tm,),
        in_specs=[
            pl.BlockSpec((tm, wc), lambda i: (i, 0)),
            pl.BlockSpec((tm, wc), lambda i: (i, 1)),
            pl.BlockSpec((tm, wc), lambda i: (i, 2)),
            pl.BlockSpec((1, wc), lambda i: (0, 0)),
            pl.BlockSpec((H_C, CHUNK_C, CHUNK_C), lambda i: (0, 0, 0)),
            pl.BlockSpec((CHUNK_C, H_C), lambda i: (0, 0)),
        ],
        out_specs=(pl.BlockSpec((tm, wc), lambda i: (i, 0)),
                   pl.BlockSpec((CHUNK_C, wc), lambda i: (0, 0))),
        compiler_params=_params(("arbitrary",)),
        name="c_gate_prompt",
    )(z, z, z, v_gain.reshape(1, wc), w_s, b_s.T)


def _c_gate_sample_kernel(u_ref, vg_ref, g_ref, vgain_ref, w00_ref, b0_ref, a_ref, vrow_ref):
    v = _rms_rows(vg_ref[...], vgain_ref[...])
    vrow_ref[...] = v
    mixed = w00_ref[...] * v + b0_ref[...]
    a_ref[...] = (u_ref[...] * mixed * _silu(g_ref[...])).astype(BF16)


def _c_gate_sample(u, vg, g, v_gain, w_s, b_s):
    m, wc = u.shape
    gc = wc // H_C
    w00 = jnp.repeat(w_s[:, 0, 0], gc).reshape(1, wc)
    b0 = jnp.repeat(b_s[:, 0], gc).reshape(1, wc)
    return pl.pallas_call(
        _c_gate_sample_kernel,
        out_shape=(jax.ShapeDtypeStruct((m, wc), BF16), jax.ShapeDtypeStruct((m, wc), F32)),
        name="c_gate_sample",
    )(u, vg, g, v_gain.reshape(1, wc), w00, b0)


def kernel(x_prompt, x_sample, cache_k, cache_v, state_pool, page_table, norm_gain, w_in_ab, q_gain,
           k_gain, sb_bias, w_pool, pool_scale, w_out_ab, w_in_c, v_gain, w_spatial, b_spatial,
           w_out_c):
    bp, s, d = x_prompt.shape
    bs, t, _ = x_sample.shape
    assert bp == 1 and t == 1
    xp = x_prompt.reshape(s, d)
    xs = x_sample.reshape(bs, d)

    w_pool0 = w_pool[0].astype(BF16)
    colgain = jnp.concatenate([jnp.tile(q_gain[0] * (SM_SCALE * LOG2E), N_HEADS),
                               jnp.tile(k_gain[0], N_HEADS),
                               jnp.ones((W_A + W_B + W_AB,), F32)])
    n0 = 3 * W_A + W_B + W_AB
    ab_in = functools.partial(_in_proj, gain=norm_gain[0], tn=IN_PROJ_TN,
                              outs=((n0, F32), (3 * W_A, BF16)), colgain=colgain,
                              norm_cols=2 * W_A, name="ab_in_proj")
    c_in = functools.partial(_in_proj, gain=norm_gain[1], tn=IN_PROJ_TN, outs=((3 * d, F32),),
                             gelu_cols=2 * d, name="c_in_proj")

    z0s, qkv_s, w_in0 = ab_in(xs, w=w_in_ab[0], tm=bs, cast_w=True)
    z0, qkv = ab_in(xp, w=w_in0, tm=IN_PROJ_TM)
    o_a, oa_s = _sb_attention(qkv, qkv_s[:, :W_A], cache_k[0], cache_v[0], page_table, sb_bias[0],
                              th=ATTN_HALF_ROWS, tk=ATTN_SUFFIX_BLOCK, group=DECODE_PAGES_PER_STEP)

    ps = z0s[:, 3 * W_A:3 * W_A + W_B]
    prev_t = jnp.swapaxes(state_pool[0], 0, 1)
    a_s = _ab_mix_sample(ps, prev_t, oa_s, z0s[:, 3 * W_A + W_B:], w_pool0, pool_scale[0])
    xs1, w_out0 = _out_proj(a_s, w_out_ab[0], xs, tm=bs, tn=OUT_PROJ_TN, cast_w=True)
    a = _ab_mix_prompt(z0, o_a, w_pool0, pool_scale[0], tm=AB_MIX_TM)
    (xp1,) = _out_proj(a, w_out0, xp, tm=OUT_PROJ_TM, tn=OUT_PROJ_TN)
    new_k_p = z0[:, W_A:2 * W_A].reshape(1, bp, s, N_HEADS, HEAD_DIM)
    new_v_p = z0[:, 2 * W_A:3 * W_A].reshape(1, bp, s, N_HEADS, HEAD_DIM)
    new_pool_p = z0[s - POOL_BUF:, 3 * W_A:3 * W_A + W_B].reshape(1, bp, POOL_BUF, W_B)
    new_k_s = z0s[:, W_A:2 * W_A].reshape(1, bs, t, N_HEADS, HEAD_DIM)
    new_v_s = z0s[:, 2 * W_A:3 * W_A].reshape(1, bs, t, N_HEADS, HEAD_DIM)
    new_pool_s = jnp.concatenate([state_pool[0][:, 1:], ps[:, None, :]], axis=1)[None]

    z1s, w_in1 = c_in(xs1, w=w_in_c[0], tm=bs, cast_w=True)
    (z1,) = c_in(xp1, w=w_in1, tm=IN_PROJ_TM)
    a1s, v_s = _c_gate_sample(z1s[:, :d], z1s[:, d:2 * d], z1s[:, 2 * d:], v_gain[0], w_spatial[0],
                              b_spatial[0])
    ys, w_out1 = _out_proj(a1s, w_out_c[0], xs1, tm=bs, tn=OUT_PROJ_TN, cast_w=True)
    a1, v_last = _c_gate_prompt(z1, v_gain[0], w_spatial[0], b_spatial[0], chunks=C_GATE_CHUNKS)
    (yp,) = _out_proj(a1, w_out1, xp1, tm=OUT_PROJ_TM, tn=OUT_PROJ_TN)

    return (yp.reshape(bp, s, d), ys.reshape(bs, t, d), new_k_p, new_v_p, new_k_s, new_v_s,
            new_pool_p, new_pool_s, v_last.reshape(1, bp, CHUNK_C, d), v_s.reshape(1, bs, t, d))
```

```python
import functools
import math

import jax
import jax.numpy as jnp
from jax import lax
from jax.experimental import pallas as pl
from jax.experimental.pallas import tpu as pltpu

F32 = jnp.float32
BF16 = jnp.bfloat16

LANES = 128
HEAD_DIM = 128
N_HEADS = 8
W_A = N_HEADS * HEAD_DIM
POOL_WINDOWS = (2, 4, 8, 16)
G_B = 256
W_B = G_B * len(POOL_WINDOWS)
W_AB = W_A + W_B
POOL_BUF = max(POOL_WINDOWS) - 1
HALO = 16
CHUNK_C = 128
H_C = 8
EPS = 1e-6
SM_SCALE = 1.0 / math.sqrt(HEAD_DIM)
LOG2E = math.log2(math.e)
GELU_C = math.sqrt(2.0 / math.pi)
VMEM_LIMIT = 56 * 1024 * 1024

IN_PROJ_TM = 1024
IN_PROJ_TN = 1024
IN_PROJ_ROW_CHUNK = 256
OUT_PROJ_TM, OUT_PROJ_TN = 512, 2048
ATTN_HALF_ROWS, ATTN_SUFFIX_BLOCK = 512, 256
DECODE_PAGES_PER_GROUP = 8
AB_MIX_TM = 512
C_GATE_CHUNKS = 4


def _params(semantics):
    return pltpu.CompilerParams(dimension_semantics=semantics, vmem_limit_bytes=VMEM_LIMIT)


def _softplus_2(z2):
    neg_abs = lax.bitcast_convert_type(
        lax.bitcast_convert_type(z2, jnp.uint32) | jnp.uint32(0x80000000), F32)
    return jnp.maximum(z2, 0.0) + jnp.log(1.0 + jnp.exp2(neg_abs)) * LOG2E


def _silu(g):
    return g * jax.nn.sigmoid(g)


def _gelu_tanh(x):
    return 0.5 * x * (1.0 + jnp.tanh(GELU_C * (x + 0.044715 * (x * x * x))))


def _rms_rows(xf, gain):
    ms = jnp.mean(xf * xf, axis=-1, keepdims=True)
    return xf * lax.rsqrt(ms + EPS) * gain


def _in_proj_kernel(x_ref, gain_ref, w_ref, *refs, norm_blocks, gelu_blocks, n_out, cast_w):
    if norm_blocks:
        colgain_ref, refs = refs[0], refs[1:]
    out_refs, refs = refs[:n_out], refs[n_out:]
    if cast_w:
        wb_ref, refs = refs[0], refs[1:]
    (h_ref,) = refs
    j = pl.program_id(1)

    @pl.when(j == 0)
    def _():
        h_ref[...] = _rms_rows(x_ref[...], gain_ref[...]).astype(BF16)

    if cast_w:
        w = w_ref[...].astype(BF16)
        wb_ref[...] = w
    else:
        w = w_ref[...]
    tm = h_ref.shape[0]
    step = min(tm, IN_PROJ_ROW_CHUNK)
    for r0 in range(0, tm, step):
        rows = slice(r0, r0 + step)
        z = jnp.dot(h_ref[rows, :], w, preferred_element_type=F32)
        if norm_blocks:
            is_norm = j < norm_blocks
            cols = []
            for c in range(z.shape[1] // HEAD_DIM):
                zc = z[:, c * HEAD_DIM:(c + 1) * HEAD_DIM]
                rs = lax.rsqrt(jnp.mean(zc * zc, axis=-1, keepdims=True) + EPS)
                cols.append(zc * jnp.where(is_norm, rs, 1.0))
            z = jnp.concatenate(cols, axis=1) * colgain_ref[...]
        if gelu_blocks:
            z = jnp.where(j < gelu_blocks, _gelu_tanh(z), z)
        for out_ref in out_refs:
            out_ref[rows, :] = z.astype(out_ref.dtype)


def _in_proj(x, gain, w, *, tm, tn, outs, colgain=None, norm_cols=0, gelu_cols=0, cast_w=False,
             name):
    m, d = x.shape
    n = w.shape[1]
    assert not cast_w or m == tm
    in_specs = [
        pl.BlockSpec((tm, d), lambda i, j: (i, 0)),
        pl.BlockSpec((1, d), lambda i, j: (0, 0)),
        pl.BlockSpec((d, tn), lambda i, j: (0, j)),
    ]
    args = [x, gain.reshape(1, d), w]
    if norm_cols:
        in_specs.append(pl.BlockSpec((1, tn), lambda i, j: (0, j)))
        args.append(colgain.reshape(1, n))
    out_shape, out_specs = [], []
    for cols, dtype in outs:
        count = cols // tn
        spare = 0 if cols == n else 1
        out_shape.append(jax.ShapeDtypeStruct((m, (count + spare) * tn), dtype))
        out_specs.append(pl.BlockSpec(
            (tm, tn), lambda i, j, last=count + spare - 1: (i, jnp.minimum(j, last))))
    if cast_w:
        out_shape.append(jax.ShapeDtypeStruct((d, n), BF16))
        out_specs.append(pl.BlockSpec((d, tn), lambda i, j: (0, j)))
    return pl.pallas_call(
        functools.partial(_in_proj_kernel, norm_blocks=norm_cols // tn,
                          gelu_blocks=gelu_cols // tn, n_out=len(outs), cast_w=cast_w),
        out_shape=tuple(out_shape),
        grid=(m // tm, n // tn),
        in_specs=in_specs,
        out_specs=tuple(out_specs),
        scratch_shapes=[pltpu.VMEM((tm, d), BF16)],
        compiler_params=_params(("parallel", "arbitrary")),
        name=name,
    )(*args)


def _strict_upper(n):
    r = lax.broadcasted_iota(jnp.int32, (n, n), 0)
    c = lax.broadcasted_iota(jnp.int32, (n, n), 1)
    return jnp.where(r > c, 1.0, 0.0).astype(BF16)


def _sb_attention_kernel(pt_ref, bias_ref, q_ref, k_ref, v_ref, qs_ref, bias_col_ref, ck_ref, cv_ref,
                         o_ref, os_ref, acc_ref, r_ref, kbuf_ref, vbuf_ref, ksem_ref, vsem_ref,
                         accs_ref, rs_ref, dls_ref, dsum_ref, *, th, tk, group):
    h = pl.program_id(0)
    i = pl.program_id(1)
    n_seq, n_pages = pt_ref.shape
    groups_per_seq = n_pages // group
    n_groups = n_seq * groups_per_seq
    groups_per_head = n_groups // pl.num_programs(0)
    first_step = (h == 0) & (i == 0)
    last_step = (h == pl.num_programs(0) - 1) & (i == pl.num_programs(1) - 1)

    def page_copies(cache_ref, buf_ref, sems, n, slot):
        seq = n // groups_per_seq
        pg = n % groups_per_seq
        return [pltpu.make_async_copy(
            cache_ref.at[pt_ref[seq, n_pages - 1 - (pg * group + g)]], buf_ref.at[slot, g],
            sems.at[slot]) for g in range(group)]

    def k_copies(n, slot):
        return page_copies(ck_ref, kbuf_ref, ksem_ref, n, slot)

    def v_copies(n, slot):
        return page_copies(cv_ref, vbuf_ref, vsem_ref, n, slot)

    def decode_second_half(m, v_slot):
        seq = m // groups_per_seq
        fresh = (m % groups_per_seq) == 0
        acc0 = jnp.where(fresh, 0.0, accs_ref[...])
        run0 = jnp.where(fresh, 0.0, rs_ref[...])
        acc, run = _decode_pv(dls_ref[...], dsum_ref[...],
                              [vbuf_ref[v_slot, g] for g in range(group)], acc0, run0)
        accs_ref[...] = acc
        rs_ref[...] = run
        os_ref[seq] = acc

    def decode_fetch(n):
        k_slot = n % 2
        for c in k_copies(n, k_slot) + v_copies(jnp.maximum(n - 1, 0), (n + 2) % 3):
            c.wait()
        nxt = jnp.minimum(n + 1, n_groups - 1)
        for c in k_copies(nxt, 1 - k_slot) + v_copies(nxt, (n + 1) % 3):
            c.start()

    def decode_first_half(n):
        ls2, stacked = _decode_qk(qs_ref[n // groups_per_seq], bias_col_ref[...] * LOG2E,
                                  [kbuf_ref[n % 2, g] for g in range(group)])
        dls_ref[...] = ls2
        dsum_ref[...] = _decode_suffix(stacked)

    def decode_previous(n):
        decode_second_half(jnp.maximum(n - 1, 0), (n + 2) % 3)

    @pl.when(first_step)
    def _():
        for ref in (accs_ref, rs_ref, dls_ref, dsum_ref, os_ref):
            ref[...] = jnp.zeros_like(ref)
        for c in k_copies(0, 0) + v_copies(0, 0) + v_copies(0, 2):
            c.start()

    bias2 = bias_ref[h] * LOG2E
    upper = _strict_upper(tk)
    nsub = th // tk
    half_rows = [slice(half * th, (half + 1) * th) for half in range(2)]
    even = 1 - i % 2
    group_base = h * groups_per_head + (i * (i - 1)) // 2 + (i + 1) // 2

    def qk_stage(half, kt, masked):
        ks = pl.multiple_of(kt * th, th)
        z2 = lax.dot_general(q_ref[half_rows[half], :], k_ref[pl.ds(ks, th), :],
                             (((1,), (1,)), ((), ())), preferred_element_type=F32) + bias2
        sp2 = _softplus_2(z2)
        ls2 = z2 - sp2
        valid = None
        if masked:
            t_pos = lax.broadcasted_iota(jnp.int32, (th, th), 0)
            s_pos = lax.broadcasted_iota(jnp.int32, (th, th), 1)
            valid = s_pos < t_pos
            sp2 = jnp.where(valid, sp2, 0.0)
        blocks = [sp2[:, c * tk:(c + 1) * tk] for c in range(nsub)]
        totals = [jnp.broadcast_to(jnp.sum(b, axis=-1, keepdims=True), (th, LANES)) for b in blocks]
        return ks, ls2, jnp.concatenate(blocks, axis=0).astype(BF16), totals, valid

    def suffix_stage(st):
        return jnp.dot(st[2], upper, preferred_element_type=F32)

    def pv_stage(st, later, run):
        ks, ls2, _, totals, valid = st
        ws = [None] * nsub
        for c in reversed(range(nsub)):
            offs = jnp.concatenate([run] * (tk // LANES), axis=1)
            ws[c] = jnp.exp2(ls2[:, c * tk:(c + 1) * tk] - later[c * th:(c + 1) * th] - offs)
            run = run + totals[c]
        w = jnp.concatenate(ws, axis=1)
        if valid is not None:
            w = jnp.where(valid, w, 0.0)
        return jnp.dot(w.astype(BF16), v_ref[pl.ds(ks, th), :], preferred_element_type=F32), run

    def diagonal(with_decode):
        if with_decode:
            decode_fetch(group_base)
        zero = jnp.zeros((th, LANES), F32)
        st = qk_stage(1, 2 * i + 1, True)
        pv_hi, run1 = pv_stage(st, suffix_stage(st), zero)
        if with_decode:
            decode_previous(group_base)
        st = qk_stage(0, 2 * i, True)
        pv0, run0 = pv_stage(st, suffix_stage(st), zero)
        if with_decode:
            decode_first_half(group_base)
        st = qk_stage(1, 2 * i, False)
        pv_lo, run1 = pv_stage(st, suffix_stage(st), run1)
        acc_ref[half_rows[0], :] = pv0
        acc_ref[half_rows[1], :] = pv_hi + pv_lo
        r_ref[half_rows[0], :] = run0
        r_ref[half_rows[1], :] = run1

    pl.when(even == 1)(functools.partial(diagonal, True))
    pl.when(even == 0)(functools.partial(diagonal, False))

    def body(it, carry):
        kt = 2 * i - 1 - 2 * it
        n = group_base + even + it
        decode_fetch(n)
        pvs = [None] * 4
        runs = [r_ref[rows, :] for rows in half_rows]
        for t, (half, ktile) in enumerate([(0, kt), (1, kt), (0, kt - 1), (1, kt - 1)]):
            st = qk_stage(half, ktile, False)
            pvs[t], runs[half] = pv_stage(st, suffix_stage(st), runs[half])
            if t == 0:
                decode_previous(n)
            if t == 2:
                decode_first_half(n)
        for half, rows in enumerate(half_rows):
            acc_ref[rows, :] += pvs[half] + pvs[half + 2]
            r_ref[rows, :] = runs[half]
        return carry

    lax.fori_loop(0, i, body, 0)
    o_ref[...] = acc_ref[...]

    @pl.when(last_step)
    def _():
        last = n_groups - 1
        for c in v_copies(last, last % 3):
            c.wait()
        decode_second_half(last, last % 3)
        for c in k_copies(last, n_groups % 2) + v_copies(last, n_groups % 3):
            c.wait()


def _sb_attention(qkv, q_s, cache_k, cache_v, page_table, bias, *, th, tk, group):
    s = qkv.shape[0]
    b = q_s.shape[0]
    page = cache_k.shape[1]
    n_pages = page_table.shape[1]
    tq = 2 * th
    steps = s // tq
    keys = group * page
    assert b * (n_pages // group) == N_HEADS * (steps * (steps - 1) // 2 + (steps + 1) // 2)
    grid_spec = pltpu.PrefetchScalarGridSpec(
        num_scalar_prefetch=1,
        grid=(N_HEADS, steps),
        in_specs=[
            pl.BlockSpec(memory_space=pltpu.SMEM),
            pl.BlockSpec((tq, HEAD_DIM), lambda h, i, pt: (i, h)),
            pl.BlockSpec((s, HEAD_DIM), lambda h, i, pt: (0, N_HEADS + h)),
            pl.BlockSpec((s, HEAD_DIM), lambda h, i, pt: (0, 2 * N_HEADS + h)),
            pl.BlockSpec((b, N_HEADS, HEAD_DIM), lambda h, i, pt: (0, 0, 0)),
            pl.BlockSpec((N_HEADS, 1), lambda h, i, pt: (0, 0)),
            pl.BlockSpec(memory_space=pl.ANY),
            pl.BlockSpec(memory_space=pl.ANY),
        ],
        out_specs=(pl.BlockSpec((tq, HEAD_DIM), lambda h, i, pt: (i, h)),
                   pl.BlockSpec((b, N_HEADS, HEAD_DIM), lambda h, i, pt: (0, 0, 0))),
        scratch_shapes=[
            pltpu.VMEM((tq, HEAD_DIM), F32),
            pltpu.VMEM((tq, LANES), F32),
            pltpu.VMEM((2, group, page, N_HEADS, HEAD_DIM), F32),
            pltpu.VMEM((3, group, page, N_HEADS, HEAD_DIM), F32),
            pltpu.SemaphoreType.DMA((2,)),
            pltpu.SemaphoreType.DMA((3,)),
            pltpu.VMEM((N_HEADS, HEAD_DIM), F32),
            pltpu.VMEM((N_HEADS, LANES), F32),
            pltpu.VMEM((N_HEADS, keys * N_HEADS), F32),
            pltpu.VMEM((keys * N_HEADS // LANES * N_HEADS, 2 * LANES), F32),
        ],
    )
    o_a, o_s = pl.pallas_call(
        functools.partial(_sb_attention_kernel, th=th, tk=tk, group=group),
        out_shape=(jax.ShapeDtypeStruct((s, W_A), F32),
                   jax.ShapeDtypeStruct((b, N_HEADS, HEAD_DIM), F32)),
        grid_spec=grid_spec,
        compiler_params=_params(("arbitrary", "arbitrary")),
        name="sb_attention",
    )(page_table, bias, qkv, qkv, qkv, q_s.reshape(b, N_HEADS, HEAD_DIM).astype(F32),
      bias.reshape(N_HEADS, 1), cache_k, cache_v)
    return o_a, o_s.reshape(b, W_A)


def _suffix_and_total(n):
    r = lax.broadcasted_iota(jnp.int32, (n, 2 * n), 0)
    c = lax.broadcasted_iota(jnp.int32, (n, 2 * n), 1)
    return jnp.where((r > c) | (c >= n), 1.0, 0.0).astype(BF16)


def _own_lanes():
    lane = lax.broadcasted_iota(jnp.int32, (N_HEADS, LANES), 1)
    row = lax.broadcasted_iota(jnp.int32, (N_HEADS, LANES), 0)
    return (lane % N_HEADS) == row


def _newest_first(n_tiles, tiles_per_page):
    return [g * tiles_per_page + j for g in range(n_tiles // tiles_per_page)
            for j in reversed(range(tiles_per_page))]


def _decode_qk(q, bias2, k_pages):
    page = k_pages[0].shape[0]
    flat = page * N_HEADS
    keys_per_tile = LANES // N_HEADS
    lane = lax.broadcasted_iota(jnp.int32, (N_HEADS, LANES), 1)
    row = lax.broadcasted_iota(jnp.int32, (N_HEADS, LANES), 0)
    at_key = [lane == c * N_HEADS + row for c in range(keys_per_tile)]
    tiles = []
    for kp in k_pages:
        cols = jnp.sum(kp * q[None], axis=-1, keepdims=True)
        for j in range(page // keys_per_tile):
            zt = jnp.zeros((N_HEADS, LANES), F32)
            for c in range(keys_per_tile):
                col = jnp.broadcast_to(cols[j * keys_per_tile + c], (N_HEADS, LANES))
                zt = jnp.where(at_key[c], col, zt)
            tiles.append(zt)
    z2 = jnp.concatenate(tiles, axis=1) + bias2
    sp2 = _softplus_2(z2)
    ls2 = z2 - sp2
    own_t = _own_lanes()
    order = _newest_first(z2.shape[1] // LANES, flat // LANES)
    stacked = jnp.concatenate(
        [jnp.where(own_t, sp2[:, t * LANES:(t + 1) * LANES], 0.0) for t in order], axis=0)
    hi = stacked.astype(BF16)
    lo = (stacked - hi.astype(F32)).astype(BF16)
    return ls2, jnp.concatenate([hi, lo], axis=0)


def _decode_suffix(stacked):
    s = jnp.dot(stacked, _suffix_and_total(LANES), preferred_element_type=F32)
    half = s.shape[0] // 2
    return s[:half] + s[half:]


def _decode_pv(ls2, sums, v_pages, acc, run):
    flat = v_pages[0].shape[0] * N_HEADS
    own_t = _own_lanes()
    n_tiles = ls2.shape[1] // LANES
    ws = [None] * n_tiles
    for n, t in enumerate(_newest_first(n_tiles, flat // LANES)):
        rows = slice(n * N_HEADS, (n + 1) * N_HEADS)
        later = sums[rows, :LANES] + run
        ws[t] = jnp.where(own_t, jnp.exp2(ls2[:, t * LANES:(t + 1) * LANES] - later), 0.0)
        run = run + sums[rows, LANES:]
    w = jnp.concatenate(ws, axis=1).astype(BF16)
    vm = jnp.concatenate([vp.reshape(flat, HEAD_DIM).astype(BF16) for vp in v_pages], axis=0)
    return acc + jnp.dot(w, vm, preferred_element_type=F32), run


def _pool_mix_gate(window_sums, p, count_inv, o_a, g, wp_ref, scale_ref, a_ref):
    a_ref[:, :W_A] = (o_a * _silu(g[:, :W_A])).astype(BF16)
    for gi in range(len(POOL_WINDOWS)):
        cols = slice(gi * G_B, (gi + 1) * G_B)
        pooled = window_sums[gi] * count_inv[gi] - p[:, cols]
        mixed = jnp.dot(pooled.astype(BF16), wp_ref[gi], preferred_element_type=F32)
        o_b = mixed * scale_ref[:, cols]
        a_ref[:, W_A + gi * G_B:W_A + (gi + 1) * G_B] = (
            o_b * _silu(g[:, W_A + gi * G_B:W_A + (gi + 1) * G_B])).astype(BF16)


def _ab_mix_prompt_kernel(p_ref, halo_ref, oa_ref, g_ref, wp_ref, scale_ref, a_ref, buf_ref, *, tm):
    i = pl.program_id(0)
    buf_ref[:HALO, :] = jnp.where(i > 0, halo_ref[...], 0.0)
    buf_ref[HALO:, :] = p_ref[...]
    pos = i * tm + lax.broadcasted_iota(jnp.int32, (tm, 1), 0)
    sums, invs = [], []
    for gi, wnd in enumerate(POOL_WINDOWS):
        cols = slice(gi * G_B, (gi + 1) * G_B)
        acc = buf_ref[:, cols]
        shift = 1
        while shift < wnd:
            acc = acc + pltpu.roll(acc, shift, 0)
            shift *= 2
        sums.append(acc[HALO:])
        invs.append(1.0 / jnp.minimum(wnd, pos + 1).astype(F32))
    _pool_mix_gate(sums, p_ref[...], invs, oa_ref[...], g_ref[...], wp_ref, scale_ref, a_ref)


def _ab_mix_prompt(z, o_a, w_pool, scale, *, tm):
    m = z.shape[0]
    halo_blocks = tm // HALO
    p_col = 3 * W_A // W_B
    g_col = (3 * W_A + W_B) // W_AB
    return pl.pallas_call(
        functools.partial(_ab_mix_prompt_kernel, tm=tm),
        out_shape=jax.ShapeDtypeStruct((m, W_AB), BF16),
        grid=(m // tm,),
        in_specs=[
            pl.BlockSpec((tm, W_B), lambda i: (i, p_col)),
            pl.BlockSpec((HALO, W_B), lambda i: (jnp.maximum(i * halo_blocks - 1, 0), p_col)),
            pl.BlockSpec((tm, W_A), lambda i: (i, 0)),
            pl.BlockSpec((tm, W_AB), lambda i: (i, g_col)),
            pl.BlockSpec((len(POOL_WINDOWS), G_B, G_B), lambda i: (0, 0, 0)),
            pl.BlockSpec((1, W_B), lambda i: (0, 0)),
        ],
        out_specs=pl.BlockSpec((tm, W_AB), lambda i: (i, 0)),
        scratch_shapes=[pltpu.VMEM((tm + HALO, W_B), F32)],
        compiler_params=_params(("parallel",)),
        name="ab_mix_prompt",
    )(z, z, o_a, z, w_pool, scale.reshape(1, W_B))


def _ab_mix_sample_kernel(p_ref, prev_ref, oa_ref, g_ref, wp_ref, scale_ref, a_ref):
    p = p_ref[...]
    sums, invs = [], []
    for gi, wnd in enumerate(POOL_WINDOWS):
        cols = slice(gi * G_B, (gi + 1) * G_B)
        acc = p[:, cols]
        for back in range(1, wnd):
            acc = acc + prev_ref[POOL_BUF - back, :, cols]
        sums.append(acc)
        invs.append(1.0 / wnd)
    _pool_mix_gate(sums, p, invs, oa_ref[...], g_ref[...], wp_ref, scale_ref, a_ref)


def _ab_mix_sample(p, prev_t, o_a, g, w_pool, scale):
    m = p.shape[0]
    return pl.pallas_call(
        _ab_mix_sample_kernel,
        out_shape=jax.ShapeDtypeStruct((m, W_AB), BF16),
        name="ab_mix_sample",
        compiler_params=pltpu.CompilerParams(vmem_limit_bytes=VMEM_LIMIT),
    )(p, prev_t, o_a, g, w_pool, scale.reshape(1, W_B))


def _out_proj_kernel(a_ref, w_ref, x_ref, y_ref, *wb_ref):
    w = w_ref[...]
    if wb_ref:
        w = w.astype(BF16)
        wb_ref[0][...] = w
    y_ref[...] = x_ref[...] + jnp.dot(a_ref[...], w, preferred_element_type=F32)


def _out_proj(a, w, x, *, tm, tn, cast_w=False):
    m, kdim = a.shape
    n = w.shape[1]
    assert not cast_w or m == tm
    out_shape = [jax.ShapeDtypeStruct((m, n), F32)]
    out_specs = [pl.BlockSpec((tm, tn), lambda i, j: (i, j))]
    if cast_w:
        out_shape.append(jax.ShapeDtypeStruct((kdim, n), BF16))
        out_specs.append(pl.BlockSpec((kdim, tn), lambda i, j: (0, j)))
    return pl.pallas_call(
        _out_proj_kernel,
        out_shape=tuple(out_shape),
        grid=(m // tm, n // tn),
        in_specs=[
            pl.BlockSpec((tm, kdim), lambda i, j: (i, 0)),
            pl.BlockSpec((kdim, tn), lambda i, j: (0, j)),
            pl.BlockSpec((tm, tn), lambda i, j: (i, j)),
        ],
        out_specs=tuple(out_specs),
        compiler_params=_params(("parallel", "parallel")),
        name="out_proj",
    )(a, w, x)


def _c_gate_prompt_kernel(u_ref, vg_ref, g_ref, vgain_ref, ws_ref, bst_ref, a_ref, vrow_ref, *,
                          chunks):
    gc = u_ref.shape[1] // H_C
    r = lax.broadcasted_iota(jnp.int32, (CHUNK_C, CHUNK_C), 0)
    c = lax.broadcasted_iota(jnp.int32, (CHUNK_C, CHUNK_C), 1)
    lower = r >= c
    for ch in range(chunks):
        rows = slice(ch * CHUNK_C, (ch + 1) * CHUNK_C)
        v = _rms_rows(vg_ref[rows, :], vgain_ref[...])
        if ch == chunks - 1:
            vrow_ref[...] = v
        for hd in range(H_C):
            cols = slice(hd * gc, (hd + 1) * gc)
            ws = jnp.where(lower, ws_ref[hd], 0.0).astype(BF16)
            mixed = jnp.dot(ws, v[:, cols].astype(BF16), preferred_element_type=F32)
            mixed = mixed + bst_ref[:, hd:hd + 1]
            a_ref[rows, cols] = (u_ref[rows, cols] * mixed * _silu(g_ref[rows, cols])).astype(BF16)


def _c_gate_prompt(z, v_gain, w_s, b_s, *, chunks):
    m = z.shape[0]
    wc = z.shape[1] // 3
    tm = chunks * CHUNK_C
    return pl.pallas_call(
        functools.partial(_c_gate_prompt_kernel, chunks=chunks),
        out_shape=(jax.ShapeDtypeStruct((m, wc), BF16),
                   jax.ShapeDtypeStruct((CHUNK_C, wc), F32)),
        grid=(m // tm,),
        in_specs=[
            pl.BlockSpec((tm, wc), lambda i: (i, 0)),
            pl.BlockSpec((tm, wc), lambda i: (i, 1)),
            pl.BlockSpec((tm, wc), lambda i: (i, 2)),
            pl.BlockSpec((1, wc), lambda i: (0, 0)),
            pl.BlockSpec((H_C, CHUNK_C, CHUNK_C), lambda i: (0, 0, 0)),
            pl.BlockSpec((CHUNK_C, H_C), lambda i: (0, 0)),
        ],
        out_specs=(pl.BlockSpec((tm, wc), lambda i: (i, 0)),
                   pl.BlockSpec((CHUNK_C, wc), lambda i: (0, 0))),
        compiler_params=_params(("arbitrary",)),
        name="c_gate_prompt",
    )(z, z, z, v_gain.reshape(1, wc), w_s, b_s.T)


def _c_gate_sample_kernel(u_ref, vg_ref, g_ref, vgain_ref, w00_ref, b0_ref, a_ref, vrow_ref):
    v = _rms_rows(vg_ref[...], vgain_ref[...])
    vrow_ref[...] = v
    mixed = w00_ref[...] * v + b0_ref[...]
    a_ref[...] = (u_ref[...] * mixed * _silu(g_ref[...])).astype(BF16)


def _c_gate_sample(u, vg, g, v_gain, w_s, b_s):
    m, wc = u.shape
    gc = wc // H_C
    w00 = jnp.repeat(w_s[:, 0, 0], gc).reshape(1, wc)
    b0 = jnp.repeat(b_s[:, 0], gc).reshape(1, wc)
    return pl.pallas_call(
        _c_gate_sample_kernel,
        out_shape=(jax.ShapeDtypeStruct((m, wc), BF16), jax.ShapeDtypeStruct((m, wc), F32)),
        name="c_gate_sample",
    )(u, vg, g, v_gain.reshape(1, wc), w00, b0)


def kernel(x_prompt, x_sample, cache_k, cache_v, state_pool, page_table, norm_gain, w_in_ab, q_gain,
           k_gain, sb_bias, w_pool, pool_scale, w_out_ab, w_in_c, v_gain, w_spatial, b_spatial,
           w_out_c):
    bp, s, d = x_prompt.shape
    bs, t, _ = x_sample.shape
    assert bp == 1 and t == 1
    xp = x_prompt.reshape(s, d)
    xs = x_sample.reshape(bs, d)

    w_pool0 = w_pool[0].astype(BF16)
    colgain = jnp.concatenate([jnp.tile(q_gain[0] * (SM_SCALE * LOG2E), N_HEADS),
                               jnp.tile(k_gain[0], N_HEADS),
                               jnp.ones((W_A + W_B + W_AB,), F32)])
    n0 = 3 * W_A + W_B + W_AB
    ab_in = functools.partial(_in_proj, gain=norm_gain[0], tn=IN_PROJ_TN,
                              outs=((n0, F32), (3 * W_A, BF16)), colgain=colgain,
                              norm_cols=2 * W_A, name="ab_in_proj")
    c_in = functools.partial(_in_proj, gain=norm_gain[1], tn=IN_PROJ_TN, outs=((3 * d, F32),),
                             gelu_cols=2 * d, name="c_in_proj")

    z0s, qkv_s, w_in0 = ab_in(xs, w=w_in_ab[0], tm=bs, cast_w=True)
    z0, qkv = ab_in(xp, w=w_in0, tm=IN_PROJ_TM)
    o_a, oa_s = _sb_attention(qkv, qkv_s[:, :W_A], cache_k[0], cache_v[0], page_table, sb_bias[0],
                              th=ATTN_HALF_ROWS, tk=ATTN_SUFFIX_BLOCK, group=DECODE_PAGES_PER_GROUP)

    ps = z0s[:, 3 * W_A:3 * W_A + W_B]
    prev_t = jnp.swapaxes(state_pool[0], 0, 1)
    a_s = _ab_mix_sample(ps, prev_t, oa_s, z0s[:, 3 * W_A + W_B:], w_pool0, pool_scale[0])
    xs1, w_out0 = _out_proj(a_s, w_out_ab[0], xs, tm=bs, tn=OUT_PROJ_TN, cast_w=True)
    a = _ab_mix_prompt(z0, o_a, w_pool0, pool_scale[0], tm=AB_MIX_TM)
    (xp1,) = _out_proj(a, w_out0, xp, tm=OUT_PROJ_TM, tn=OUT_PROJ_TN)
    new_k_p = z0[:, W_A:2 * W_A].reshape(1, bp, s, N_HEADS, HEAD_DIM)
    new_v_p = z0[:, 2 * W_A:3 * W_A].reshape(1, bp, s, N_HEADS, HEAD_DIM)
    new_pool_p = z0[s - POOL_BUF:, 3 * W_A:3 * W_A + W_B].reshape(1, bp, POOL_BUF, W_B)
    new_k_s = z0s[:, W_A:2 * W_A].reshape(1, bs, t, N_HEADS, HEAD_DIM)
    new_v_s = z0s[:, 2 * W_A:3 * W_A].reshape(1, bs, t, N_HEADS, HEAD_DIM)
    new_pool_s = jnp.concatenate([state_pool[0][:, 1:], ps[:, None, :]], axis=1)[None]

    z1s, w_in1 = c_in(xs1, w=w_in_c[0], tm=bs, cast_w=True)
    (z1,) = c_in(xp1, w=w_in1, tm=IN_PROJ_TM)
    a1s, v_s = _c_gate_sample(z1s[:, :d], z1s[:, d:2 * d], z1s[:, 2 * d:], v_gain[0], w_spatial[0],
                              b_spatial[0])
    ys, w_out1 = _out_proj(a1s, w_out_c[0], xs1, tm=bs, tn=OUT_PROJ_TN, cast_w=True)
    a1, v_last = _c_gate_prompt(z1, v_gain[0], w_spatial[0], b_spatial[0], chunks=C_GATE_CHUNKS)
    (yp,) = _out_proj(a1, w_out1, xp1, tm=OUT_PROJ_TM, tn=OUT_PROJ_TN)

    return (yp.reshape(bp, s, d), ys.reshape(bs, t, d), new_k_p, new_v_p, new_k_s, new_v_s,
            new_pool_p, new_pool_s, v_last.reshape(1, bp, CHUNK_C, d), v_s.reshape(1, bs, t, d))
```

```python
import functools
import math

import jax
import jax.numpy as jnp
from jax import lax
from jax.experimental import pallas as pl
from jax.experimental.pallas import tpu as pltpu

F32 = jnp.float32
BF16 = jnp.bfloat16

LANES = 128
HEAD_DIM = 128
N_HEADS = 8
W_A = N_HEADS * HEAD_DIM
POOL_WINDOWS = (2, 4, 8, 16)
G_B = 256
W_B = G_B * len(POOL_WINDOWS)
W_AB = W_A + W_B
POOL_BUF = max(POOL_WINDOWS) - 1
HALO = 16
CHUNK_C = 128
H_C = 8
EPS = 1e-6
SM_SCALE = 1.0 / math.sqrt(HEAD_DIM)
LOG2E = math.log2(math.e)
GELU_C = math.sqrt(2.0 / math.pi)
VMEM_LIMIT = 56 * 1024 * 1024

IN_PROJ_TM = 1024
IN_PROJ_TN = 1024
IN_PROJ_ROW_CHUNK = 256
OUT_PROJ_TM, OUT_PROJ_TN = 512, 2048
ATTN_HALF_ROWS, ATTN_SUFFIX_BLOCK = 512, 256
DECODE_PAGES_PER_GROUP = 8
AB_MIX_TM = 512
C_GATE_CHUNKS = 4


def _params(semantics):
    return pltpu.CompilerParams(dimension_semantics=semantics, vmem_limit_bytes=VMEM_LIMIT)


def _softplus_2(z2):
    neg_abs = lax.bitcast_convert_type(
        lax.bitcast_convert_type(z2, jnp.uint32) | jnp.uint32(0x80000000), F32)
    return jnp.maximum(z2, 0.0) + jnp.log(1.0 + jnp.exp2(neg_abs)) * LOG2E


def _silu(g):
    return g * jax.nn.sigmoid(g)


def _gelu_tanh(x):
    return 0.5 * x * (1.0 + jnp.tanh(GELU_C * (x + 0.044715 * (x * x * x))))


def _rms_rows(xf, gain):
    ms = jnp.mean(xf * xf, axis=-1, keepdims=True)
    return xf * lax.rsqrt(ms + EPS) * gain


def _in_proj_kernel(x_ref, gain_ref, w_ref, *refs, norm_blocks, gelu_blocks, n_out, cast_w):
    if norm_blocks:
        colgain_ref, refs = refs[0], refs[1:]
    out_refs, refs = refs[:n_out], refs[n_out:]
    if cast_w:
        wb_ref, refs = refs[0], refs[1:]
    (h_ref,) = refs
    j = pl.program_id(1)

    @pl.when(j == 0)
    def _():
        h_ref[...] = _rms_rows(x_ref[...], gain_ref[...]).astype(BF16)

    if cast_w:
        w = w_ref[...].astype(BF16)
        wb_ref[...] = w
    else:
        w = w_ref[...]
    tm = h_ref.shape[0]
    step = min(tm, IN_PROJ_ROW_CHUNK)
    for r0 in range(0, tm, step):
        rows = slice(r0, r0 + step)
        z = jnp.dot(h_ref[rows, :], w, preferred_element_type=F32)
        if norm_blocks:
            is_norm = j < norm_blocks
            cols = []
            for c in range(z.shape[1] // HEAD_DIM):
                zc = z[:, c * HEAD_DIM:(c + 1) * HEAD_DIM]
                rs = lax.rsqrt(jnp.mean(zc * zc, axis=-1, keepdims=True) + EPS)
                cols.append(zc * jnp.where(is_norm, rs, 1.0))
            z = jnp.concatenate(cols, axis=1) * colgain_ref[...]
        if gelu_blocks:
            z = jnp.where(j < gelu_blocks, _gelu_tanh(z), z)
        for out_ref in out_refs:
            out_ref[rows, :] = z.astype(out_ref.dtype)


def _in_proj(x, gain, w, *, tm, tn, outs, colgain=None, norm_cols=0, gelu_cols=0, cast_w=False,
             name):
    m, d = x.shape
    n = w.shape[1]
    assert not cast_w or m == tm
    in_specs = [
        pl.BlockSpec((tm, d), lambda i, j: (i, 0)),
        pl.BlockSpec((1, d), lambda i, j: (0, 0)),
        pl.BlockSpec((d, tn), lambda i, j: (0, j)),
    ]
    args = [x, gain.reshape(1, d), w]
    if norm_cols:
        in_specs.append(pl.BlockSpec((1, tn), lambda i, j: (0, j)))
        args.append(colgain.reshape(1, n))
    out_shape, out_specs = [], []
    for cols, dtype in outs:
        count = cols // tn
        spare = 0 if cols == n else 1
        out_shape.append(jax.ShapeDtypeStruct((m, (count + spare) * tn), dtype))
        out_specs.append(pl.BlockSpec(
            (tm, tn), lambda i, j, last=count + spare - 1: (i, jnp.minimum(j, last))))
    if cast_w:
        out_shape.append(jax.ShapeDtypeStruct((d, n), BF16))
        out_specs.append(pl.BlockSpec((d, tn), lambda i, j: (0, j)))
    return pl.pallas_call(
        functools.partial(_in_proj_kernel, norm_blocks=norm_cols // tn,
                          gelu_blocks=gelu_cols // tn, n_out=len(outs), cast_w=cast_w),
        out_shape=tuple(out_shape),
        grid=(m // tm, n // tn),
        in_specs=in_specs,
        out_specs=tuple(out_specs),
        scratch_shapes=[pltpu.VMEM((tm, d), BF16)],
        compiler_params=_params(("parallel", "arbitrary")),
        name=name,
    )(*args)


def _strict_upper(n):
    r = lax.broadcasted_iota(jnp.int32, (n, n), 0)
    c = lax.broadcasted_iota(jnp.int32, (n, n), 1)
    return jnp.where(r > c, 1.0, 0.0).astype(BF16)


def _sb_attention_kernel(pt_ref, bias_ref, q_ref, k_ref, v_ref, qs_ref, bias_col_ref, ck_ref, cv_ref,
                         o_ref, os_ref, acc_ref, r_ref, kbuf_ref, vbuf_ref, ksem_ref, vsem_ref,
                         accs_ref, rs_ref, dls_ref, dsum_ref, *, th, tk, group):
    h = pl.program_id(0)
    i = pl.program_id(1)
    n_seq, n_pages = pt_ref.shape
    groups_per_seq = n_pages // group
    n_groups = n_seq * groups_per_seq
    groups_per_head = n_groups // pl.num_programs(0)
    first_step = (h == 0) & (i == 0)
    last_step = (h == pl.num_programs(0) - 1) & (i == pl.num_programs(1) - 1)

    def page_copies(cache_ref, buf_ref, sems, n, slot):
        seq = n // groups_per_seq
        pg = n % groups_per_seq
        return [pltpu.make_async_copy(
            cache_ref.at[pt_ref[seq, n_pages - 1 - (pg * group + g)]], buf_ref.at[slot, g],
            sems.at[slot]) for g in range(group)]

    def k_copies(n, slot):
        return page_copies(ck_ref, kbuf_ref, ksem_ref, n, slot)

    def v_copies(n, slot):
        return page_copies(cv_ref, vbuf_ref, vsem_ref, n, slot)

    def decode_second_half(m, v_slot):
        seq = m // groups_per_seq
        fresh = (m % groups_per_seq) == 0
        acc0 = jnp.where(fresh, 0.0, accs_ref[...])
        run0 = jnp.where(fresh, 0.0, rs_ref[...])
        acc, run = _decode_pv(dls_ref[...], dsum_ref[...],
                              [vbuf_ref[v_slot, g] for g in range(group)], acc0, run0)
        accs_ref[...] = acc
        rs_ref[...] = run
        os_ref[seq] = acc

    def decode_fetch(n):
        k_slot = n % 2
        for c in k_copies(n, k_slot) + v_copies(jnp.maximum(n - 1, 0), (n + 2) % 3):
            c.wait()
        nxt = jnp.minimum(n + 1, n_groups - 1)
        for c in k_copies(nxt, 1 - k_slot) + v_copies(nxt, (n + 1) % 3):
            c.start()

    def decode_first_half(n):
        ls2, stacked = _decode_qk(qs_ref[n // groups_per_seq], bias_col_ref[...] * LOG2E,
                                  [kbuf_ref[n % 2, g] for g in range(group)])
        dls_ref[...] = ls2
        dsum_ref[...] = _decode_suffix(stacked)

    def decode_previous(n):
        decode_second_half(jnp.maximum(n - 1, 0), (n + 2) % 3)

    @pl.when(first_step)
    def _():
        for ref in (accs_ref, rs_ref, dls_ref, dsum_ref, os_ref):
            ref[...] = jnp.zeros_like(ref)
        for c in k_copies(0, 0) + v_copies(0, 0) + v_copies(0, 2):
            c.start()

    bias2 = bias_ref[h] * LOG2E
    upper = _strict_upper(tk)
    nsub = th // tk
    half_rows = [slice(half * th, (half + 1) * th) for half in range(2)]
    even = 1 - i % 2
    group_base = h * groups_per_head + (i * (i - 1)) // 2 + (i + 1) // 2

    def qk_stage(half, kt, masked):
        ks = pl.multiple_of(kt * th, th)
        z2 = lax.dot_general(q_ref[half_rows[half], :], k_ref[pl.ds(ks, th), :],
                             (((1,), (1,)), ((), ())), preferred_element_type=F32) + bias2
        sp2 = _softplus_2(z2)
        ls2 = z2 - sp2
        valid = None
        if masked:
            t_pos = lax.broadcasted_iota(jnp.int32, (th, th), 0)
            s_pos = lax.broadcasted_iota(jnp.int32, (th, th), 1)
            valid = s_pos < t_pos
            sp2 = jnp.where(valid, sp2, 0.0)
        blocks = [sp2[:, c * tk:(c + 1) * tk] for c in range(nsub)]
        totals = [jnp.broadcast_to(jnp.sum(b, axis=-1, keepdims=True), (th, LANES)) for b in blocks]
        return ks, ls2, jnp.concatenate(blocks, axis=0).astype(BF16), totals, valid

    def suffix_stage(st):
        return jnp.dot(st[2], upper, preferred_element_type=F32)

    def pv_stage(st, later, run):
        ks, ls2, _, totals, valid = st
        ws = [None] * nsub
        for c in reversed(range(nsub)):
            offs = jnp.concatenate([run] * (tk // LANES), axis=1)
            ws[c] = jnp.exp2(ls2[:, c * tk:(c + 1) * tk] - later[c * th:(c + 1) * th] - offs)
            run = run + totals[c]
        w = jnp.concatenate(ws, axis=1)
        if valid is not None:
            w = jnp.where(valid, w, 0.0)
        return jnp.dot(w.astype(BF16), v_ref[pl.ds(ks, th), :], preferred_element_type=F32), run

    def diagonal(with_decode):
        if with_decode:
            decode_fetch(group_base)
        zero = jnp.zeros((th, LANES), F32)
        st = qk_stage(1, 2 * i + 1, True)
        pv_hi, run1 = pv_stage(st, suffix_stage(st), zero)
        st = qk_stage(0, 2 * i, True)
        pv0, run0 = pv_stage(st, suffix_stage(st), zero)
        if with_decode:
            decode_previous(group_base)
        st = qk_stage(1, 2 * i, False)
        pv_lo, run1 = pv_stage(st, suffix_stage(st), run1)
        if with_decode:
            decode_first_half(group_base)
        acc_ref[half_rows[0], :] = pv0
        acc_ref[half_rows[1], :] = pv_hi + pv_lo
        r_ref[half_rows[0], :] = run0
        r_ref[half_rows[1], :] = run1

    pl.when(even == 1)(functools.partial(diagonal, True))
    pl.when(even == 0)(functools.partial(diagonal, False))

    def body(it, carry):
        kt = 2 * i - 1 - 2 * it
        n = group_base + even + it
        decode_fetch(n)
        pvs = [None] * 4
        runs = [r_ref[rows, :] for rows in half_rows]
        for t, (half, ktile) in enumerate([(0, kt), (1, kt), (0, kt - 1), (1, kt - 1)]):
            st = qk_stage(half, ktile, False)
            pvs[t], runs[half] = pv_stage(st, suffix_stage(st), runs[half])
            if t == 2:
                decode_previous(n)
            if t == 3:
                decode_first_half(n)
        for half, rows in enumerate(half_rows):
            acc_ref[rows, :] += pvs[half] + pvs[half + 2]
            r_ref[rows, :] = runs[half]
        return carry

    lax.fori_loop(0, i, body, 0)
    o_ref[...] = acc_ref[...]

    @pl.when(last_step)
    def _():
        last = n_groups - 1
        for c in v_copies(last, last % 3):
            c.wait()
        decode_second_half(last, last % 3)
        for c in k_copies(last, n_groups % 2) + v_copies(last, n_groups % 3):
            c.wait()


def _sb_attention(qkv, q_s, cache_k, cache_v, page_table, bias, *, th, tk, group):
    s = qkv.shape[0]
    b = q_s.shape[0]
    page = cache_k.shape[1]
    n_pages = page_table.shape[1]
    tq = 2 * th
    steps = s // tq
    keys = group * page
    assert b * (n_pages // group) == N_HEADS * (steps * (steps - 1) // 2 + (steps + 1) // 2)
    grid_spec = pltpu.PrefetchScalarGridSpec(
        num_scalar_prefetch=1,
        grid=(N_HEADS, steps),
        in_specs=[
            pl.BlockSpec(memory_space=pltpu.SMEM),
            pl.BlockSpec((tq, HEAD_DIM), lambda h, i, pt: (i, h)),
            pl.BlockSpec((s, HEAD_DIM), lambda h, i, pt: (0, N_HEADS + h)),
            pl.BlockSpec((s, HEAD_DIM), lambda h, i, pt: (0, 2 * N_HEADS + h)),
            pl.BlockSpec((b, N_HEADS, HEAD_DIM), lambda h, i, pt: (0, 0, 0)),
            pl.BlockSpec((N_HEADS, 1), lambda h, i, pt: (0, 0)),
            pl.BlockSpec(memory_space=pl.ANY),
            pl.BlockSpec(memory_space=pl.ANY),
        ],
        out_specs=(pl.BlockSpec((tq, HEAD_DIM), lambda h, i, pt: (i, h)),
                   pl.BlockSpec((b, N_HEADS, HEAD_DIM), lambda h, i, pt: (0, 0, 0))),
        scratch_shapes=[
            pltpu.VMEM((tq, HEAD_DIM), F32),
            pltpu.VMEM((tq, LANES), F32),
            pltpu.VMEM((2, group, page, N_HEADS, HEAD_DIM), F32),
            pltpu.VMEM((3, group, page, N_HEADS, HEAD_DIM), F32),
            pltpu.SemaphoreType.DMA((2,)),
            pltpu.SemaphoreType.DMA((3,)),
            pltpu.VMEM((N_HEADS, HEAD_DIM), F32),
            pltpu.VMEM((N_HEADS, LANES), F32),
            pltpu.VMEM((N_HEADS, keys * N_HEADS), F32),
            pltpu.VMEM((keys * N_HEADS // LANES * N_HEADS, 2 * LANES), F32),
        ],
    )
    o_a, o_s = pl.pallas_call(
        functools.partial(_sb_attention_kernel, th=th, tk=tk, group=group),
        out_shape=(jax.ShapeDtypeStruct((s, W_A), F32),
                   jax.ShapeDtypeStruct((b, N_HEADS, HEAD_DIM), F32)),
        grid_spec=grid_spec,
        compiler_params=_params(("arbitrary", "arbitrary")),
        name="sb_attention",
    )(page_table, bias, qkv, qkv, qkv, q_s.reshape(b, N_HEADS, HEAD_DIM).astype(F32),
      bias.reshape(N_HEADS, 1), cache_k, cache_v)
    return o_a, o_s.reshape(b, W_A)


def _suffix_and_total(n):
    r = lax.broadcasted_iota(jnp.int32, (n, 2 * n), 0)
    c = lax.broadcasted_iota(jnp.int32, (n, 2 * n), 1)
    return jnp.where((r > c) | (c >= n), 1.0, 0.0).astype(BF16)


def _own_lanes():
    lane = lax.broadcasted_iota(jnp.int32, (N_HEADS, LANES), 1)
    row = lax.broadcasted_iota(jnp.int32, (N_HEADS, LANES), 0)
    return (lane % N_HEADS) == row


def _newest_first(n_tiles, tiles_per_page):
    return [g * tiles_per_page + j for g in range(n_tiles // tiles_per_page)
            for j in reversed(range(tiles_per_page))]


def _decode_qk(q, bias2, k_pages):
    page = k_pages[0].shape[0]
    flat = page * N_HEADS
    keys_per_tile = LANES // N_HEADS
    lane = lax.broadcasted_iota(jnp.int32, (N_HEADS, LANES), 1)
    row = lax.broadcasted_iota(jnp.int32, (N_HEADS, LANES), 0)
    at_key = [lane == c * N_HEADS + row for c in range(keys_per_tile)]
    tiles = []
    for kp in k_pages:
        cols = jnp.sum(kp * q[None], axis=-1, keepdims=True)
        for j in range(page // keys_per_tile):
            zt = jnp.zeros((N_HEADS, LANES), F32)
            for c in range(keys_per_tile):
                col = jnp.broadcast_to(cols[j * keys_per_tile + c], (N_HEADS, LANES))
                zt = jnp.where(at_key[c], col, zt)
            tiles.append(zt)
    z2 = jnp.concatenate(tiles, axis=1) + bias2
    sp2 = _softplus_2(z2)
    ls2 = z2 - sp2
    own_t = _own_lanes()
    order = _newest_first(z2.shape[1] // LANES, flat // LANES)
    stacked = jnp.concatenate(
        [jnp.where(own_t, sp2[:, t * LANES:(t + 1) * LANES], 0.0) for t in order], axis=0)
    hi = stacked.astype(BF16)
    lo = (stacked - hi.astype(F32)).astype(BF16)
    return ls2, jnp.concatenate([hi, lo], axis=0)


def _decode_suffix(stacked):
    s = jnp.dot(stacked, _suffix_and_total(LANES), preferred_element_type=F32)
    half = s.shape[0] // 2
    return s[:half] + s[half:]


def _decode_pv(ls2, sums, v_pages, acc, run):
    flat = v_pages[0].shape[0] * N_HEADS
    own_t = _own_lanes()
    n_tiles = ls2.shape[1] // LANES
    ws = [None] * n_tiles
    for n, t in enumerate(_newest_first(n_tiles, flat // LANES)):
        rows = slice(n * N_HEADS, (n + 1) * N_HEADS)
        later = sums[rows, :LANES] + run
        ws[t] = jnp.where(own_t, jnp.exp2(ls2[:, t * LANES:(t + 1) * LANES] - later), 0.0)
        run = run + sums[rows, LANES:]
    w = jnp.concatenate(ws, axis=1).astype(BF16)
    vm = jnp.concatenate([vp.reshape(flat, HEAD_DIM).astype(BF16) for vp in v_pages], axis=0)
    return acc + jnp.dot(w, vm, preferred_element_type=F32), run


def _pool_mix_gate(window_sums, p, count_inv, o_a, g, wp_ref, scale_ref, a_ref):
    a_ref[:, :W_A] = (o_a * _silu(g[:, :W_A])).astype(BF16)
    for gi in range(len(POOL_WINDOWS)):
        cols = slice(gi * G_B, (gi + 1) * G_B)
        pooled = window_sums[gi] * count_inv[gi] - p[:, cols]
        mixed = jnp.dot(pooled.astype(BF16), wp_ref[gi], preferred_element_type=F32)
        o_b = mixed * scale_ref[:, cols]
        a_ref[:, W_A + gi * G_B:W_A + (gi + 1) * G_B] = (
            o_b * _silu(g[:, W_A + gi * G_B:W_A + (gi + 1) * G_B])).astype(BF16)


def _ab_mix_prompt_kernel(p_ref, halo_ref, oa_ref, g_ref, wp_ref, scale_ref, a_ref, buf_ref, *, tm):
    i = pl.program_id(0)
    buf_ref[:HALO, :] = jnp.where(i > 0, halo_ref[...], 0.0)
    buf_ref[HALO:, :] = p_ref[...]
    pos = i * tm + lax.broadcasted_iota(jnp.int32, (tm, 1), 0)
    sums, invs = [], []
    for gi, wnd in enumerate(POOL_WINDOWS):
        cols = slice(gi * G_B, (gi + 1) * G_B)
        acc = buf_ref[:, cols]
        shift = 1
        while shift < wnd:
            acc = acc + pltpu.roll(acc, shift, 0)
            shift *= 2
        sums.append(acc[HALO:])
        invs.append(1.0 / jnp.minimum(wnd, pos + 1).astype(F32))
    _pool_mix_gate(sums, p_ref[...], invs, oa_ref[...], g_ref[...], wp_ref, scale_ref, a_ref)


def _ab_mix_prompt(z, o_a, w_pool, scale, *, tm):
    m = z.shape[0]
    halo_blocks = tm // HALO
    p_col = 3 * W_A // W_B
    g_col = (3 * W_A + W_B) // W_AB
    return pl.pallas_call(
        functools.partial(_ab_mix_prompt_kernel, tm=tm),
        out_shape=jax.ShapeDtypeStruct((m, W_AB), BF16),
        grid=(m // tm,),
        in_specs=[
            pl.BlockSpec((tm, W_B), lambda i: (i, p_col)),
            pl.BlockSpec((HALO, W_B), lambda i: (jnp.maximum(i * halo_blocks - 1, 0), p_col)),
            pl.BlockSpec((tm, W_A), lambda i: (i, 0)),
            pl.BlockSpec((tm, W_AB), lambda i: (i, g_col)),
            pl.BlockSpec((len(POOL_WINDOWS), G_B, G_B), lambda i: (0, 0, 0)),
            pl.BlockSpec((1, W_B), lambda i: (0, 0)),
        ],
        out_specs=pl.BlockSpec((tm, W_AB), lambda i: (i, 0)),
        scratch_shapes=[pltpu.VMEM((tm + HALO, W_B), F32)],
        compiler_params=_params(("parallel",)),
        name="ab_mix_prompt",
    )(z, z, o_a, z, w_pool, scale.reshape(1, W_B))


def _ab_mix_sample_kernel(p_ref, prev_ref, oa_ref, g_ref, wp_ref, scale_ref, a_ref):
    p = p_ref[...]
    sums, invs = [], []
    for gi, wnd in enumerate(POOL_WINDOWS):
        cols = slice(gi * G_B, (gi + 1) * G_B)
        acc = p[:, cols]
        for back in range(1, wnd):
            acc = acc + prev_ref[POOL_BUF - back, :, cols]
        sums.append(acc)
        invs.append(1.0 / wnd)
    _pool_mix_gate(sums, p, invs, oa_ref[...], g_ref[...], wp_ref, scale_ref, a_ref)


def _ab_mix_sample(p, prev_t, o_a, g, w_pool, scale):
    m = p.shape[0]
    return pl.pallas_call(
        _ab_mix_sample_kernel,
        out_shape=jax.ShapeDtypeStruct((m, W_AB), BF16),
        name="ab_mix_sample",
        compiler_params=pltpu.CompilerParams(vmem_limit_bytes=VMEM_LIMIT),
    )(p, prev_t, o_a, g, w_pool, scale.reshape(1, W_B))


def _out_proj_kernel(a_ref, w_ref, x_ref, y_ref, *wb_ref):
    w = w_ref[...]
    if wb_ref:
        w = w.astype(BF16)
        wb_ref[0][...] = w
    y_ref[...] = x_ref[...] + jnp.dot(a_ref[...], w, preferred_element_type=F32)


def _out_proj(a, w, x, *, tm, tn, cast_w=False):
    m, kdim = a.shape
    n = w.shape[1]
    assert not cast_w or m == tm
    out_shape = [jax.ShapeDtypeStruct((m, n), F32)]
    out_specs = [pl.BlockSpec((tm, tn), lambda i, j: (i, j))]
    if cast_w:
        out_shape.append(jax.ShapeDtypeStruct((kdim, n), BF16))
        out_specs.append(pl.BlockSpec((kdim, tn), lambda i, j: (0, j)))
    return pl.pallas_call(
        _out_proj_kernel,
        out_shape=tuple(out_shape),
        grid=(m // tm, n // tn),
        in_specs=[
            pl.BlockSpec((tm, kdim), lambda i, j: (i, 0)),
            pl.BlockSpec((kdim, tn), lambda i, j: (0, j)),
            pl.BlockSpec((tm, tn), lambda i, j: (i, j)),
        ],
        out_specs=tuple(out_specs),
        compiler_params=_params(("parallel", "parallel")),
        name="out_proj",
    )(a, w, x)


def _c_gate_prompt_kernel(u_ref, vg_ref, g_ref, vgain_ref, ws_ref, bst_ref, a_ref, vrow_ref, *,
                          chunks):
    gc = u_ref.shape[1] // H_C
    r = lax.broadcasted_iota(jnp.int32, (CHUNK_C, CHUNK_C), 0)
    c = lax.broadcasted_iota(jnp.int32, (CHUNK_C, CHUNK_C), 1)
    lower = r >= c
    for ch in range(chunks):
        rows = slice(ch * CHUNK_C, (ch + 1) * CHUNK_C)
        v = _rms_rows(vg_ref[rows, :], vgain_ref[...])
        if ch == chunks - 1:
            vrow_ref[...] = v
        for hd in range(H_C):
            cols = slice(hd * gc, (hd + 1) * gc)
            ws = jnp.where(lower, ws_ref[hd], 0.0).astype(BF16)
            mixed = jnp.dot(ws, v[:, cols].astype(BF16), preferred_element_type=F32)
            mixed = mixed + bst_ref[:, hd:hd + 1]
            a_ref[rows, cols] = (u_ref[rows, cols] * mixed * _silu(g_ref[rows, cols])).astype(BF16)


def _c_gate_prompt(z, v_gain, w_s, b_s, *, chunks):
    m = z.shape[0]
    wc = z.shape[1] // 3
    tm = chunks * CHUNK_C
    return pl.pallas_call(
        functools.partial(_c_gate_prompt_kernel, chunks=chunks),
        out_shape=(jax.ShapeDtypeStruct((m, wc), BF16),
                   jax.ShapeDtypeStruct((CHUNK_C, wc), F32)),
        grid=(m // tm,),
        in_specs=[
            pl.BlockSpec((tm, wc), lambda i: (i, 0)),
            pl.BlockSpec((tm, wc), lambda i: (i, 1)),
            pl.BlockSpec((tm, wc), lambda i: (i, 2)),
            pl.BlockSpec((1, wc), lambda i: (0, 0)),
            pl.BlockSpec((H_C, CHUNK_C, CHUNK_C), lambda i: (0, 0, 0)),
            pl.BlockSpec((CHUNK_C, H_C), lambda i: (0, 0)),
        ],
        out_specs=(pl.BlockSpec((tm, wc), lambda i: (i, 0)),
                   pl.BlockSpec((CHUNK_C, wc), lambda i: (0, 0))),
        compiler_params=_params(("arbitrary",)),
        name="c_gate_prompt",
    )(z, z, z, v_gain.reshape(1, wc), w_s, b_s.T)


def _c_gate_sample_kernel(u_ref, vg_ref, g_ref, vgain_ref, w00_ref, b0_ref, a_ref, vrow_ref):
    v = _rms_rows(vg_ref[...], vgain_ref[...])
    vrow_ref[...] = v
    mixed = w00_ref[...] * v + b0_ref[...]
    a_ref[...] = (u_ref[...] * mixed * _silu(g_ref[...])).astype(BF16)


def _c_gate_sample(u, vg, g, v_gain, w_s, b_s):
    m, wc = u.shape
    gc = wc // H_C
    w00 = jnp.repeat(w_s[:, 0, 0], gc).reshape(1, wc)
    b0 = jnp.repeat(b_s[:, 0], gc).reshape(1, wc)
    return pl.pallas_call(
        _c_gate_sample_kernel,
        out_shape=(jax.ShapeDtypeStruct((m, wc), BF16), jax.ShapeDtypeStruct((m, wc), F32)),
        name="c_gate_sample",
    )(u, vg, g, v_gain.reshape(1, wc), w00, b0)


def kernel(x_prompt, x_sample, cache_k, cache_v, state_pool, page_table, norm_gain, w_in_ab, q_gain,
           k_gain, sb_bias, w_pool, pool_scale, w_out_ab, w_in_c, v_gain, w_spatial, b_spatial,
           w_out_c):
    bp, s, d = x_prompt.shape
    bs, t, _ = x_sample.shape
    assert bp == 1 and t == 1
    xp = x_prompt.reshape(s, d)
    xs = x_sample.reshape(bs, d)

    w_pool0 = w_pool[0].astype(BF16)
    colgain = jnp.concatenate([jnp.tile(q_gain[0] * (SM_SCALE * LOG2E), N_HEADS),
                               jnp.tile(k_gain[0], N_HEADS),
                               jnp.ones((W_A + W_B + W_AB,), F32)])
    n0 = 3 * W_A + W_B + W_AB
    ab_in = functools.partial(_in_proj, gain=norm_gain[0], tn=IN_PROJ_TN,
                              outs=((n0, F32), (3 * W_A, BF16)), colgain=colgain,
                              norm_cols=2 * W_A, name="ab_in_proj")
    c_in = functools.partial(_in_proj, gain=norm_gain[1], tn=IN_PROJ_TN, outs=((3 * d, F32),),
                             gelu_cols=2 * d, name="c_in_proj")

    z0s, qkv_s, w_in0 = ab_in(xs, w=w_in_ab[0], tm=bs, cast_w=True)
    z0, qkv = ab_in(xp, w=w_in0, tm=IN_PROJ_TM)
    o_a, oa_s = _sb_attention(qkv, qkv_s[:, :W_A], cache_k[0], cache_v[0], page_table, sb_bias[0],
                              th=ATTN_HALF_ROWS, tk=ATTN_SUFFIX_BLOCK, group=DECODE_PAGES_PER_GROUP)

    ps = z0s[:, 3 * W_A:3 * W_A + W_B]
    prev_t = jnp.swapaxes(state_pool[0], 0, 1)
    a_s = _ab_mix_sample(ps, prev_t, oa_s, z0s[:, 3 * W_A + W_B:], w_pool0, pool_scale[0])
    xs1, w_out0 = _out_proj(a_s, w_out_ab[0], xs, tm=bs, tn=OUT_PROJ_TN, cast_w=True)
    a = _ab_mix_prompt(z0, o_a, w_pool0, pool_scale[0], tm=AB_MIX_TM)
    (xp1,) = _out_proj(a, w_out0, xp, tm=OUT_PROJ_TM, tn=OUT_PROJ_TN)
    new_k_p = z0[:, W_A:2 * W_A].reshape(1, bp, s, N_HEADS, HEAD_DIM)
    new_v_p = z0[:, 2 * W_A:3 * W_A].reshape(1, bp, s, N_HEADS, HEAD_DIM)
    new_pool_p = z0[s - POOL_BUF:, 3 * W_A:3 * W_A + W_B].reshape(1, bp, POOL_BUF, W_B)
    new_k_s = z0s[:, W_A:2 * W_A].reshape(1, bs, t, N_HEADS, HEAD_DIM)
    new_v_s = z0s[:, 2 * W_A:3 * W_A].reshape(1, bs, t, N_HEADS, HEAD_DIM)
    new_pool_s = jnp.concatenate([state_pool[0][:, 1:], ps[:, None, :]], axis=1)[None]

    z1s, w_in1 = c_in(xs1, w=w_in_c[0], tm=bs, cast_w=True)
    (z1,) = c_in(xp1, w=w_in1, tm=IN_PROJ_TM)
    a1s, v_s = _c_gate_sample(z1s[:, :d], z1s[:, d:2 * d], z1s[:, 2 * d:], v_gain[0], w_spatial[0],
                              b_spatial[0])
    ys, w_out1 = _out_proj(a1s, w_out_c[0], xs1, tm=bs, tn=OUT_PROJ_TN, cast_w=True)
    a1, v_last = _c_gate_prompt(z1, v_gain[0], w_spatial[0], b_spatial[0], chunks=C_GATE_CHUNKS)
    (yp,) = _out_proj(a1, w_out1, xp1, tm=OUT_PROJ_TM, tn=OUT_PROJ_TN)

    return (yp.reshape(bp, s, d), ys.reshape(bs, t, d), new_k_p, new_v_p, new_k_s, new_v_s,
            new_pool_p, new_pool_s, v_last.reshape(1, bp, CHUNK_C, d), v_s.reshape(1, bs, t, d))
```

```python
import functools
import math

import jax
import jax.numpy as jnp
from jax import lax
from jax.experimental import pallas as pl
from jax.experimental.pallas import tpu as pltpu

F32 = jnp.float32
BF16 = jnp.bfloat16

LANES = 128
HEAD_DIM = 128
N_HEADS = 8
W_A = N_HEADS * HEAD_DIM
POOL_WINDOWS = (2, 4, 8, 16)
G_B = 256
W_B = G_B * len(POOL_WINDOWS)
W_AB = W_A + W_B
POOL_BUF = max(POOL_WINDOWS) - 1
HALO = 16
CHUNK_C = 128
H_C = 8
EPS = 1e-6
SM_SCALE = 1.0 / math.sqrt(HEAD_DIM)
LOG2E = math.log2(math.e)
GELU_C = math.sqrt(2.0 / math.pi)
VMEM_LIMIT = 56 * 1024 * 1024

IN_PROJ_TM = 1024
IN_PROJ_TN = 1024
IN_PROJ_ROW_CHUNK = 256
OUT_PROJ_TM, OUT_PROJ_TN = 512, 2048
ATTN_HALF_ROWS, ATTN_SUFFIX_BLOCK = 512, 256
DECODE_PAGES_PER_GROUP = 8
AB_MIX_TM = 512
C_GATE_CHUNKS = 4


def _params(semantics):
    return pltpu.CompilerParams(dimension_semantics=semantics, vmem_limit_bytes=VMEM_LIMIT)


def _softplus_2(z2):
    neg_abs = lax.bitcast_convert_type(
        lax.bitcast_convert_type(z2, jnp.uint32) | jnp.uint32(0x80000000), F32)
    return jnp.maximum(z2, 0.0) + jnp.log(1.0 + jnp.exp2(neg_abs)) * LOG2E


def _silu(g):
    return g * jax.nn.sigmoid(g)


def _gelu_tanh(x):
    return 0.5 * x * (1.0 + jnp.tanh(GELU_C * (x + 0.044715 * (x * x * x))))


def _rms_rows(xf, gain):
    ms = jnp.mean(xf * xf, axis=-1, keepdims=True)
    return xf * lax.rsqrt(ms + EPS) * gain


def _in_proj_kernel(x_ref, gain_ref, w_ref, *refs, norm_blocks, gelu_blocks, n_out, cast_w):
    if norm_blocks:
        colgain_ref, refs = refs[0], refs[1:]
    out_refs, refs = refs[:n_out], refs[n_out:]
    if cast_w:
        wb_ref, refs = refs[0], refs[1:]
    (h_ref,) = refs
    j = pl.program_id(1)

    @pl.when(j == 0)
    def _():
        h_ref[...] = _rms_rows(x_ref[...], gain_ref[...]).astype(BF16)

    if cast_w:
        w = w_ref[...].astype(BF16)
        wb_ref[...] = w
    else:
        w = w_ref[...]
    tm = h_ref.shape[0]
    step = min(tm, IN_PROJ_ROW_CHUNK)
    for r0 in range(0, tm, step):
        rows = slice(r0, r0 + step)
        z = jnp.dot(h_ref[rows, :], w, preferred_element_type=F32)
        if norm_blocks:
            is_norm = j < norm_blocks
            cols = []
            for c in range(z.shape[1] // HEAD_DIM):
                zc = z[:, c * HEAD_DIM:(c + 1) * HEAD_DIM]
                rs = lax.rsqrt(jnp.mean(zc * zc, axis=-1, keepdims=True) + EPS)
                cols.append(zc * jnp.where(is_norm, rs, 1.0))
            z = jnp.concatenate(cols, axis=1) * colgain_ref[...]
        if gelu_blocks:
            z = jnp.where(j < gelu_blocks, _gelu_tanh(z), z)
        for out_ref in out_refs:
            out_ref[rows, :] = z.astype(out_ref.dtype)


def _in_proj(x, gain, w, *, tm, tn, outs, colgain=None, norm_cols=0, gelu_cols=0, cast_w=False,
             name):
    m, d = x.shape
    n = w.shape[1]
    assert not cast_w or m == tm
    in_specs = [
        pl.BlockSpec((tm, d), lambda i, j: (i, 0)),
        pl.BlockSpec((1, d), lambda i, j: (0, 0)),
        pl.BlockSpec((d, tn), lambda i, j: (0, j)),
    ]
    args = [x, gain.reshape(1, d), w]
    if norm_cols:
        in_specs.append(pl.BlockSpec((1, tn), lambda i, j: (0, j)))
        args.append(colgain.reshape(1, n))
    out_shape, out_specs = [], []
    for cols, dtype in outs:
        count = cols // tn
        spare = 0 if cols == n else 1
        out_shape.append(jax.ShapeDtypeStruct((m, (count + spare) * tn), dtype))
        out_specs.append(pl.BlockSpec(
            (tm, tn), lambda i, j, last=count + spare - 1: (i, jnp.minimum(j, last))))
    if cast_w:
        out_shape.append(jax.ShapeDtypeStruct((d, n), BF16))
        out_specs.append(pl.BlockSpec((d, tn), lambda i, j: (0, j)))
    return pl.pallas_call(
        functools.partial(_in_proj_kernel, norm_blocks=norm_cols // tn,
                          gelu_blocks=gelu_cols // tn, n_out=len(outs), cast_w=cast_w),
        out_shape=tuple(out_shape),
        grid=(m // tm, n // tn),
        in_specs=in_specs,
        out_specs=tuple(out_specs),
        scratch_shapes=[pltpu.VMEM((tm, d), BF16)],
        compiler_params=_params(("parallel", "arbitrary")),
        name=name,
    )(*args)


def _strict_upper(n):
    r = lax.broadcasted_iota(jnp.int32, (n, n), 0)
    c = lax.broadcasted_iota(jnp.int32, (n, n), 1)
    return jnp.where(r > c, 1.0, 0.0).astype(BF16)


def _sb_attention_kernel(pt_ref, bias_ref, q_ref, k_ref, v_ref, qs_ref, bias_col_ref, ck_ref, cv_ref,
                         o_ref, os_ref, acc_ref, r_ref, kbuf_ref, vbuf_ref, ksem_ref, vsem_ref,
                         accs_ref, rs_ref, dls_ref, dsum_ref, *, th, tk, group):
    h = pl.program_id(0)
    i = pl.program_id(1)
    n_seq, n_pages = pt_ref.shape
    groups_per_seq = n_pages // group
    n_groups = n_seq * groups_per_seq
    groups_per_head = n_groups // pl.num_programs(0)
    first_step = (h == 0) & (i == 0)
    last_step = (h == pl.num_programs(0) - 1) & (i == pl.num_programs(1) - 1)

    def page_copies(cache_ref, buf_ref, sems, n, slot):
        seq = n // groups_per_seq
        pg = n % groups_per_seq
        return [pltpu.make_async_copy(
            cache_ref.at[pt_ref[seq, n_pages - 1 - (pg * group + g)]], buf_ref.at[slot, g],
            sems.at[slot]) for g in range(group)]

    def k_copies(n, slot):
        return page_copies(ck_ref, kbuf_ref, ksem_ref, n, slot)

    def v_copies(n, slot):
        return page_copies(cv_ref, vbuf_ref, vsem_ref, n, slot)

    def decode_second_half(m, v_slot):
        seq = m // groups_per_seq
        fresh = (m % groups_per_seq) == 0
        acc0 = jnp.where(fresh, 0.0, accs_ref[...])
        run0 = jnp.where(fresh, 0.0, rs_ref[...])
        acc, run = _decode_pv(dls_ref[...], dsum_ref[...],
                              [vbuf_ref[v_slot, g] for g in range(group)], acc0, run0)
        accs_ref[...] = acc
        rs_ref[...] = run
        os_ref[seq] = acc

    def decode_fetch(n):
        k_slot = n % 2
        for c in k_copies(n, k_slot) + v_copies(jnp.maximum(n - 1, 0), (n + 2) % 3):
            c.wait()
        nxt = jnp.minimum(n + 1, n_groups - 1)
        for c in k_copies(nxt, 1 - k_slot) + v_copies(nxt, (n + 1) % 3):
            c.start()

    def decode_first_half(n):
        ls2, stacked = _decode_qk(qs_ref[n // groups_per_seq], bias_col_ref[...] * LOG2E,
                                  [kbuf_ref[n % 2, g] for g in range(group)])
        dls_ref[...] = ls2
        dsum_ref[...] = _decode_suffix(stacked)

    def decode_previous(n):
        decode_second_half(jnp.maximum(n - 1, 0), (n + 2) % 3)

    @pl.when(first_step)
    def _():
        for ref in (accs_ref, rs_ref, dls_ref, dsum_ref, os_ref):
            ref[...] = jnp.zeros_like(ref)
        for c in k_copies(0, 0) + v_copies(0, 0) + v_copies(0, 2):
            c.start()

    bias2 = bias_ref[h] * LOG2E
    upper = _strict_upper(tk)
    nsub = th // tk
    half_rows = [slice(half * th, (half + 1) * th) for half in range(2)]
    even = 1 - i % 2
    group_base = h * groups_per_head + (i * (i - 1)) // 2 + (i + 1) // 2

    def qk_stage(half, kt, masked):
        ks = pl.multiple_of(kt * th, th)
        z2 = lax.dot_general(q_ref[half_rows[half], :], k_ref[pl.ds(ks, th), :],
                             (((1,), (1,)), ((), ())), preferred_element_type=F32) + bias2
        sp2 = _softplus_2(z2)
        ls2 = z2 - sp2
        valid = None
        if masked:
            t_pos = lax.broadcasted_iota(jnp.int32, (th, th), 0)
            s_pos = lax.broadcasted_iota(jnp.int32, (th, th), 1)
            valid = s_pos < t_pos
            sp2 = jnp.where(valid, sp2, 0.0)
        blocks = [sp2[:, c * tk:(c + 1) * tk] for c in range(nsub)]
        totals = [jnp.broadcast_to(jnp.sum(b, axis=-1, keepdims=True), (th, LANES)) for b in blocks]
        return ks, ls2, jnp.concatenate(blocks, axis=0).astype(BF16), totals, valid

    def suffix_stage(st):
        return jnp.dot(st[2], upper, preferred_element_type=F32)

    def pv_stage(st, later, run):
        ks, ls2, _, totals, valid = st
        ws = [None] * nsub
        for c in reversed(range(nsub)):
            offs = jnp.concatenate([run] * (tk // LANES), axis=1)
            ws[c] = jnp.exp2(ls2[:, c * tk:(c + 1) * tk] - later[c * th:(c + 1) * th] - offs)
            run = run + totals[c]
        w = jnp.concatenate(ws, axis=1)
        if valid is not None:
            w = jnp.where(valid, w, 0.0)
        return jnp.dot(w.astype(BF16), v_ref[pl.ds(ks, th), :], preferred_element_type=F32), run

    def diagonal(with_decode):
        if with_decode:
            decode_fetch(group_base)
        zero = jnp.zeros((th, LANES), F32)
        st = qk_stage(1, 2 * i + 1, True)
        pv_hi, run1 = pv_stage(st, suffix_stage(st), zero)
        st = qk_stage(0, 2 * i, True)
        pv0, run0 = pv_stage(st, suffix_stage(st), zero)
        if with_decode:
            decode_previous(group_base)
        st = qk_stage(1, 2 * i, False)
        pv_lo, run1 = pv_stage(st, suffix_stage(st), run1)
        if with_decode:
            decode_first_half(group_base)
        acc_ref[half_rows[0], :] = pv0
        acc_ref[half_rows[1], :] = pv_hi + pv_lo
        r_ref[half_rows[0], :] = run0
        r_ref[half_rows[1], :] = run1

    pl.when(even == 1)(functools.partial(diagonal, True))
    pl.when(even == 0)(functools.partial(diagonal, False))

    def body(it, carry):
        kt = 2 * i - 1 - 2 * it
        n = group_base + even + it
        decode_fetch(n)
        pvs = [None] * 4
        runs = [r_ref[rows, :] for rows in half_rows]
        for t, (half, ktile) in enumerate([(0, kt), (1, kt), (0, kt - 1), (1, kt - 1)]):
            st = qk_stage(half, ktile, False)
            pvs[t], runs[half] = pv_stage(st, suffix_stage(st), runs[half])
            if t == 2:
                decode_previous(n)
            if t == 3:
                decode_first_half(n)
        for half, rows in enumerate(half_rows):
            acc_ref[rows, :] += pvs[half] + pvs[half + 2]
            r_ref[rows, :] = runs[half]
        return carry

    lax.fori_loop(0, i, body, 0)
    o_ref[...] = acc_ref[...]

    @pl.when(last_step)
    def _():
        last = n_groups - 1
        for c in v_copies(last, last % 3):
            c.wait()
        decode_second_half(last, last % 3)
        for c in k_copies(last, n_groups % 2) + v_copies(last, n_groups % 3):
            c.wait()


def _sb_attention(qkv, q_s, cache_k, cache_v, page_table, bias, *, th, tk, group):
    s = qkv.shape[0]
    b = q_s.shape[0]
    page = cache_k.shape[1]
    n_pages = page_table.shape[1]
    tq = 2 * th
    steps = s // tq
    keys = group * page
    assert b * (n_pages // group) == N_HEADS * (steps * (steps - 1) // 2 + (steps + 1) // 2)
    grid_spec = pltpu.PrefetchScalarGridSpec(
        num_scalar_prefetch=1,
        grid=(N_HEADS, steps),
        in_specs=[
            pl.BlockSpec(memory_space=pltpu.SMEM),
            pl.BlockSpec((tq, HEAD_DIM), lambda h, i, pt: (i, h)),
            pl.BlockSpec((s, HEAD_DIM), lambda h, i, pt: (0, N_HEADS + h)),
            pl.BlockSpec((s, HEAD_DIM), lambda h, i, pt: (0, 2 * N_HEADS + h)),
            pl.BlockSpec((b, N_HEADS, HEAD_DIM), lambda h, i, pt: (0, 0, 0)),
            pl.BlockSpec((N_HEADS, 1), lambda h, i, pt: (0, 0)),
            pl.BlockSpec(memory_space=pl.ANY),
            pl.BlockSpec(memory_space=pl.ANY),
        ],
        out_specs=(pl.BlockSpec((tq, HEAD_DIM), lambda h, i, pt: (i, h)),
                   pl.BlockSpec((b, N_HEADS, HEAD_DIM), lambda h, i, pt: (0, 0, 0))),
        scratch_shapes=[
            pltpu.VMEM((tq, HEAD_DIM), F32),
            pltpu.VMEM((tq, LANES), F32),
            pltpu.VMEM((2, group, page, N_HEADS, HEAD_DIM), F32),
            pltpu.VMEM((3, group, page, N_HEADS, HEAD_DIM), F32),
            pltpu.SemaphoreType.DMA((2,)),
            pltpu.SemaphoreType.DMA((3,)),
            pltpu.VMEM((N_HEADS, HEAD_DIM), F32),
            pltpu.VMEM((N_HEADS, LANES), F32),
            pltpu.VMEM((N_HEADS, keys * N_HEADS), F32),
            pltpu.VMEM((keys * N_HEADS // LANES * N_HEADS, 2 * LANES), F32),
        ],
    )
    o_a, o_s = pl.pallas_call(
        functools.partial(_sb_attention_kernel, th=th, tk=tk, group=group),
        out_shape=(jax.ShapeDtypeStruct((s, W_A), F32),
                   jax.ShapeDtypeStruct((b, N_HEADS, HEAD_DIM), F32)),
        grid_spec=grid_spec,
        compiler_params=_params(("arbitrary", "arbitrary")),
        name="sb_attention",
    )(page_table, bias, qkv, qkv, qkv, q_s.reshape(b, N_HEADS, HEAD_DIM).astype(F32),
      bias.reshape(N_HEADS, 1), cache_k, cache_v)
    return o_a, o_s.reshape(b, W_A)


def _suffix_and_total(n):
    r = lax.broadcasted_iota(jnp.int32, (n, 2 * n), 0)
    c = lax.broadcasted_iota(jnp.int32, (n, 2 * n), 1)
    return jnp.where((r > c) | (c >= n), 1.0, 0.0).astype(BF16)


def _own_lanes():
    lane = lax.broadcasted_iota(jnp.int32, (N_HEADS, LANES), 1)
    row = lax.broadcasted_iota(jnp.int32, (N_HEADS, LANES), 0)
    return (lane % N_HEADS) == row


def _newest_first(n_tiles, tiles_per_page):
    return [g * tiles_per_page + j for g in range(n_tiles // tiles_per_page)
            for j in reversed(range(tiles_per_page))]


def _decode_qk(q, bias2, k_pages):
    page = k_pages[0].shape[0]
    flat = page * N_HEADS
    keys_per_tile = LANES // N_HEADS
    lane = lax.broadcasted_iota(jnp.int32, (N_HEADS, LANES), 1)
    row = lax.broadcasted_iota(jnp.int32, (N_HEADS, LANES), 0)
    at_key = [lane == c * N_HEADS + row for c in range(keys_per_tile)]
    tiles = []
    for kp in k_pages:
        cols = jnp.sum(kp * q[None], axis=-1, keepdims=True)
        for j in range(page // keys_per_tile):
            zt = jnp.zeros((N_HEADS, LANES), F32)
            for c in range(keys_per_tile):
                col = jnp.broadcast_to(cols[j * keys_per_tile + c], (N_HEADS, LANES))
                zt = jnp.where(at_key[c], col, zt)
            tiles.append(zt)
    z2 = jnp.concatenate(tiles, axis=1) + bias2
    sp2 = _softplus_2(z2)
    ls2 = z2 - sp2
    own_t = _own_lanes()
    order = _newest_first(z2.shape[1] // LANES, flat // LANES)
    stacked = jnp.concatenate(
        [jnp.where(own_t, sp2[:, t * LANES:(t + 1) * LANES], 0.0) for t in order], axis=0)
    hi = stacked.astype(BF16)
    lo = (stacked - hi.astype(F32)).astype(BF16)
    return ls2, jnp.concatenate([hi, lo], axis=0)


def _decode_suffix(stacked):
    s = jnp.dot(stacked, _suffix_and_total(LANES), preferred_element_type=F32)
    half = s.shape[0] // 2
    return s[:half] + s[half:]


def _decode_pv(ls2, sums, v_pages, acc, run):
    flat = v_pages[0].shape[0] * N_HEADS
    own_t = _own_lanes()
    n_tiles = ls2.shape[1] // LANES
    ws = [None] * n_tiles
    for n, t in enumerate(_newest_first(n_tiles, flat // LANES)):
        rows = slice(n * N_HEADS, (n + 1) * N_HEADS)
        later = sums[rows, :LANES] + run
        ws[t] = jnp.where(own_t, jnp.exp2(ls2[:, t * LANES:(t + 1) * LANES] - later), 0.0)
        run = run + sums[rows, LANES:]
    w = jnp.concatenate(ws, axis=1).astype(BF16)
    vm = jnp.concatenate([vp.reshape(flat, HEAD_DIM).astype(BF16) for vp in v_pages], axis=0)
    return acc + jnp.dot(w, vm, preferred_element_type=F32), run


def _pool_mix_gate(window_sums, p, count_inv, o_a, g, wp_ref, scale_ref, a_ref):
    a_ref[:, :W_A] = (o_a * _silu(g[:, :W_A])).astype(BF16)
    for gi in range(len(POOL_WINDOWS)):
        cols = slice(gi * G_B, (gi + 1) * G_B)
        pooled = window_sums[gi] * count_inv[gi] - p[:, cols]
        mixed = jnp.dot(pooled.astype(BF16), wp_ref[gi], preferred_element_type=F32)
        o_b = mixed * scale_ref[:, cols]
        a_ref[:, W_A + gi * G_B:W_A + (gi + 1) * G_B] = (
            o_b * _silu(g[:, W_A + gi * G_B:W_A + (gi + 1) * G_B])).astype(BF16)


def _ab_mix_prompt_kernel(p_ref, halo_ref, oa_ref, g_ref, wp_ref, scale_ref, a_ref, buf_ref, *, tm):
    i = pl.program_id(0)
    buf_ref[:HALO, :] = jnp.where(i > 0, halo_ref[...], 0.0)
    buf_ref[HALO:, :] = p_ref[...]
    pos = i * tm + lax.broadcasted_iota(jnp.int32, (tm, 1), 0)
    sums, invs = [], []
    for gi, wnd in enumerate(POOL_WINDOWS):
        cols = slice(gi * G_B, (gi + 1) * G_B)
        acc = buf_ref[:, cols]
        shift = 1
        while shift < wnd:
            acc = acc + pltpu.roll(acc, shift, 0)
            shift *= 2
        sums.append(acc[HALO:])
        invs.append(1.0 / jnp.minimum(wnd, pos + 1).astype(F32))
    _pool_mix_gate(sums, p_ref[...], invs, oa_ref[...], g_ref[...], wp_ref, scale_ref, a_ref)


def _ab_mix_prompt(z, o_a, w_pool, scale, *, tm):
    m = z.shape[0]
    halo_blocks = tm // HALO
    p_col = 3 * W_A // W_B
    g_col = (3 * W_A + W_B) // W_AB
    return pl.pallas_call(
        functools.partial(_ab_mix_prompt_kernel, tm=tm),
        out_shape=jax.ShapeDtypeStruct((m, W_AB), BF16),
        grid=(m // tm,),
        in_specs=[
            pl.BlockSpec((tm, W_B), lambda i: (i, p_col)),
            pl.BlockSpec((HALO, W_B), lambda i: (jnp.maximum(i * halo_blocks - 1, 0), p_col)),
            pl.BlockSpec((tm, W_A), lambda i: (i, 0)),
            pl.BlockSpec((tm, W_AB), lambda i: (i, g_col)),
            pl.BlockSpec((len(POOL_WINDOWS), G_B, G_B), lambda i: (0, 0, 0)),
            pl.BlockSpec((1, W_B), lambda i: (0, 0)),
        ],
        out_specs=pl.BlockSpec((tm, W_AB), lambda i: (i, 0)),
        scratch_shapes=[pltpu.VMEM((tm + HALO, W_B), F32)],
        compiler_params=_params(("parallel",)),
        name="ab_mix_prompt",
    )(z, z, o_a, z, w_pool, scale.reshape(1, W_B))


def _ab_mix_sample_kernel(p_ref, prev_ref, oa_ref, g_ref, wp_ref, scale_ref, a_ref):
    p = p_ref[...]
    sums, invs = [], []
    for gi, wnd in enumerate(POOL_WINDOWS):
        cols = slice(gi * G_B, (gi + 1) * G_B)
        acc = p[:, cols]
        for back in range(1, wnd):
            acc = acc + prev_ref[POOL_BUF - back, :, cols]
        sums.append(acc)
        invs.append(1.0 / wnd)
    _pool_mix_gate(sums, p, invs, oa_ref[...], g_ref[...], wp_ref, scale_ref, a_ref)


def _ab_mix_sample(p, prev_t, o_a, g, w_pool, scale):
    m = p.shape[0]
    return pl.pallas_call(
        _ab_mix_sample_kernel,
        out_shape=jax.ShapeDtypeStruct((m, W_AB), BF16),
        name="ab_mix_sample",
        compiler_params=pltpu.CompilerParams(vmem_limit_bytes=VMEM_LIMIT,
                                             allow_input_fusion=[True] * 6),
    )(p, prev_t, o_a, g, w_pool, scale.reshape(1, W_B))


def _out_proj_kernel(a_ref, w_ref, x_ref, y_ref, *wb_ref):
    w = w_ref[...]
    if wb_ref:
        w = w.astype(BF16)
        wb_ref[0][...] = w
    y_ref[...] = x_ref[...] + jnp.dot(a_ref[...], w, preferred_element_type=F32)


def _out_proj(a, w, x, *, tm, tn, cast_w=False):
    m, kdim = a.shape
    n = w.shape[1]
    assert not cast_w or m == tm
    out_shape = [jax.ShapeDtypeStruct((m, n), F32)]
    out_specs = [pl.BlockSpec((tm, tn), lambda i, j: (i, j))]
    if cast_w:
        out_shape.append(jax.ShapeDtypeStruct((kdim, n), BF16))
        out_specs.append(pl.BlockSpec((kdim, tn), lambda i, j: (0, j)))
    return pl.pallas_call(
        _out_proj_kernel,
        out_shape=tuple(out_shape),
        grid=(m // tm, n // tn),
        in_specs=[
            pl.BlockSpec((tm, kdim), lambda i, j: (i, 0)),
            pl.BlockSpec((kdim, tn), lambda i, j: (0, j)),
            pl.BlockSpec((tm, tn), lambda i, j: (i, j)),
        ],
        out_specs=tuple(out_specs),
        compiler_params=_params(("parallel", "parallel")),
        name="out_proj",
    )(a, w, x)


def _c_gate_prompt_kernel(u_ref, vg_ref, g_ref, vgain_ref, ws_ref, bst_ref, a_ref, vrow_ref, *,
                          chunks):
    gc = u_ref.shape[1] // H_C
    r = lax.broadcasted_iota(jnp.int32, (CHUNK_C, CHUNK_C), 0)
    c = lax.broadcasted_iota(jnp.int32, (CHUNK_C, CHUNK_C), 1)
    lower = r >= c
    for ch in range(chunks):
        rows = slice(ch * CHUNK_C, (ch + 1) * CHUNK_C)
        v = _rms_rows(vg_ref[rows, :], vgain_ref[...])
        if ch == chunks - 1:
            vrow_ref[...] = v
        for hd in range(H_C):
            cols = slice(hd * gc, (hd + 1) * gc)
            ws = jnp.where(lower, ws_ref[hd], 0.0).astype(BF16)
            mixed = jnp.dot(ws, v[:, cols].astype(BF16), preferred_element_type=F32)
            mixed = mixed + bst_ref[:, hd:hd + 1]
            a_ref[rows, cols] = (u_ref[rows, cols] * mixed * _silu(g_ref[rows, cols])).astype(BF16)


def _c_gate_prompt(z, v_gain, w_s, b_s, *, chunks):
    m = z.shape[0]
    wc = z.shape[1] // 3
    tm = chunks * CHUNK_C
    return pl.pallas_call(
        functools.partial(_c_gate_prompt_kernel, chunks=chunks),
        out_shape=(jax.ShapeDtypeStruct((m, wc), BF16),
                   jax.ShapeDtypeStruct((CHUNK_C, wc), F32)),
        grid=(m // tm,),
        in_specs=[
            pl.BlockSpec((tm, wc), lambda i: (i, 0)),
            pl.BlockSpec((tm, wc), lambda i: (i, 1)),
            pl.BlockSpec((tm, wc), lambda i: (i, 2)),
            pl.BlockSpec((1, wc), lambda i: (0, 0)),
            pl.BlockSpec((H_C, CHUNK_C, CHUNK_C), lambda i: (0, 0, 0)),
            pl.BlockSpec((CHUNK_C, H_C), lambda i: (0, 0)),
        ],
        out_specs=(pl.BlockSpec((tm, wc), lambda i: (i, 0)),
                   pl.BlockSpec((CHUNK_C, wc), lambda i: (0, 0))),
        compiler_params=_params(("arbitrary",)),
        name="c_gate_prompt",
    )(z, z, z, v_gain.reshape(1, wc), w_s, b_s.T)


def _c_gate_sample_kernel(u_ref, vg_ref, g_ref, vgain_ref, w00_ref, b0_ref, a_ref, vrow_ref):
    v = _rms_rows(vg_ref[...], vgain_ref[...])
    vrow_ref[...] = v
    mixed = w00_ref[...] * v + b0_ref[...]
    a_ref[...] = (u_ref[...] * mixed * _silu(g_ref[...])).astype(BF16)


def _c_gate_sample(u, vg, g, v_gain, w_s, b_s):
    m, wc = u.shape
    gc = wc // H_C
    w00 = jnp.repeat(w_s[:, 0, 0], gc).reshape(1, wc)
    b0 = jnp.repeat(b_s[:, 0], gc).reshape(1, wc)
    return pl.pallas_call(
        _c_gate_sample_kernel,
        out_shape=(jax.ShapeDtypeStruct((m, wc), BF16), jax.ShapeDtypeStruct((m, wc), F32)),
        name="c_gate_sample",
        compiler_params=pltpu.CompilerParams(allow_input_fusion=[True] * 6),
    )(u, vg, g, v_gain.reshape(1, wc), w00, b0)


def kernel(x_prompt, x_sample, cache_k, cache_v, state_pool, page_table, norm_gain, w_in_ab, q_gain,
           k_gain, sb_bias, w_pool, pool_scale, w_out_ab, w_in_c, v_gain, w_spatial, b_spatial,
           w_out_c):
    bp, s, d = x_prompt.shape
    bs, t, _ = x_sample.shape
    assert bp == 1 and t == 1
    xp = x_prompt.reshape(s, d)
    xs = x_sample.reshape(bs, d)

    w_pool0 = w_pool[0].astype(BF16)
    colgain = jnp.concatenate([jnp.tile(q_gain[0] * (SM_SCALE * LOG2E), N_HEADS),
                               jnp.tile(k_gain[0], N_HEADS),
                               jnp.ones((W_A + W_B + W_AB,), F32)])
    n0 = 3 * W_A + W_B + W_AB
    ab_in = functools.partial(_in_proj, gain=norm_gain[0], tn=IN_PROJ_TN,
                              outs=((n0, F32), (3 * W_A, BF16)), colgain=colgain,
                              norm_cols=2 * W_A, name="ab_in_proj")
    c_in = functools.partial(_in_proj, gain=norm_gain[1], tn=IN_PROJ_TN, outs=((3 * d, F32),),
                             gelu_cols=2 * d, name="c_in_proj")

    z0s, qkv_s, w_in0 = ab_in(xs, w=w_in_ab[0], tm=bs, cast_w=True)
    z0, qkv = ab_in(xp, w=w_in0, tm=IN_PROJ_TM)
    o_a, oa_s = _sb_attention(qkv, qkv_s[:, :W_A], cache_k[0], cache_v[0], page_table, sb_bias[0],
                              th=ATTN_HALF_ROWS, tk=ATTN_SUFFIX_BLOCK, group=DECODE_PAGES_PER_GROUP)

    ps = z0s[:, 3 * W_A:3 * W_A + W_B]
    prev_t = jnp.swapaxes(state_pool[0], 0, 1)
    a_s = _ab_mix_sample(ps, prev_t, oa_s, z0s[:, 3 * W_A + W_B:], w_pool0, pool_scale[0])
    xs1, w_out0 = _out_proj(a_s, w_out_ab[0], xs, tm=bs, tn=OUT_PROJ_TN, cast_w=True)
    a = _ab_mix_prompt(z0, o_a, w_pool0, pool_scale[0], tm=AB_MIX_TM)
    (xp1,) = _out_proj(a, w_out0, xp, tm=OUT_PROJ_TM, tn=OUT_PROJ_TN)
    new_k_p = z0[:, W_A:2 * W_A].reshape(1, bp, s, N_HEADS, HEAD_DIM)
    new_v_p = z0[:, 2 * W_A:3 * W_A].reshape(1, bp, s, N_HEADS, HEAD_DIM)
    new_pool_p = z0[s - POOL_BUF:, 3 * W_A:3 * W_A + W_B].reshape(1, bp, POOL_BUF, W_B)
    new_k_s = z0s[:, W_A:2 * W_A].reshape(1, bs, t, N_HEADS, HEAD_DIM)
    new_v_s = z0s[:, 2 * W_A:3 * W_A].reshape(1, bs, t, N_HEADS, HEAD_DIM)
    new_pool_s = jnp.concatenate([state_pool[0][:, 1:], ps[:, None, :]], axis=1)[None]

    z1s, w_in1 = c_in(xs1, w=w_in_c[0], tm=bs, cast_w=True)
    (z1,) = c_in(xp1, w=w_in1, tm=IN_PROJ_TM)
    a1s, v_s = _c_gate_sample(z1s[:, :d], z1s[:, d:2 * d], z1s[:, 2 * d:], v_gain[0], w_spatial[0],
                              b_spatial[0])
    ys, w_out1 = _out_proj(a1s, w_out_c[0], xs1, tm=bs, tn=OUT_PROJ_TN, cast_w=True)
    a1, v_last = _c_gate_prompt(z1, v_gain[0], w_spatial[0], b_spatial[0], chunks=C_GATE_CHUNKS)
    (yp,) = _out_proj(a1, w_out1, xp1, tm=OUT_PROJ_TM, tn=OUT_PROJ_TN)

    return (yp.reshape(bp, s, d), ys.reshape(bs, t, d), new_k_p, new_v_p, new_k_s, new_v_s,
            new_pool_p, new_pool_s, v_last.reshape(1, bp, CHUNK_C, d), v_s.reshape(1, bs, t, d))
```
